```python
import jax, jax.numpy as jnp
from jax import lax
import numpy as np

D_MODEL = 1024
BATCH = 32
SEQ = 256
DEPTH = 2
DEC_BATCH = 8
DEC_SEQ = 2048
PAST_LEN = 512

GRID_W = 64
CHUNK = 128
Q_BLOCK = 128
A_WIDTH = D_MODEL // 2
A_GROUPS = 4
A_GC = A_WIDTH // A_GROUPS
HEAD_DIM = 64
N_Q = (D_MODEL // 2) // HEAD_DIM
N_KV = N_Q // 4
GQA = N_Q // N_KV
Q_W = N_Q * HEAD_DIM
KV_W = N_KV * HEAD_DIM
ROPE_THETA = 10000.0
C_HEAD = 64
C_WIDTH = D_MODEL // 2
C_HEADS = C_WIDTH // C_HEAD
DECAY_LORA = 64
AAA_LORA = 64
GATE_LORA = 128
N_DIR = 2
RWKV_FEAT = 3 * C_WIDTH + DECAY_LORA + AAA_LORA
GN_EPS = 64e-5
N_BRANCH = 3
D_FF = 4 * D_MODEL
ALPHA = (2 * DEPTH) ** 0.25
BETA = (8 * DEPTH) ** -0.25
IN_SIZES = (A_WIDTH, A_WIDTH, Q_W, KV_W, KV_W, 3 * C_WIDTH, N_DIR * (DECAY_LORA + AAA_LORA), GATE_LORA, N_BRANCH * D_MODEL)
IN_COLS = A_WIDTH * 2 + Q_W + 2 * KV_W + 3 * C_WIDTH + N_DIR * (DECAY_LORA + AAA_LORA) + GATE_LORA + N_BRANCH * D_MODEL

kernel_name = 'hybrid_diffusion_gmlp_gqa_rwkv7_step'


def _split_cols(x, sizes):
    out, start = [], 0
    for s in sizes:
        out.append(x[..., start:start + s])
        start += s
    return out


def _layer_norm(x, g, b, eps=1e-5):
    xf = x.astype(jnp.float32)
    mu = jnp.mean(xf, -1, keepdims=True)
    var = jnp.mean(jnp.square(xf - mu), -1, keepdims=True)
    return ((xf - mu) * lax.rsqrt(var + eps)).astype(x.dtype) * g + b


def _rms_norm(x, g, eps=1e-6):
    xf = x.astype(jnp.float32)
    return (xf * lax.rsqrt(jnp.mean(xf * xf, -1, keepdims=True) + eps)).astype(x.dtype) * g


def _axial_rope(x):
    L = x.shape[1]
    rows = L // GRID_W
    row = jnp.repeat(jnp.arange(rows, dtype=jnp.float32), GRID_W)
    col = jnp.tile(jnp.arange(GRID_W, dtype=jnp.float32), rows)
    half = HEAD_DIM // 2
    inv_freq = ROPE_THETA ** (-jnp.arange(0, half, 2, dtype=jnp.float32) / half)
    xf = x.astype(jnp.float32)

    def rot(xp, pos):
        ang = pos[:, None] * inv_freq[None, :]
        cos = jnp.cos(ang)[None, :, None, :]
        sin = jnp.sin(ang)[None, :, None, :]
        x1, x2 = xp[..., :half // 2], xp[..., half // 2:]
        return jnp.concatenate([x1 * cos - x2 * sin, x2 * cos + x1 * sin], -1)

    out = jnp.concatenate([rot(xf[..., :half], row), rot(xf[..., half:], col)], -1)
    return out.astype(x.dtype)


def _blocked_attention(q, k, v):
    B, Lq = q.shape[0], q.shape[1]
    nb = Lq // Q_BLOCK
    qb = jnp.moveaxis(q.reshape(B, nb, Q_BLOCK, N_KV, GQA, HEAD_DIM), 1, 0)
    scale = HEAD_DIM ** -0.5

    def one_block(qblk):
        s = jnp.einsum('bqhgd,bkhd->bhgqk', qblk, k).astype(jnp.float32) * scale
        p = jax.nn.softmax(s, axis=-1).astype(v.dtype)
        return jnp.einsum('bhgqk,bkhd->bqhgd', p, v)

    o = lax.map(one_block, qb)
    return jnp.moveaxis(o, 0, 1).reshape(B, Lq, Q_W)


def _chunk_spatial_gate(u, v, ln_g, ln_b, w_s, b_s):
    B, L, _ = u.shape
    vn = _layer_norm(v, ln_g, ln_b).reshape(B, L // CHUNK, CHUNK, A_GROUPS, A_GC)
    s = jnp.einsum('gpq,bnqgc->bnpgc', w_s, vn) + b_s.T[None, None, :, :, None]
    return u * s.reshape(B, L, A_WIDTH)


def _token_shift(f, mu, reverse):
    zero = jnp.zeros_like(f[:, :1])
    nb = jnp.concatenate([f[:, 1:], zero], 1) if reverse else jnp.concatenate([zero, f[:, :-1]], 1)
    return f + mu * (nb - f)


def _wkv_scan(r, w, k, v, a, b, s0, reverse):
    xs = tuple(jnp.moveaxis(t.astype(jnp.float32), 1, 0) for t in (r, w, k, v, a, b))

    def step(S, xt):
        r_t, w_t, k_t, v_t, a_t, b_t = xt
        sa = jnp.einsum('bhvk,bhk->bhv', S, a_t)
        S = S * w_t[:, :, None, :] + sa[..., None] * b_t[:, :, None, :] + v_t[..., None] * k_t[:, :, None, :]
        return S, jnp.einsum('bhvk,bhk->bhv', S, r_t)

    s_fin, ys = lax.scan(step, s0.astype(jnp.float32), xs, reverse=reverse)
    return jnp.moveaxis(ys, 0, 1).astype(r.dtype), s_fin


def _rwkv_mix(rkv, lora, g_down, s0, P, l):
    B, L, _ = rkv.shape
    hs = lambda t: t.reshape(B, L, C_HEADS, C_HEAD)
    k_k = P['rwkv_k_k'][l].reshape(C_HEADS, C_HEAD)
    k_a = P['rwkv_k_a'][l].reshape(C_HEADS, C_HEAD)
    y_sum, bonus_sum, finals = 0.0, 0.0, []
    for d in range(N_DIR):
        f = _token_shift(jnp.concatenate([rkv, lora[:, :, d]], -1), P['rwkv_mu'][l, d], reverse=(d == 1))
        r, k, v, wd, ad = _split_cols(f, (C_WIDTH, C_WIDTH, C_WIDTH, DECAY_LORA, AAA_LORA))
        w_log = (-jax.nn.softplus(-(P['rwkv_w0'][l, d] + jnp.tanh(wd) @ P['rwkv_w2'][l, d])) - 0.5).astype(jnp.float32)
        decay = hs(jnp.exp(-jnp.exp(w_log)))
        a = hs(jax.nn.sigmoid(P['rwkv_a0'][l, d] + ad @ P['rwkv_a2'][l, d]))
        r, k, v = hs(r), hs(k), hs(v)
        kk = k * k_k
        kk = kk / jnp.maximum(jnp.sqrt(jnp.sum(jnp.square(kk.astype(jnp.float32)), -1, keepdims=True)), 1e-12).astype(kk.dtype)
        k_mod = k * (1 + (a - 1) * k_a)
        y, s_fin = _wkv_scan(r, decay, k_mod, v, -kk, kk * a, s0[:, d], reverse=(d == 1))
        y_sum = y_sum + y
        bonus_sum = bonus_sum + jnp.sum(r * k_mod * P['rwkv_r_k'][l], -1, keepdims=True) * v
        finals.append(s_fin)
    yf = y_sum.astype(jnp.float32)
    mu = jnp.mean(yf, -1, keepdims=True)
    var = jnp.mean(jnp.square(yf - mu), -1, keepdims=True)
    gn = ((yf - mu) * lax.rsqrt(var + GN_EPS)).astype(rkv.dtype).reshape(B, L, C_WIDTH)
    gn = gn * P['rwkv_lnx_g'][l] + P['rwkv_lnx_b'][l]
    g = jax.nn.sigmoid(g_down) @ P['rwkv_g2'][l]
    out = (gn + bonus_sum.reshape(B, L, C_WIDTH)) * g
    return out, jnp.stack(finals, axis=1)


def _trunk_layer(x, cvec, P, l, ctx):
    B, L, _ = x.shape
    latent = ctx is not None
    mod = (jax.nn.silu(cvec) @ P['w_ada'][l] + P['b_ada'][l])[:, None, :]
    sh1, sc1, g1, sh2, sc2, g2 = jnp.split(mod, 6, axis=-1)
    h = x * (1 + sc1) + sh1
    proj = h @ P['w_in'][l]
    uA, vA, q, k, v, rkv, lora, g_down, g_log = _split_cols(proj, IN_SIZES)
    oA = _chunk_spatial_gate(uA, vA, P['sgu_ln_g'][l], P['sgu_ln_b'][l], P['sgu_w'][l], P['sgu_b'][l])
    q = _rms_norm(q.reshape(B, L, N_Q, HEAD_DIM), P['q_norm'][l])
    k = _rms_norm(k.reshape(B, L, N_KV, HEAD_DIM), P['k_norm'][l])
    v = v.reshape(B, L, N_KV, HEAD_DIM)
    if latent:
        ctx_k, ctx_v, ctx_s = ctx
        q_r, k_r = _axial_rope(q), _axial_rope(k)
        oB = _blocked_attention(q_r, jnp.concatenate([ctx_k, k_r], 1), jnp.concatenate([ctx_v, v], 1))
        s0 = ctx_s
    else:
        oB = _blocked_attention(q, k, v)
        s0 = jnp.zeros((B, N_DIR, C_HEADS, C_HEAD, C_HEAD), jnp.float32)
    oC, s_fin = _rwkv_mix(rkv, lora.reshape(B, L, N_DIR, DECAY_LORA + AAA_LORA), g_down, s0, P, l)
    branches = jnp.stack([oA, oB, oC], axis=2)
    p = jnp.einsum('bljc,jcd->bljd', branches, P['w_branch'][l])
    gates = jax.nn.sigmoid(g_log.reshape(B, L, N_BRANCH, D_MODEL))
    mixed = jnp.sum(gates * p, axis=2) @ P['w_out'][l]
    x = _layer_norm(ALPHA * x + g1 * mixed, P['ln1_g'][l], P['ln1_b'][l])
    h = x * (1 + sc2) + sh2
    f = jnp.square(jax.nn.relu(h @ P['w_up'][l])) @ P['w_down'][l]
    x = _layer_norm(ALPHA * x + g2 * f, P['ln2_g'][l], P['ln2_b'][l])
    if latent:
        return x
    return x, (k, v, s_fin.astype(x.dtype))


def setup_inputs(seed: int = 0) -> dict:
    key = jax.random.key(seed)
    ks = iter(jax.random.split(key, 48))
    nrm = lambda shape, s=1.0: s * jax.random.normal(next(ks), shape, jnp.float32)
    return {
        'x_prompt': nrm((BATCH, SEQ, D_MODEL)),
        'x_sample': nrm((DEC_BATCH, DEC_SEQ, D_MODEL)),
        'cache_k': nrm((DEC_BATCH, DEPTH, PAST_LEN, N_KV, HEAD_DIM)),
        'cache_v': nrm((DEC_BATCH, DEPTH, PAST_LEN, N_KV, HEAD_DIM)),
        'state_wkv': nrm((DEC_BATCH, DEPTH, N_DIR, C_HEADS, C_HEAD, C_HEAD), 0.5),
        'c': nrm((DEC_BATCH, D_MODEL)),
        'c_ctx': nrm((D_MODEL,)),
        'w_ada': nrm((DEPTH, D_MODEL, 6 * D_MODEL), 0.5 * D_MODEL ** -0.5),
        'b_ada': nrm((DEPTH, 6 * D_MODEL), 0.01),
        'w_in': nrm((DEPTH, D_MODEL, IN_COLS), D_MODEL ** -0.5),
        'sgu_ln_g': 1.0 + nrm((DEPTH, A_WIDTH), 0.02),
        'sgu_ln_b': nrm((DEPTH, A_WIDTH), 0.02),
        'sgu_w': nrm((DEPTH, A_GROUPS, CHUNK, CHUNK), 0.5 * CHUNK ** -0.5),
        'sgu_b': 1.0 + nrm((DEPTH, A_GROUPS, CHUNK), 0.02),
        'q_norm': 1.0 + nrm((DEPTH, HEAD_DIM), 0.02),
        'k_norm': 1.0 + nrm((DEPTH, HEAD_DIM), 0.02),
        'rwkv_mu': jax.random.uniform(next(ks), (DEPTH, N_DIR, RWKV_FEAT), jnp.float32),
        'rwkv_w0': nrm((DEPTH, N_DIR, C_WIDTH), 0.5),
        'rwkv_w2': nrm((DEPTH, N_DIR, DECAY_LORA, C_WIDTH), 0.3 * DECAY_LORA ** -0.5),
        'rwkv_a0': nrm((DEPTH, N_DIR, C_WIDTH), 0.1),
        'rwkv_a2': nrm((DEPTH, N_DIR, AAA_LORA, C_WIDTH), 0.5 * AAA_LORA ** -0.5),
        'rwkv_k_k': 0.85 + nrm((DEPTH, C_WIDTH), 0.02),
        'rwkv_k_a': 1.0 + nrm((DEPTH, C_WIDTH), 0.02),
        'rwkv_r_k': nrm((DEPTH, C_HEADS, C_HEAD), 0.1),
        'rwkv_g2': nrm((DEPTH, GATE_LORA, C_WIDTH), GATE_LORA ** -0.5),
        'rwkv_lnx_g': 1.0 + nrm((DEPTH, C_WIDTH), 0.02),
        'rwkv_lnx_b': nrm((DEPTH, C_WIDTH), 0.02),
        'w_branch': nrm((DEPTH, N_BRANCH, C_WIDTH, D_MODEL), BETA * C_WIDTH ** -0.5),
        'w_out': nrm((DEPTH, D_MODEL, D_MODEL), BETA * D_MODEL ** -0.5),
        'ln1_g': 1.0 + nrm((DEPTH, D_MODEL), 0.02),
        'ln1_b': nrm((DEPTH, D_MODEL), 0.02),
        'w_up': nrm((DEPTH, D_MODEL, D_FF), D_MODEL ** -0.5),
        'w_down': nrm((DEPTH, D_FF, D_MODEL), BETA * D_FF ** -0.5),
        'ln2_g': 1.0 + nrm((DEPTH, D_MODEL), 0.02),
        'ln2_b': nrm((DEPTH, D_MODEL), 0.02),
    }


def reference(x_prompt, x_sample, cache_k, cache_v, state_wkv, c, c_ctx, w_ada, b_ada, w_in, sgu_ln_g, sgu_ln_b, sgu_w, sgu_b, q_norm, k_norm, rwkv_mu, rwkv_w0, rwkv_w2, rwkv_a0, rwkv_a2, rwkv_k_k, rwkv_k_a, rwkv_r_k, rwkv_g2, rwkv_lnx_g, rwkv_lnx_b, w_branch, w_out, ln1_g, ln1_b, w_up, w_down, ln2_g, ln2_b):
    P = dict(w_ada=w_ada, b_ada=b_ada, w_in=w_in, sgu_ln_g=sgu_ln_g, sgu_ln_b=sgu_ln_b, sgu_w=sgu_w, sgu_b=sgu_b,
             q_norm=q_norm, k_norm=k_norm, rwkv_mu=rwkv_mu, rwkv_w0=rwkv_w0, rwkv_w2=rwkv_w2, rwkv_a0=rwkv_a0,
             rwkv_a2=rwkv_a2, rwkv_k_k=rwkv_k_k, rwkv_k_a=rwkv_k_a, rwkv_r_k=rwkv_r_k, rwkv_g2=rwkv_g2,
             rwkv_lnx_g=rwkv_lnx_g, rwkv_lnx_b=rwkv_lnx_b, w_branch=w_branch, w_out=w_out, ln1_g=ln1_g,
             ln1_b=ln1_b, w_up=w_up, w_down=w_down, ln2_g=ln2_g, ln2_b=ln2_b)
    y = x_prompt
    ks, vs, ss = [], [], []
    for l in range(DEPTH):
        y, (k_l, v_l, s_l) = _trunk_layer(y, c_ctx[None, :], P, l, None)
        ks.append(k_l)
        vs.append(v_l)
        ss.append(s_l)
    new_cache_k = jnp.stack(ks, axis=1)
    new_cache_v = jnp.stack(vs, axis=1)
    new_state_wkv = jnp.stack(ss, axis=1)
    z = x_sample
    for l in range(DEPTH):
        z = _trunk_layer(z, c, P, l, (cache_k[:, l], cache_v[:, l], state_wkv[:, l]))
    return (y, z, new_cache_k, new_cache_v, new_state_wkv)
```

```python
import functools
import math

import jax
import jax.numpy as jnp
from jax import lax
from jax.experimental import pallas as pl
from jax.experimental.pallas import tpu as pltpu

F32 = jnp.float32
BF16 = jnp.bfloat16

D_MODEL = 1024
HALF = D_MODEL // 2
HEAD = 64
N_HEADS = HALF // HEAD
N_KV = 2
GQA = N_HEADS // N_KV
KV_W = N_KV * HEAD
GRID_W = 64
SGU_CHUNK = 128
SGU_GROUPS = 4
LORA = 64
GATE_LORA = 128
D_FF = 4 * D_MODEL
ROPE_THETA = 10000.0
GN_EPS = 64e-5
SEG_A = 2 * HALF
SEG_B = HALF + 2 * KV_W
SEG_C = 3 * HALF + 4 * LORA + GATE_LORA
SEG_G = 3 * D_MODEL
RKV_W = 3 * HALF
LANES = 128
SUBLANES = 8
VMEM_LIMIT = 48 * 1024 * 1024


def _cparams(*sem):
    return pltpu.CompilerParams(dimension_semantics=sem, vmem_limit_bytes=VMEM_LIMIT)


def _dot(a, b):
    return jnp.dot(a, b, preferred_element_type=F32)


def _dot_nt(a, b):
    return lax.dot_general(a, b, (((1,), (1,)), ((), ())), preferred_element_type=F32)


def _dot_tn(a, b):
    return lax.dot_general(a, b, (((0,), (0,)), ((), ())), preferred_element_type=F32)


def _layer_norm(x, g, b, eps=1e-5):
    mu = jnp.mean(x, axis=-1, keepdims=True)
    xc = x - mu
    var = jnp.mean(xc * xc, axis=-1, keepdims=True)
    return xc * lax.rsqrt(var + eps) * g + b


def _ada_kernel(c_ref, w_ref, b_ref, o_ref):
    c = c_ref[...]
    s = (c * jax.nn.sigmoid(c)).astype(BF16)
    o_ref[0] = _dot(s, w_ref[0]) + b_ref[0]


def _ada(cvec, w_ada, b_ada):
    depth, _, n = w_ada.shape
    r = cvec.shape[0]
    tn = 1536
    return pl.pallas_call(
        _ada_kernel,
        grid=(depth, n // tn),
        in_specs=[pl.BlockSpec((r, D_MODEL), lambda l, j: (0, 0)),
                  pl.BlockSpec((1, D_MODEL, tn), lambda l, j: (l, 0, j)),
                  pl.BlockSpec((1, 1, tn), lambda l, j: (l, 0, j))],
        out_specs=pl.BlockSpec((1, r, tn), lambda l, j: (l, 0, j)),
        out_shape=jax.ShapeDtypeStruct((depth, r, n), F32),
        compiler_params=_cparams("parallel", "parallel"),
        name="ada",
    )(cvec, w_ada, b_ada)


def _mod_row(i, tm, t_ctx, l_lat):
    r = i * tm
    return jnp.where(r < t_ctx, 0, 1 + (r - t_ctx) // l_lat)


def _modmm_kernel(x_ref, sh_ref, sc_ref, w_ref, o_ref, h_ref):
    @pl.when(pl.program_id(1) == 0)
    def _():
        h_ref[...] = (x_ref[...] * (1.0 + sc_ref[0]) + sh_ref[0]).astype(BF16)

    o_ref[...] = _dot(h_ref[...], w_ref[...]).astype(o_ref.dtype)


def _modmm(x, mod, sh_blk, sc_blk, w, tm, t_ctx, l_lat, out_dtype=F32):
    t = x.shape[0]
    n = w.shape[1]
    tn = n if n <= 2048 else n // 2
    row = functools.partial(_mod_row, tm=tm, t_ctx=t_ctx, l_lat=l_lat)
    return pl.pallas_call(
        _modmm_kernel,
        grid=(t // tm, n // tn),
        in_specs=[pl.BlockSpec((tm, D_MODEL), lambda i, j: (i, 0)),
                  pl.BlockSpec((1, 1, D_MODEL), lambda i, j: (row(i), 0, sh_blk)),
                  pl.BlockSpec((1, 1, D_MODEL), lambda i, j: (row(i), 0, sc_blk)),
                  pl.BlockSpec((D_MODEL, tn), lambda i, j: (0, j))],
        out_specs=pl.BlockSpec((tm, tn), lambda i, j: (i, j)),
        out_shape=jax.ShapeDtypeStruct((t, n), out_dtype),
        scratch_shapes=[pltpu.VMEM((tm, D_MODEL), BF16)],
        compiler_params=_cparams("parallel", "arbitrary"),
        name="modmm",
    )(x, mod, mod, w)


def _sgu_kernel(uv_ref, g_ref, b_ref, ws_ref, bs_ref, o_ref, *, tm):
    v = uv_ref[:, HALF:]
    vn = _layer_norm(v, g_ref[...], b_ref[...]).astype(BF16)
    gc = HALF // SGU_GROUPS
    for n in range(tm // SGU_CHUNK):
        rows = slice(n * SGU_CHUNK, (n + 1) * SGU_CHUNK)
        for g in range(SGU_GROUPS):
            cols = slice(g * gc, (g + 1) * gc)
            s = _dot(ws_ref[g], vn[rows, cols]) + bs_ref[:, cols]
            o_ref[rows, cols] = (uv_ref[rows, cols] * s).astype(o_ref.dtype)


def _sgu(seg_a, ln_g, ln_b, w_s, b_s_full, tm):
    t = seg_a.shape[0]
    return pl.pallas_call(
        functools.partial(_sgu_kernel, tm=tm),
        grid=(t // tm,),
        in_specs=[pl.BlockSpec((tm, SEG_A), lambda i: (i, 0)),
                  pl.BlockSpec((1, HALF), lambda i: (0, 0)),
                  pl.BlockSpec((1, HALF), lambda i: (0, 0)),
                  pl.BlockSpec((SGU_GROUPS, SGU_CHUNK, SGU_CHUNK), lambda i: (0, 0, 0)),
                  pl.BlockSpec((SGU_CHUNK, HALF), lambda i: (0, 0))],
        out_specs=pl.BlockSpec((tm, HALF), lambda i: (i, 0)),
        out_shape=jax.ShapeDtypeStruct((t, HALF), BF16),
        compiler_params=_cparams("parallel"),
        name="sgu",
    )(seg_a, ln_g, ln_b, w_s, b_s_full)


def _rope_swap(x):
    lane = lax.broadcasted_iota(jnp.int32, x.shape, 1)
    up = pltpu.roll(x, LANES - 16, axis=1)
    dn = pltpu.roll(x, 16, axis=1)
    return jnp.where((lane & 16) == 0, up, dn)


def _head_rms(x, seg_ref, g):
    ms = _dot((x * x).astype(BF16), seg_ref[...])
    return x * lax.rsqrt(ms + 1e-6) * g


def _kvprep_kernel(k_ref, v_ref, g_ref, seg_ref, cos_ref, sin_ref, *out_refs, rope):
    kn = _head_rms(k_ref[...], seg_ref, g_ref[...])
    if rope:
        kr_ref, vb_ref = out_refs
        kn = kn * cos_ref[...] + _rope_swap(kn) * sin_ref[...]
    else:
        kn_ref, kr_ref, vb_ref = out_refs
        kn_ref[...] = kn
    kr_ref[...] = kn.astype(BF16)
    vb_ref[...] = v_ref[...].astype(BF16)


def _kvprep(seg_b, row_off, b, l, k_norm2, seg_mat, cos, sin, rope, tk):
    t = b * l
    off = row_off // tk
    lb = l // tk
    out_shape = [jax.ShapeDtypeStruct((t, KV_W), BF16), jax.ShapeDtypeStruct((t, KV_W), BF16)]
    out_specs = [pl.BlockSpec((tk, KV_W), lambda i: (i, 0)), pl.BlockSpec((tk, KV_W), lambda i: (i, 0))]
    if not rope:
        out_shape = [jax.ShapeDtypeStruct((t, KV_W), F32)] + out_shape
        out_specs = [pl.BlockSpec((tk, KV_W), lambda i: (i, 0))] + out_specs
    return pl.pallas_call(
        functools.partial(_kvprep_kernel, rope=rope),
        grid=(t // tk,),
        in_specs=[pl.BlockSpec((tk, KV_W), lambda i: (off + i, HALF // KV_W)),
                  pl.BlockSpec((tk, KV_W), lambda i: (off + i, HALF // KV_W + 1)),
                  pl.BlockSpec((1, KV_W), lambda i: (0, 0)),
                  pl.BlockSpec((KV_W, KV_W), lambda i: (0, 0)),
                  pl.BlockSpec((tk, KV_W), lambda i: (i % lb, 0)),
                  pl.BlockSpec((tk, KV_W), lambda i: (i % lb, 0))],
        out_specs=out_specs,
        out_shape=out_shape,
        compiler_params=_cparams("parallel"),
        name="kvprep",
    )(seg_b, seg_b, k_norm2, seg_mat, cos, sin)


def _attn_kernel(q_ref, g_ref, seg_ref, cos_ref, sin_ref, k_ref, v_ref, o_ref, *, rope, tq):
    qs = []
    for s in range(HALF // LANES):
        q = _head_rms(q_ref[:, s * LANES:(s + 1) * LANES], seg_ref, g_ref[...])
        if rope:
            q = q * cos_ref[...] + _rope_swap(q) * sin_ref[...]
        qs.append((q * HEAD ** -0.5).astype(BF16))
    k = k_ref[0]
    v = v_ref[0]
    for g in range(N_KV):
        kg = k[:, g * HEAD:(g + 1) * HEAD]
        vg = v[:, g * HEAD:(g + 1) * HEAD]
        heads = range(g * GQA, (g + 1) * GQA)
        qg = jnp.concatenate([qs[h // 2][:, (h % 2) * HEAD:(h % 2 + 1) * HEAD] for h in heads], axis=0)
        s = _dot_nt(qg, kg)
        e = jnp.exp(s - jnp.max(s, axis=-1, keepdims=True))
        o = _dot(e.astype(BF16), vg) / jnp.sum(e, axis=-1, keepdims=True)
        for j, h in enumerate(heads):
            o_ref[:, h * HEAD:(h + 1) * HEAD] = o[j * tq:(j + 1) * tq].astype(o_ref.dtype)


def _attention(seg_b, row_off, b, l, kfull, vfull, q_norm2, seg_mat, cos, sin, rope, tq):
    t = b * l
    off = row_off // tq
    lb = l // tq
    lk = kfull.shape[1]
    return pl.pallas_call(
        functools.partial(_attn_kernel, rope=rope, tq=tq),
        grid=(b, lb),
        in_specs=[pl.BlockSpec((tq, HALF), lambda bi, i: (off + bi * lb + i, 0)),
                  pl.BlockSpec((1, LANES), lambda bi, i: (0, 0)),
                  pl.BlockSpec((KV_W, KV_W), lambda bi, i: (0, 0)),
                  pl.BlockSpec((tq, LANES), lambda bi, i: (i, 0)),
                  pl.BlockSpec((tq, LANES), lambda bi, i: (i, 0)),
                  pl.BlockSpec((1, lk, KV_W), lambda bi, i: (bi, 0, 0)),
                  pl.BlockSpec((1, lk, KV_W), lambda bi, i: (bi, 0, 0))],
        out_specs=pl.BlockSpec((tq, HALF), lambda bi, i: (bi * lb + i, 0)),
        out_shape=jax.ShapeDtypeStruct((t, HALF), BF16),
        compiler_params=_cparams("parallel", "parallel"),
        name="attention",
    )(seg_b, q_norm2, seg_mat, cos, sin, kfull, vfull)


def _split3(x):
    h = x.astype(BF16)
    r1 = x - h.astype(F32)
    m = r1.astype(BF16)
    lo = (r1 - m.astype(F32)).astype(BF16)
    return h, m, lo


def _wkv_dir(x, nb_row, d, c, s_ref, mu_rkv, mu_lo, w0, a0, wwa, k_k, k_a, r_k, seg1, y_ref, rev):
    rows = lax.broadcasted_iota(jnp.int32, (c, 1), 0)
    edge = (c - 1) if rev else 0

    def shifted(cur, nb):
        rolled = pltpu.roll(cur, (c - 1) if rev else 1, axis=0)
        return jnp.where(rows == edge, nb, rolled)

    rkv = x[:, :RKV_W]
    lo = x[:, RKV_W + 2 * LORA * d:RKV_W + 2 * LORA * (d + 1)]
    f = rkv + mu_rkv * (shifted(rkv, nb_row[:, :RKV_W]) - rkv)
    fl = lo + mu_lo * (shifted(lo, nb_row[:, RKV_W + 2 * LORA * d:RKV_W + 2 * LORA * (d + 1)]) - lo)
    r = f[:, :HALF]
    k = f[:, HALF:2 * HALF]
    v = f[:, 2 * HALF:]
    lane = lax.broadcasted_iota(jnp.int32, fl.shape, 1)
    lin = _dot(jnp.where(lane < LORA, jnp.tanh(fl), fl).astype(BF16), wwa)
    lw = (-math.exp(-0.5)) * jax.nn.sigmoid(w0 + lin[:, :HALF])
    asig = jax.nn.sigmoid(a0 + lin[:, HALF:])
    kk = k * k_k
    ss = _dot((kk * kk).astype(BF16), seg1)
    kkn = kk / jnp.maximum(jnp.sqrt(ss), 1e-12)
    kmod = k * (1.0 + (asig - 1.0) * k_a)
    bonus = _dot((r * kmod * r_k).astype(BF16), seg1) * v
    y_ref[:, HALF:] = bonus

    ti = lax.broadcasted_iota(jnp.int32, (c, c), 0)
    si = lax.broadcasted_iota(jnp.int32, (c, c), 1)
    incl = (si >= ti) if rev else (si <= ti)
    strict = (si > ti) if rev else (si < ti)
    tri = incl.astype(BF16)
    h3, m3, l3 = _split3(lw)
    cum = _dot(tri, h3) + _dot(tri, m3) + _dot(tri, l3)
    ref = cum[c // 2:c // 2 + 1]
    end = 0 if rev else c - 1
    cum_end = cum[end:end + 1]
    g = cum - ref
    e_pos = jnp.exp(g)
    e_neg = jnp.exp(-g)
    e_ref = jnp.exp(ref)
    e_tot = jnp.exp(cum_end)
    e_end = jnp.exp(cum_end - ref)
    at_c = -kkn * jnp.exp(g - lw)
    rt_c = r * e_pos
    bt = kkn * asig * e_neg
    kt = kmod * e_neg
    at_true = (at_c * e_ref).astype(BF16)
    rt_true = (rt_c * e_ref).astype(BF16)
    bh = (bt * e_end).astype(BF16)
    kh = (kt * e_end).astype(BF16)
    at_c = at_c.astype(BF16)
    rt_c = rt_c.astype(BF16)
    bt = bt.astype(BF16)
    kt = kt.astype(BF16)
    vb = v.astype(BF16)

    for h in range(N_HEADS):
        sl = slice(h * HEAD, (h + 1) * HEAD)
        g1 = _dot_nt(jnp.concatenate([at_c[:, sl], rt_c[:, sl]], axis=0),
                     jnp.concatenate([bt[:, sl], kt[:, sl]], axis=0))
        n_pow = jnp.where(strict, g1[:c, :c], 0.0).astype(BF16)
        a_ak = jnp.where(strict, g1[:c, c:], 0.0).astype(BF16)
        m_rb = jnp.where(incl, g1[c:, :c], 0.0).astype(BF16)
        m_rk = jnp.where(incl, g1[c:, c:], 0.0).astype(BF16)
        xp = at_true[:, sl].astype(F32)
        xq = _dot(a_ak, vb[:, sl])
        steps = int(math.log2(c))
        for j in range(steps):
            xp = xp + _dot(n_pow, xp.astype(BF16))
            xq = xq + _dot(n_pow, xq.astype(BF16))
            if j + 1 < steps:
                n_pow = _dot(n_pow, n_pow).astype(BF16)
        s0 = s_ref[d, h]
        s0b = s0.astype(BF16)
        u = _dot_nt(xp.astype(BF16), s0b) + xq
        ub = u.astype(BF16)
        y = _dot_nt(rt_true[:, sl], s0b) + _dot(m_rb, ub) + _dot(m_rk, vb[:, sl])
        y_ref[:, sl] = y
        s_ref[d, h] = s0 * e_tot[:, sl] + _dot_tn(ub, bh[:, sl]) + _dot_tn(vb[:, sl], kh[:, sl])


def _wkv_kernel(*refs, c, n_c, latent):
    if latent:
        (x0_ref, x1_ref, p0_ref, n1_ref, s0_ref, mu_ref, w0_ref, a0_ref, wwa_ref, kk_ref, ka_ref, rk_ref,
         seg_ref, y0_ref, y1_ref, sf_ref, s_ref) = refs
    else:
        (x0_ref, x1_ref, p0_ref, n1_ref, mu_ref, w0_ref, a0_ref, wwa_ref, kk_ref, ka_ref, rk_ref,
         seg_ref, y0_ref, y1_ref, sf_ref, s_ref) = refs
    i = pl.program_id(1)

    @pl.when(i == 0)
    def _():
        s_ref[...] = s0_ref[0] if latent else jnp.zeros(s_ref.shape, F32)

    inner = (i > 0).astype(F32)
    for d, (x_ref, nb_ref, y_ref) in enumerate(((x0_ref, p0_ref, y0_ref), (x1_ref, n1_ref, y1_ref))):
        nb = nb_ref[SUBLANES - 1:SUBLANES, :] if d == 0 else nb_ref[0:1, :]
        _wkv_dir(x_ref[...], nb * inner, d, c, s_ref,
                 mu_ref[d, :, :RKV_W], mu_ref[d, :, RKV_W:], w0_ref[d], a0_ref[d], wwa_ref[d],
                 kk_ref[...], ka_ref[...], rk_ref[...], seg_ref[...], y_ref, rev=(d == 1))

    @pl.when(i == n_c - 1)
    def _():
        sf_ref[0] = s_ref[...]


def _wkv(seg_c, row_off, b, l, c, s0, mu, w0, a0, wwa, k_k, k_a, r_k, seg1):
    t = b * l
    n_c = l // c
    t_all = seg_c.shape[0]
    cb = row_off // c
    c8 = c // SUBLANES
    r8 = row_off // SUBLANES
    last8 = t_all // SUBLANES - 1
    latent = s0 is not None
    const2 = lambda bi, i: (0, 0)
    const3 = lambda bi, i: (0, 0, 0)
    in_specs = [pl.BlockSpec((c, SEG_C), lambda bi, i: (cb + bi * n_c + i, 0)),
                pl.BlockSpec((c, SEG_C), lambda bi, i: (cb + bi * n_c + n_c - 1 - i, 0)),
                pl.BlockSpec((SUBLANES, SEG_C),
                             lambda bi, i: (jnp.maximum(r8 + (bi * n_c + i) * c8 - 1, 0), 0)),
                pl.BlockSpec((SUBLANES, SEG_C),
                             lambda bi, i: (jnp.minimum(r8 + (bi * n_c + n_c - i) * c8, last8), 0))]
    args = [seg_c, seg_c, seg_c, seg_c]
    if latent:
        in_specs.append(pl.BlockSpec((1, 2, N_HEADS, HEAD, HEAD), lambda bi, i: (bi, 0, 0, 0, 0)))
        args.append(s0)
    in_specs += [pl.BlockSpec(mu.shape, const3), pl.BlockSpec(w0.shape, const3), pl.BlockSpec(a0.shape, const3),
                 pl.BlockSpec(wwa.shape, const3), pl.BlockSpec(k_k.shape, const2), pl.BlockSpec(k_a.shape, const2),
                 pl.BlockSpec(r_k.shape, const2), pl.BlockSpec(seg1.shape, const2)]
    args += [mu, w0, a0, wwa, k_k, k_a, r_k, seg1]
    return pl.pallas_call(
        functools.partial(_wkv_kernel, c=c, n_c=n_c, latent=latent),
        grid=(b, n_c),
        in_specs=in_specs,
        out_specs=[pl.BlockSpec((c, 2 * HALF), lambda bi, i: (bi * n_c + i, 0)),
                   pl.BlockSpec((c, 2 * HALF), lambda bi, i: (bi * n_c + n_c - 1 - i, 0)),
                   pl.BlockSpec((1, 2, N_HEADS, HEAD, HEAD), lambda bi, i: (bi, 0, 0, 0, 0))],
        out_shape=[jax.ShapeDtypeStruct((t, 2 * HALF), F32), jax.ShapeDtypeStruct((t, 2 * HALF), F32),
                   jax.ShapeDtypeStruct((b, 2, N_HEADS, HEAD, HEAD), F32)],
        scratch_shapes=[pltpu.VMEM((2, N_HEADS, HEAD, HEAD), F32)],
        compiler_params=_cparams("parallel", "arbitrary"),
        name="wkv",
    )(*args)


def _rwkv_post_kernel(y0_ref, y1_ref, gd_ref, segm_ref, g2_ref, lg_ref, lb_ref, o_ref):
    ys = y0_ref[:, :HALF] + y1_ref[:, :HALF]
    bonus = y0_ref[:, HALF:] + y1_ref[:, HALF:]
    mu = _dot(ys.astype(BF16), segm_ref[...])
    yc = ys - mu
    var = _dot((yc * yc).astype(BF16), segm_ref[...])
    gn = yc * lax.rsqrt(var + GN_EPS) * lg_ref[...] + lb_ref[...]
    gate = _dot(jax.nn.sigmoid(gd_ref[...]).astype(BF16), g2_ref[...])
    o_ref[...] = ((gn + bonus) * gate).astype(o_ref.dtype)


def _rwkv_post(y0, y1, seg_c, row_off, segm, g2, lnx_g, lnx_b, tm):
    t = y0.shape[0]
    off = row_off // tm
    const2 = lambda i: (0, 0)
    return pl.pallas_call(
        _rwkv_post_kernel,
        grid=(t // tm,),
        in_specs=[pl.BlockSpec((tm, 2 * HALF), lambda i: (i, 0)),
                  pl.BlockSpec((tm, 2 * HALF), lambda i: (i, 0)),
                  pl.BlockSpec((tm, GATE_LORA), lambda i: (off + i, (SEG_C - GATE_LORA) // GATE_LORA)),
                  pl.BlockSpec((HALF, HALF), const2),
                  pl.BlockSpec((GATE_LORA, HALF), const2),
                  pl.BlockSpec((1, HALF), const2),
                  pl.BlockSpec((1, HALF), const2)],
        out_specs=pl.BlockSpec((tm, HALF), lambda i: (i, 0)),
        out_shape=jax.ShapeDtypeStruct((t, HALF), BF16),
        compiler_params=_cparams("parallel"),
        name="rwkv_post",
    )(y0, y1, seg_c, segm, g2, lnx_g, lnx_b)


def _merge_kernel(x_ref, oa_ref, ob_ref, oc_ref, gl_ref, wb_ref, wo_ref, g1_ref, lg_ref, lb_ref, o_ref, *, alpha):
    acc = None
    for j, br in enumerate((oa_ref, ob_ref, oc_ref)):
        p = _dot(br[...], wb_ref[j])
        term = jax.nn.sigmoid(gl_ref[:, j * D_MODEL:(j + 1) * D_MODEL]) * p
        acc = term if acc is None else acc + term
    mixed = _dot(acc.astype(BF16), wo_ref[...])
    o_ref[...] = _layer_norm(alpha * x_ref[...] + g1_ref[0] * mixed, lg_ref[...], lb_ref[...])


def _merge(x, o_a, o_b, o_c, seg_g, w_branch, w_out, mod, ln_g, ln_b, tm, t_ctx, l_lat, alpha):
    t = x.shape[0]
    row = functools.partial(_mod_row, tm=tm, t_ctx=t_ctx, l_lat=l_lat)
    tok = lambda w: pl.BlockSpec((tm, w), lambda i: (i, 0))
    return pl.pallas_call(
        functools.partial(_merge_kernel, alpha=alpha),
        grid=(t // tm,),
        in_specs=[tok(D_MODEL), tok(HALF), tok(HALF), tok(HALF), tok(SEG_G),
                  pl.BlockSpec(w_branch.shape, lambda i: (0, 0, 0)),
                  pl.BlockSpec(w_out.shape, lambda i: (0, 0)),
                  pl.BlockSpec((1, 1, D_MODEL), lambda i: (row(i), 0, 2)),
                  pl.BlockSpec((1, D_MODEL), lambda i: (0, 0)),
                  pl.BlockSpec((1, D_MODEL), lambda i: (0, 0))],
        out_specs=tok(D_MODEL),
        out_shape=jax.ShapeDtypeStruct((t, D_MODEL), F32),
        compiler_params=_cparams("parallel"),
        name="merge",
    )(x, o_a, o_b, o_c, seg_g, w_branch, w_out, mod, ln_g, ln_b)


def _ffn_kernel(x_ref, sh_ref, sc_ref, g2_ref, wu_ref, wd_ref, lg_ref, lb_ref, o_ref, h_ref, acc_ref, *, alpha, n_f):
    j = pl.program_id(1)

    @pl.when(j == 0)
    def _():
        h_ref[...] = (x_ref[...] * (1.0 + sc_ref[0]) + sh_ref[0]).astype(BF16)
        acc_ref[...] = jnp.zeros(acc_ref.shape, F32)

    u = jnp.maximum(_dot(h_ref[...], wu_ref[...]), 0.0)
    acc_ref[...] += _dot((u * u).astype(BF16), wd_ref[...])

    @pl.when(j == n_f - 1)
    def _():
        o_ref[...] = _layer_norm(alpha * x_ref[...] + g2_ref[0] * acc_ref[...], lg_ref[...], lb_ref[...])


def _ffn(x, mod, w_up, w_down, ln_g, ln_b, tm, tf, t_ctx, l_lat, alpha):
    t = x.shape[0]
    n_f = D_FF // tf
    row = functools.partial(_mod_row, tm=tm, t_ctx=t_ctx, l_lat=l_lat)
    modspec = lambda blk: pl.BlockSpec((1, 1, D_MODEL), lambda i, j: (row(i), 0, blk))
    return pl.pallas_call(
        functools.partial(_ffn_kernel, alpha=alpha, n_f=n_f),
        grid=(t // tm, n_f),
        in_specs=[pl.BlockSpec((tm, D_MODEL), lambda i, j: (i, 0)),
                  modspec(3), modspec(4), modspec(5),
                  pl.BlockSpec((D_MODEL, tf), lambda i, j: (0, j)),
                  pl.BlockSpec((tf, D_MODEL), lambda i, j: (j, 0)),
                  pl.BlockSpec((1, D_MODEL), lambda i, j: (0, 0)),
                  pl.BlockSpec((1, D_MODEL), lambda i, j: (0, 0))],
        out_specs=pl.BlockSpec((tm, D_MODEL), lambda i, j: (i, 0)),
        out_shape=jax.ShapeDtypeStruct((t, D_MODEL), F32),
        scratch_shapes=[pltpu.VMEM((tm, D_MODEL), BF16), pltpu.VMEM((tm, D_MODEL), F32)],
        compiler_params=_cparams("parallel", "arbitrary"),
        name="ffn",
    )(x, mod, mod, mod, w_up, w_down, ln_g, ln_b)


def _rope_tables(l):
    pos = jnp.arange(l, dtype=jnp.int32)
    row = (pos // GRID_W).astype(F32)
    col = (pos % GRID_W).astype(F32)
    half = HEAD // 2
    inv_freq = ROPE_THETA ** (-jnp.arange(0, half, 2, dtype=F32) / half)
    ang_r = row[:, None] * inv_freq[None, :]
    ang_c = col[:, None] * inv_freq[None, :]
    cos = jnp.concatenate([jnp.cos(ang_r), jnp.cos(ang_r), jnp.cos(ang_c), jnp.cos(ang_c)], axis=-1)
    sin = jnp.concatenate([-jnp.sin(ang_r), jnp.sin(ang_r), -jnp.sin(ang_c), jnp.sin(ang_c)], axis=-1)
    return jnp.tile(cos, (1, LANES // HEAD)), jnp.tile(sin, (1, LANES // HEAD))


def _head_block_matrix(width, value):
    idx = jnp.arange(width) // HEAD
    return jnp.where(idx[:, None] == idx[None, :], value, 0.0).astype(BF16)


def _pick_tile(pref, *sizes):
    return min(pref, functools.reduce(math.gcd, sizes))


def kernel(x_prompt, x_sample, cache_k, cache_v, state_wkv, c, c_ctx, w_ada, b_ada, w_in, sgu_ln_g, sgu_ln_b, sgu_w, sgu_b, q_norm, k_norm, rwkv_mu, rwkv_w0, rwkv_w2, rwkv_a0, rwkv_a2, rwkv_k_k, rwkv_k_a, rwkv_r_k, rwkv_g2, rwkv_lnx_g, rwkv_lnx_b, w_branch, w_out, ln1_g, ln1_b, w_up, w_down, ln2_g, ln2_b):
    depth = w_in.shape[0]
    b_ctx, l_ctx, _ = x_prompt.shape
    b_lat, l_lat, _ = x_sample.shape
    past = cache_k.shape[2]
    t_ctx = b_ctx * l_ctx
    t_lat = b_lat * l_lat
    alpha = (2 * depth) ** 0.25

    tm = _pick_tile(512, t_ctx, l_lat)
    tm_ffn = _pick_tile(1024, t_ctx, l_lat)
    tk = _pick_tile(512, l_ctx, l_lat)
    tq_ctx = _pick_tile(256, l_ctx)
    tq_lat = _pick_tile(128, l_lat)
    c_ctx_chunk = _pick_tile(128, l_ctx)
    c_lat_chunk = _pick_tile(128, l_lat)

    n_rows = 1 + b_lat
    pad_rows = -n_rows % 16
    cvec = jnp.concatenate([c_ctx[None, :], c, jnp.zeros((pad_rows, D_MODEL), F32)], axis=0)
    mod_all = _ada(cvec, w_ada.astype(BF16), b_ada[:, None, :])

    w_in_b = w_in.astype(BF16)
    w_branch_b = w_branch.astype(BF16)
    w_out_b = w_out.astype(BF16)
    w_up_b = w_up.astype(BF16)
    w_down_b = w_down.astype(BF16)
    sgu_w_b = sgu_w.astype(BF16)
    g2_b = rwkv_g2.astype(BF16)
    seg_mean2 = _head_block_matrix(KV_W, 1.0 / HEAD)
    seg_mean8 = _head_block_matrix(HALF, 1.0 / HEAD)
    seg_ones8 = _head_block_matrix(HALF, 1.0)
    cos_ctx, sin_ctx = _rope_tables(l_ctx)
    cos_lat, sin_lat = _rope_tables(l_lat)
    zeros_lora = jnp.zeros((depth, 2, LORA, HALF), F32)
    wwa = jnp.concatenate([jnp.concatenate([rwkv_w2, zeros_lora], axis=-1),
                           jnp.concatenate([zeros_lora, rwkv_a2], axis=-1)], axis=-2).astype(BF16)

    x = jnp.concatenate([x_prompt.reshape(t_ctx, D_MODEL), x_sample.reshape(t_lat, D_MODEL)], axis=0)
    new_k, new_v, new_s = [], [], []
    for l in range(depth):
        mod = mod_all[l][:, None, :]
        proj = functools.partial(_modmm, x, mod, 0, 1, tm=tm, t_ctx=t_ctx, l_lat=l_lat)
        seg_a = proj(w_in_b[l, :, :SEG_A])
        seg_b = proj(w_in_b[l, :, SEG_A:SEG_A + SEG_B])
        seg_c = proj(w_in_b[l, :, SEG_A + SEG_B:SEG_A + SEG_B + SEG_C])
        seg_g = proj(w_in_b[l, :, SEG_A + SEG_B + SEG_C:])

        b_s_full = jnp.repeat(sgu_b[l].T, HALF // SGU_GROUPS, axis=1)
        o_a = _sgu(seg_a, sgu_ln_g[l][None], sgu_ln_b[l][None], sgu_w_b[l], b_s_full, tm)

        qn2 = jnp.tile(q_norm[l], LANES // HEAD)[None]
        kn2 = jnp.tile(k_norm[l], KV_W // HEAD)[None]
        kn_ctx, kr_ctx, vb_ctx = _kvprep(seg_b, 0, b_ctx, l_ctx, kn2, seg_mean2, cos_ctx, sin_ctx, False, tk)
        kr_lat, vb_lat = _kvprep(seg_b, t_ctx, b_lat, l_lat, kn2, seg_mean2, cos_lat, sin_lat, True, tk)
        new_k.append(kn_ctx.reshape(b_ctx, l_ctx, N_KV, HEAD))
        new_v.append(seg_b[:t_ctx, HALF + KV_W:].reshape(b_ctx, l_ctx, N_KV, HEAD))
        k_lat = jnp.concatenate([cache_k[:, l].reshape(b_lat, past, KV_W).astype(BF16),
                                 kr_lat.reshape(b_lat, l_lat, KV_W)], axis=1)
        v_lat = jnp.concatenate([cache_v[:, l].reshape(b_lat, past, KV_W).astype(BF16),
                                 vb_lat.reshape(b_lat, l_lat, KV_W)], axis=1)
        ob_ctx = _attention(seg_b, 0, b_ctx, l_ctx, kr_ctx.reshape(b_ctx, l_ctx, KV_W),
                            vb_ctx.reshape(b_ctx, l_ctx, KV_W), qn2, seg_mean2, cos_ctx, sin_ctx, False, tq_ctx)
        ob_lat = _attention(seg_b, t_ctx, b_lat, l_lat, k_lat, v_lat, qn2, seg_mean2, cos_lat, sin_lat, True, tq_lat)
        o_b = jnp.concatenate([ob_ctx, ob_lat], axis=0)

        rw = dict(mu=rwkv_mu[l][:, None, :], w0=rwkv_w0[l][:, None, :], a0=rwkv_a0[l][:, None, :], wwa=wwa[l],
                  k_k=rwkv_k_k[l][None], k_a=rwkv_k_a[l][None], r_k=rwkv_r_k[l].reshape(1, HALF), seg1=seg_ones8)
        y0c, y1c, s_ctx = _wkv(seg_c, 0, b_ctx, l_ctx, c_ctx_chunk, None, **rw)
        y0l, y1l, _ = _wkv(seg_c, t_ctx, b_lat, l_lat, c_lat_chunk, state_wkv[:, l], **rw)
        new_s.append(s_ctx)
        post = functools.partial(_rwkv_post, segm=seg_mean8, g2=g2_b[l], lnx_g=rwkv_lnx_g[l][None],
                                 lnx_b=rwkv_lnx_b[l][None], tm=tm)
        o_c = jnp.concatenate([post(y0c, y1c, seg_c, 0), post(y0l, y1l, seg_c, t_ctx)], axis=0)

        x = _merge(x, o_a, o_b, o_c, seg_g, w_branch_b[l], w_out_b[l], mod, ln1_g[l][None], ln1_b[l][None],
                   tm, t_ctx, l_lat, alpha)
        x = _ffn(x, mod, w_up_b[l], w_down_b[l], ln2_g[l][None], ln2_b[l][None], tm_ffn, 1024, t_ctx, l_lat, alpha)

    y = x[:t_ctx].reshape(b_ctx, l_ctx, D_MODEL)
    z = x[t_ctx:].reshape(b_lat, l_lat, D_MODEL)
    return (y, z, jnp.stack(new_k, axis=1), jnp.stack(new_v, axis=1), jnp.stack(new_s, axis=1))
```

```python
import functools
import math

import jax
import jax.numpy as jnp
from jax import lax
from jax.experimental import pallas as pl
from jax.experimental.pallas import tpu as pltpu

F32 = jnp.float32
BF16 = jnp.bfloat16

D_MODEL = 1024
HALF = D_MODEL // 2
HEAD = 64
N_HEADS = HALF // HEAD
N_KV = 2
GQA = N_HEADS // N_KV
KV_W = N_KV * HEAD
GRID_W = 64
SGU_CHUNK = 128
SGU_GROUPS = 4
LORA = 64
GATE_LORA = 128
D_FF = 4 * D_MODEL
ROPE_THETA = 10000.0
GN_EPS = 64e-5
SEG_A = 2 * HALF
SEG_B = HALF + 2 * KV_W
SEG_C = 3 * HALF + 4 * LORA + GATE_LORA
SEG_G = 3 * D_MODEL
RKV_W = 3 * HALF
LANES = 128
SUBLANES = 8
VMEM_LIMIT = 48 * 1024 * 1024


def _cparams(*sem):
    return pltpu.CompilerParams(dimension_semantics=sem, vmem_limit_bytes=VMEM_LIMIT)


def _dot(a, b):
    return jnp.dot(a, b, preferred_element_type=F32)


def _dot_nt(a, b):
    return lax.dot_general(a, b, (((1,), (1,)), ((), ())), preferred_element_type=F32)


def _dot_tn(a, b):
    return lax.dot_general(a, b, (((0,), (0,)), ((), ())), preferred_element_type=F32)


def _layer_norm(x, g, b, eps=1e-5):
    mu = jnp.mean(x, axis=-1, keepdims=True)
    xc = x - mu
    var = jnp.mean(xc * xc, axis=-1, keepdims=True)
    return xc * lax.rsqrt(var + eps) * g + b


def _ada_kernel(c_ref, w_ref, b_ref, o_ref):
    c = c_ref[...]
    s = (c * jax.nn.sigmoid(c)).astype(BF16)
    o_ref[0] = _dot(s, w_ref[0]) + b_ref[0]


def _ada(cvec, w_ada, b_ada):
    depth, _, n = w_ada.shape
    r = cvec.shape[0]
    tn = 1536
    return pl.pallas_call(
        _ada_kernel,
        grid=(depth, n // tn),
        in_specs=[pl.BlockSpec((r, D_MODEL), lambda l, j: (0, 0)),
                  pl.BlockSpec((1, D_MODEL, tn), lambda l, j: (l, 0, j)),
                  pl.BlockSpec((1, 1, tn), lambda l, j: (l, 0, j))],
        out_specs=pl.BlockSpec((1, r, tn), lambda l, j: (l, 0, j)),
        out_shape=jax.ShapeDtypeStruct((depth, r, n), F32),
        compiler_params=_cparams("parallel", "parallel"),
        name="ada",
    )(cvec, w_ada, b_ada)


def _mod_row(i, tm, t_ctx, l_lat):
    r = i * tm
    return jnp.where(r < t_ctx, 0, 1 + (r - t_ctx) // l_lat)


def _modmm_kernel(x_ref, sh_ref, sc_ref, w_ref, o_ref, h_ref):
    @pl.when(pl.program_id(1) == 0)
    def _():
        h_ref[...] = (x_ref[...] * (1.0 + sc_ref[0]) + sh_ref[0]).astype(BF16)

    o_ref[...] = _dot(h_ref[...], w_ref[...]).astype(o_ref.dtype)


def _modmm(x, mod, sh_blk, sc_blk, w, tm, t_ctx, l_lat, out_dtype=F32):
    t = x.shape[0]
    n = w.shape[1]
    tn = n if n <= 2048 else n // 2
    row = functools.partial(_mod_row, tm=tm, t_ctx=t_ctx, l_lat=l_lat)
    return pl.pallas_call(
        _modmm_kernel,
        grid=(t // tm, n // tn),
        in_specs=[pl.BlockSpec((tm, D_MODEL), lambda i, j: (i, 0)),
                  pl.BlockSpec((1, 1, D_MODEL), lambda i, j: (row(i), 0, sh_blk)),
                  pl.BlockSpec((1, 1, D_MODEL), lambda i, j: (row(i), 0, sc_blk)),
                  pl.BlockSpec((D_MODEL, tn), lambda i, j: (0, j))],
        out_specs=pl.BlockSpec((tm, tn), lambda i, j: (i, j)),
        out_shape=jax.ShapeDtypeStruct((t, n), out_dtype),
        scratch_shapes=[pltpu.VMEM((tm, D_MODEL), BF16)],
        compiler_params=_cparams("parallel", "arbitrary"),
        name="modmm",
    )(x, mod, mod, w)


def _sgu_kernel(uv_ref, g_ref, b_ref, ws_ref, bs_ref, o_ref, *, tm):
    v = uv_ref[:, HALF:]
    vn = _layer_norm(v, g_ref[...], b_ref[...]).astype(BF16)
    gc = HALF // SGU_GROUPS
    for n in range(tm // SGU_CHUNK):
        rows = slice(n * SGU_CHUNK, (n + 1) * SGU_CHUNK)
        for g in range(SGU_GROUPS):
            cols = slice(g * gc, (g + 1) * gc)
            s = _dot(ws_ref[g], vn[rows, cols]) + bs_ref[:, cols]
            o_ref[rows, cols] = (uv_ref[rows, cols] * s).astype(o_ref.dtype)


def _sgu(seg_a, ln_g, ln_b, w_s, b_s_full, tm):
    t = seg_a.shape[0]
    return pl.pallas_call(
        functools.partial(_sgu_kernel, tm=tm),
        grid=(t // tm,),
        in_specs=[pl.BlockSpec((tm, SEG_A), lambda i: (i, 0)),
                  pl.BlockSpec((1, HALF), lambda i: (0, 0)),
                  pl.BlockSpec((1, HALF), lambda i: (0, 0)),
                  pl.BlockSpec((SGU_GROUPS, SGU_CHUNK, SGU_CHUNK), lambda i: (0, 0, 0)),
                  pl.BlockSpec((SGU_CHUNK, HALF), lambda i: (0, 0))],
        out_specs=pl.BlockSpec((tm, HALF), lambda i: (i, 0)),
        out_shape=jax.ShapeDtypeStruct((t, HALF), BF16),
        compiler_params=_cparams("parallel"),
        name="sgu",
    )(seg_a, ln_g, ln_b, w_s, b_s_full)


def _rope_swap(x):
    lane = lax.broadcasted_iota(jnp.int32, x.shape, 1)
    up = pltpu.roll(x, LANES - 16, axis=1)
    dn = pltpu.roll(x, 16, axis=1)
    return jnp.where((lane & 16) == 0, up, dn)


def _head_rms(x, seg_ref, g):
    ms = _dot((x * x).astype(BF16), seg_ref[...])
    return x * lax.rsqrt(ms + 1e-6) * g


def _kvprep_kernel(k_ref, v_ref, g_ref, seg_ref, cos_ref, sin_ref, *out_refs, rope):
    kn = _head_rms(k_ref[...], seg_ref, g_ref[...])
    if rope:
        kr_ref, vb_ref = out_refs
        kn = kn * cos_ref[...] + _rope_swap(kn) * sin_ref[...]
    else:
        kn_ref, kr_ref, vb_ref = out_refs
        kn_ref[...] = kn
    kr_ref[...] = kn.astype(BF16)
    vb_ref[...] = v_ref[...].astype(BF16)


def _kvprep(seg_b, row_off, b, l, k_norm2, seg_mat, cos, sin, rope, tk):
    t = b * l
    off = row_off // tk
    lb = l // tk
    out_shape = [jax.ShapeDtypeStruct((t, KV_W), BF16), jax.ShapeDtypeStruct((t, KV_W), BF16)]
    out_specs = [pl.BlockSpec((tk, KV_W), lambda i: (i, 0)), pl.BlockSpec((tk, KV_W), lambda i: (i, 0))]
    if not rope:
        out_shape = [jax.ShapeDtypeStruct((t, KV_W), F32)] + out_shape
        out_specs = [pl.BlockSpec((tk, KV_W), lambda i: (i, 0))] + out_specs
    return pl.pallas_call(
        functools.partial(_kvprep_kernel, rope=rope),
        grid=(t // tk,),
        in_specs=[pl.BlockSpec((tk, KV_W), lambda i: (off + i, HALF // KV_W)),
                  pl.BlockSpec((tk, KV_W), lambda i: (off + i, HALF // KV_W + 1)),
                  pl.BlockSpec((1, KV_W), lambda i: (0, 0)),
                  pl.BlockSpec((KV_W, KV_W), lambda i: (0, 0)),
                  pl.BlockSpec((tk, KV_W), lambda i: (i % lb, 0)),
                  pl.BlockSpec((tk, KV_W), lambda i: (i % lb, 0))],
        out_specs=out_specs,
        out_shape=out_shape,
        compiler_params=_cparams("parallel"),
        name="kvprep",
    )(seg_b, seg_b, k_norm2, seg_mat, cos, sin)


def _attn_kernel(q_ref, g_ref, seg_ref, cos_ref, sin_ref, k_ref, v_ref, o_ref, *, rope, tq):
    qs = []
    for s in range(HALF // LANES):
        q = _head_rms(q_ref[:, s * LANES:(s + 1) * LANES], seg_ref, g_ref[...])
        if rope:
            q = q * cos_ref[...] + _rope_swap(q) * sin_ref[...]
        qs.append((q * HEAD ** -0.5).astype(BF16))
    k = k_ref[0]
    v = v_ref[0]
    for g in range(N_KV):
        kg = k[:, g * HEAD:(g + 1) * HEAD]
        vg = v[:, g * HEAD:(g + 1) * HEAD]
        heads = range(g * GQA, (g + 1) * GQA)
        qg = jnp.concatenate([qs[h // 2][:, (h % 2) * HEAD:(h % 2 + 1) * HEAD] for h in heads], axis=0)
        s = _dot_nt(qg, kg)
        e = jnp.exp(s - jnp.max(s, axis=-1, keepdims=True))
        o = _dot(e.astype(BF16), vg) / jnp.sum(e, axis=-1, keepdims=True)
        for j, h in enumerate(heads):
            o_ref[:, h * HEAD:(h + 1) * HEAD] = o[j * tq:(j + 1) * tq].astype(o_ref.dtype)


def _attention(seg_b, row_off, b, l, kfull, vfull, q_norm2, seg_mat, cos, sin, rope, tq):
    t = b * l
    off = row_off // tq
    lb = l // tq
    lk = kfull.shape[1]
    return pl.pallas_call(
        functools.partial(_attn_kernel, rope=rope, tq=tq),
        grid=(b, lb),
        in_specs=[pl.BlockSpec((tq, HALF), lambda bi, i: (off + bi * lb + i, 0)),
                  pl.BlockSpec((1, LANES), lambda bi, i: (0, 0)),
                  pl.BlockSpec((KV_W, KV_W), lambda bi, i: (0, 0)),
                  pl.BlockSpec((tq, LANES), lambda bi, i: (i, 0)),
                  pl.BlockSpec((tq, LANES), lambda bi, i: (i, 0)),
                  pl.BlockSpec((1, lk, KV_W), lambda bi, i: (bi, 0, 0)),
                  pl.BlockSpec((1, lk, KV_W), lambda bi, i: (bi, 0, 0))],
        out_specs=pl.BlockSpec((tq, HALF), lambda bi, i: (bi * lb + i, 0)),
        out_shape=jax.ShapeDtypeStruct((t, HALF), BF16),
        compiler_params=_cparams("parallel", "parallel"),
        name="attention",
    )(seg_b, q_norm2, seg_mat, cos, sin, kfull, vfull)


def _split3(x):
    h = x.astype(BF16)
    r1 = x - h.astype(F32)
    m = r1.astype(BF16)
    lo = (r1 - m.astype(F32)).astype(BF16)
    return h, m, lo


def _wkv_prep(x, nb_row, d, c, mu_rkv, mu_lo, w0, a0, wwa, k_k, k_a, r_k, seg1, y_ref, rev):
    rows = lax.broadcasted_iota(jnp.int32, (c, 1), 0)
    edge = (c - 1) if rev else 0

    def shifted(cur, nb):
        rolled = pltpu.roll(cur, (c - 1) if rev else 1, axis=0)
        return jnp.where(rows == edge, nb, rolled)

    rkv = x[:, :RKV_W]
    lo = x[:, RKV_W + 2 * LORA * d:RKV_W + 2 * LORA * (d + 1)]
    f = rkv + mu_rkv * (shifted(rkv, nb_row[:, :RKV_W]) - rkv)
    fl = lo + mu_lo * (shifted(lo, nb_row[:, RKV_W + 2 * LORA * d:RKV_W + 2 * LORA * (d + 1)]) - lo)
    r = f[:, :HALF]
    k = f[:, HALF:2 * HALF]
    v = f[:, 2 * HALF:]
    lane = lax.broadcasted_iota(jnp.int32, fl.shape, 1)
    lin = _dot(jnp.where(lane < LORA, jnp.tanh(fl), fl).astype(BF16), wwa)
    lw = (-math.exp(-0.5)) * jax.nn.sigmoid(w0 + lin[:, :HALF])
    asig = jax.nn.sigmoid(a0 + lin[:, HALF:])
    kk = k * k_k
    ss = _dot((kk * kk).astype(BF16), seg1)
    kkn = kk / jnp.maximum(jnp.sqrt(ss), 1e-12)
    kmod = k * (1.0 + (asig - 1.0) * k_a)
    bonus = _dot((r * kmod * r_k).astype(BF16), seg1) * v
    y_ref[:, HALF:] = bonus

    ti = lax.broadcasted_iota(jnp.int32, (c, c), 0)
    si = lax.broadcasted_iota(jnp.int32, (c, c), 1)
    incl = (si >= ti) if rev else (si <= ti)
    strict = (si > ti) if rev else (si < ti)
    tri = incl.astype(BF16)
    h3, m3, l3 = _split3(lw)
    cum = _dot(tri, h3) + _dot(tri, m3) + _dot(tri, l3)
    ref = cum[c // 2:c // 2 + 1]
    end = 0 if rev else c - 1
    cum_end = cum[end:end + 1]
    g = cum - ref
    e_pos = jnp.exp(g)
    e_neg = jnp.exp(-g)
    e_ref = jnp.exp(ref)
    e_tot = jnp.exp(cum_end)
    e_end = jnp.exp(cum_end - ref)
    at_c = -kkn * jnp.exp(g - lw)
    rt_c = r * e_pos
    bt = kkn * asig * e_neg
    kt = kmod * e_neg
    return dict(at_c=at_c, rt_c=rt_c, bt=bt, kt=kt, at_true=at_c * e_ref, rt_true=rt_c * e_ref,
                bh=bt * e_end, kh=kt * e_end, v=v, e_tot=e_tot, incl=incl, strict=strict)


def _wkv_chains(preps, c, s_ref, y_refs):
    lane = lax.broadcasted_iota(jnp.int32, (c, LANES), 1)
    lo = lane < HEAD
    hi = jnp.logical_not(lo)
    chains = [(d, p) for d in range(2) for p in range(N_HEADS // 2)]

    def slab(d, p, name):
        return preps[d][name][:, p * LANES:(p + 1) * LANES]

    def keep(mask, x):
        return jnp.where(mask, x, 0.0)

    n_pow, a_ak, m_all, vsw, x_cur = {}, {}, {}, {}, {}
    for ch in chains:
        d, p = ch
        at_c, rt_c = slab(d, p, "at_c"), slab(d, p, "rt_c")
        lhs = jnp.concatenate([keep(lo, at_c), keep(hi, at_c), keep(lo, rt_c), keep(hi, rt_c)], axis=0)
        rhs = jnp.concatenate([slab(d, p, "bt"), slab(d, p, "kt")], axis=0)
        g = _dot_nt(lhs.astype(BF16), rhs.astype(BF16))
        strict, incl = preps[d]["strict"], preps[d]["incl"]
        incl2 = jnp.concatenate([incl, incl], axis=1)
        n_pow[ch] = [keep(strict, g[h * c:(h + 1) * c, :c]).astype(BF16) for h in range(2)]
        a_ak[ch] = [keep(strict, g[h * c:(h + 1) * c, c:]).astype(BF16) for h in range(2)]
        m_all[ch] = [keep(incl2, g[(2 + h) * c:(3 + h) * c, :]).astype(BF16) for h in range(2)]
        v_sw = pltpu.roll(slab(d, p, "v"), HEAD, axis=1)
        vsw[ch] = [keep(hi, v_sw).astype(BF16), keep(lo, v_sw).astype(BF16)]
    for ch in chains:
        d, p = ch
        at_true = slab(d, p, "at_true")
        x_cur[ch] = [keep(lo, at_true) + _dot(a_ak[ch][0], vsw[ch][0]),
                     keep(hi, at_true) + _dot(a_ak[ch][1], vsw[ch][1])]
    steps = int(math.log2(c))
    for j in range(steps):
        for ch in chains:
            for h in range(2):
                xb = x_cur[ch][h].astype(BF16)
                if j + 1 < steps:
                    out = _dot(n_pow[ch][h], jnp.concatenate([xb, n_pow[ch][h]], axis=1))
                    x_cur[ch][h] = x_cur[ch][h] + out[:, :LANES]
                    n_pow[ch][h] = out[:, LANES:].astype(BF16)
                else:
                    x_cur[ch][h] = x_cur[ch][h] + _dot(n_pow[ch][h], xb)
    for ch in chains:
        d, p = ch
        s0 = s_ref[d, p]
        x0, x1 = x_cur[ch]
        st = _dot_nt(jnp.concatenate([x0, x1, slab(d, p, "rt_true")], axis=0).astype(BF16), s0.astype(BF16))
        w0 = jnp.concatenate([keep(hi, st[:c] + x0).astype(BF16), vsw[ch][0]], axis=0)
        w1 = jnp.concatenate([keep(lo, st[c:2 * c] + x1).astype(BF16), vsw[ch][1]], axis=0)
        y_sw = st[2 * c:] + _dot(m_all[ch][0], w0) + _dot(m_all[ch][1], w1)
        y_refs[d][:, p * LANES:(p + 1) * LANES] = pltpu.roll(y_sw, HEAD, axis=1)
        bh, kh = slab(d, p, "bh"), slab(d, p, "kh")
        kb = jnp.concatenate([keep(lo, bh), keep(lo, kh), keep(hi, bh), keep(hi, kh)], axis=0).astype(BF16)
        e_tot = preps[d]["e_tot"][:, p * LANES:(p + 1) * LANES]
        s_ref[d, p] = s0 * e_tot + _dot_tn(jnp.concatenate([w0, w1], axis=0), kb)


def _wkv_kernel(*refs, c, n_c, latent):
    if latent:
        (x0_ref, x1_ref, p0_ref, n1_ref, s0_ref, mu_ref, w0_ref, a0_ref, wwa_ref, kk_ref, ka_ref, rk_ref,
         seg_ref, y0_ref, y1_ref, sf_ref, s_ref) = refs
    else:
        (x0_ref, x1_ref, p0_ref, n1_ref, mu_ref, w0_ref, a0_ref, wwa_ref, kk_ref, ka_ref, rk_ref,
         seg_ref, y0_ref, y1_ref, sf_ref, s_ref) = refs
    i = pl.program_id(1)

    @pl.when(i == 0)
    def _():
        s_ref[...] = s0_ref[0] if latent else jnp.zeros(s_ref.shape, F32)

    inner = (i > 0).astype(F32)
    preps = []
    for d, (x_ref, nb_ref, y_ref) in enumerate(((x0_ref, p0_ref, y0_ref), (x1_ref, n1_ref, y1_ref))):
        nb = nb_ref[SUBLANES - 1:SUBLANES, :] if d == 0 else nb_ref[0:1, :]
        preps.append(_wkv_prep(x_ref[...], nb * inner, d, c,
                               mu_ref[d, :, :RKV_W], mu_ref[d, :, RKV_W:], w0_ref[d], a0_ref[d], wwa_ref[d],
                               kk_ref[...], ka_ref[...], rk_ref[...], seg_ref[...], y_ref, rev=(d == 1)))
    _wkv_chains(preps, c, s_ref, (y0_ref, y1_ref))

    @pl.when(i == n_c - 1)
    def _():
        sf_ref[0] = s_ref[...]


def _wkv(seg_c, row_off, b, l, c, s0, mu, w0, a0, wwa, k_k, k_a, r_k, seg1):
    t = b * l
    n_c = l // c
    t_all = seg_c.shape[0]
    cb = row_off // c
    c8 = c // SUBLANES
    r8 = row_off // SUBLANES
    last8 = t_all // SUBLANES - 1
    latent = s0 is not None
    const2 = lambda bi, i: (0, 0)
    const3 = lambda bi, i: (0, 0, 0)
    in_specs = [pl.BlockSpec((c, SEG_C), lambda bi, i: (cb + bi * n_c + i, 0)),
                pl.BlockSpec((c, SEG_C), lambda bi, i: (cb + bi * n_c + n_c - 1 - i, 0)),
                pl.BlockSpec((SUBLANES, SEG_C),
                             lambda bi, i: (jnp.maximum(r8 + (bi * n_c + i) * c8 - 1, 0), 0)),
                pl.BlockSpec((SUBLANES, SEG_C),
                             lambda bi, i: (jnp.minimum(r8 + (bi * n_c + n_c - i) * c8, last8), 0))]
    args = [seg_c, seg_c, seg_c, seg_c]
    if latent:
        in_specs.append(pl.BlockSpec((1, 2, N_HEADS // 2, LANES, LANES), lambda bi, i: (bi, 0, 0, 0, 0)))
        args.append(s0)
    in_specs += [pl.BlockSpec(mu.shape, const3), pl.BlockSpec(w0.shape, const3), pl.BlockSpec(a0.shape, const3),
                 pl.BlockSpec(wwa.shape, const3), pl.BlockSpec(k_k.shape, const2), pl.BlockSpec(k_a.shape, const2),
                 pl.BlockSpec(r_k.shape, const2), pl.BlockSpec(seg1.shape, const2)]
    args += [mu, w0, a0, wwa, k_k, k_a, r_k, seg1]
    return pl.pallas_call(
        functools.partial(_wkv_kernel, c=c, n_c=n_c, latent=latent),
        grid=(b, n_c),
        in_specs=in_specs,
        out_specs=[pl.BlockSpec((c, 2 * HALF), lambda bi, i: (bi * n_c + i, 0)),
                   pl.BlockSpec((c, 2 * HALF), lambda bi, i: (bi * n_c + n_c - 1 - i, 0)),
                   pl.BlockSpec((1, 2, N_HEADS // 2, LANES, LANES), lambda bi, i: (bi, 0, 0, 0, 0))],
        out_shape=[jax.ShapeDtypeStruct((t, 2 * HALF), F32), jax.ShapeDtypeStruct((t, 2 * HALF), F32),
                   jax.ShapeDtypeStruct((b, 2, N_HEADS // 2, LANES, LANES), F32)],
        scratch_shapes=[pltpu.VMEM((2, N_HEADS // 2, LANES, LANES), F32)],
        compiler_params=_cparams("parallel", "arbitrary"),
        name="wkv",
    )(*args)


def _rwkv_post_kernel(y0_ref, y1_ref, gd_ref, segm_ref, g2_ref, lg_ref, lb_ref, o_ref):
    ys = y0_ref[:, :HALF] + y1_ref[:, :HALF]
    bonus = y0_ref[:, HALF:] + y1_ref[:, HALF:]
    mu = _dot(ys.astype(BF16), segm_ref[...])
    yc = ys - mu
    var = _dot((yc * yc).astype(BF16), segm_ref[...])
    gn = yc * lax.rsqrt(var + GN_EPS) * lg_ref[...] + lb_ref[...]
    gate = _dot(jax.nn.sigmoid(gd_ref[...]).astype(BF16), g2_ref[...])
    o_ref[...] = ((gn + bonus) * gate).astype(o_ref.dtype)


def _rwkv_post(y0, y1, seg_c, row_off, segm, g2, lnx_g, lnx_b, tm):
    t = y0.shape[0]
    off = row_off // tm
    const2 = lambda i: (0, 0)
    return pl.pallas_call(
        _rwkv_post_kernel,
        grid=(t // tm,),
        in_specs=[pl.BlockSpec((tm, 2 * HALF), lambda i: (i, 0)),
                  pl.BlockSpec((tm, 2 * HALF), lambda i: (i, 0)),
                  pl.BlockSpec((tm, GATE_LORA), lambda i: (off + i, (SEG_C - GATE_LORA) // GATE_LORA)),
                  pl.BlockSpec((HALF, HALF), const2),
                  pl.BlockSpec((GATE_LORA, HALF), const2),
                  pl.BlockSpec((1, HALF), const2),
                  pl.BlockSpec((1, HALF), const2)],
        out_specs=pl.BlockSpec((tm, HALF), lambda i: (i, 0)),
        out_shape=jax.ShapeDtypeStruct((t, HALF), BF16),
        compiler_params=_cparams("parallel"),
        name="rwkv_post",
    )(y0, y1, seg_c, segm, g2, lnx_g, lnx_b)


def _merge_kernel(x_ref, oa_ref, ob_ref, oc_ref, gl_ref, wb_ref, wo_ref, g1_ref, lg_ref, lb_ref, o_ref, *, alpha):
    acc = None
    for j, br in enumerate((oa_ref, ob_ref, oc_ref)):
        p = _dot(br[...], wb_ref[j])
        term = jax.nn.sigmoid(gl_ref[:, j * D_MODEL:(j + 1) * D_MODEL]) * p
        acc = term if acc is None else acc + term
    mixed = _dot(acc.astype(BF16), wo_ref[...])
    o_ref[...] = _layer_norm(alpha * x_ref[...] + g1_ref[0] * mixed, lg_ref[...], lb_ref[...])


def _merge(x, o_a, o_b, o_c, seg_g, w_branch, w_out, mod, ln_g, ln_b, tm, t_ctx, l_lat, alpha):
    t = x.shape[0]
    row = functools.partial(_mod_row, tm=tm, t_ctx=t_ctx, l_lat=l_lat)
    tok = lambda w: pl.BlockSpec((tm, w), lambda i: (i, 0))
    return pl.pallas_call(
        functools.partial(_merge_kernel, alpha=alpha),
        grid=(t // tm,),
        in_specs=[tok(D_MODEL), tok(HALF), tok(HALF), tok(HALF), tok(SEG_G),
                  pl.BlockSpec(w_branch.shape, lambda i: (0, 0, 0)),
                  pl.BlockSpec(w_out.shape, lambda i: (0, 0)),
                  pl.BlockSpec((1, 1, D_MODEL), lambda i: (row(i), 0, 2)),
                  pl.BlockSpec((1, D_MODEL), lambda i: (0, 0)),
                  pl.BlockSpec((1, D_MODEL), lambda i: (0, 0))],
        out_specs=tok(D_MODEL),
        out_shape=jax.ShapeDtypeStruct((t, D_MODEL), F32),
        compiler_params=_cparams("parallel"),
        name="merge",
    )(x, o_a, o_b, o_c, seg_g, w_branch, w_out, mod, ln_g, ln_b)


def _ffn_kernel(x_ref, sh_ref, sc_ref, g2_ref, wu_ref, wd_ref, lg_ref, lb_ref, o_ref, h_ref, acc_ref, *, alpha, n_f):
    j = pl.program_id(1)

    @pl.when(j == 0)
    def _():
        h_ref[...] = (x_ref[...] * (1.0 + sc_ref[0]) + sh_ref[0]).astype(BF16)
        acc_ref[...] = jnp.zeros(acc_ref.shape, F32)

    u = jnp.maximum(_dot(h_ref[...], wu_ref[...]), 0.0)
    acc_ref[...] += _dot((u * u).astype(BF16), wd_ref[...])

    @pl.when(j == n_f - 1)
    def _():
        o_ref[...] = _layer_norm(alpha * x_ref[...] + g2_ref[0] * acc_ref[...], lg_ref[...], lb_ref[...])


def _ffn(x, mod, w_up, w_down, ln_g, ln_b, tm, tf, t_ctx, l_lat, alpha):
    t = x.shape[0]
    n_f = D_FF // tf
    row = functools.partial(_mod_row, tm=tm, t_ctx=t_ctx, l_lat=l_lat)
    modspec = lambda blk: pl.BlockSpec((1, 1, D_MODEL), lambda i, j: (row(i), 0, blk))
    return pl.pallas_call(
        functools.partial(_ffn_kernel, alpha=alpha, n_f=n_f),
        grid=(t // tm, n_f),
        in_specs=[pl.BlockSpec((tm, D_MODEL), lambda i, j: (i, 0)),
                  modspec(3), modspec(4), modspec(5),
                  pl.BlockSpec((D_MODEL, tf), lambda i, j: (0, j)),
                  pl.BlockSpec((tf, D_MODEL), lambda i, j: (j, 0)),
                  pl.BlockSpec((1, D_MODEL), lambda i, j: (0, 0)),
                  pl.BlockSpec((1, D_MODEL), lambda i, j: (0, 0))],
        out_specs=pl.BlockSpec((tm, D_MODEL), lambda i, j: (i, 0)),
        out_shape=jax.ShapeDtypeStruct((t, D_MODEL), F32),
        scratch_shapes=[pltpu.VMEM((tm, D_MODEL), BF16), pltpu.VMEM((tm, D_MODEL), F32)],
        compiler_params=_cparams("parallel", "arbitrary"),
        name="ffn",
    )(x, mod, mod, mod, w_up, w_down, ln_g, ln_b)


def _rope_tables(l):
    pos = jnp.arange(l, dtype=jnp.int32)
    row = (pos // GRID_W).astype(F32)
    col = (pos % GRID_W).astype(F32)
    half = HEAD // 2
    inv_freq = ROPE_THETA ** (-jnp.arange(0, half, 2, dtype=F32) / half)
    ang_r = row[:, None] * inv_freq[None, :]
    ang_c = col[:, None] * inv_freq[None, :]
    cos = jnp.concatenate([jnp.cos(ang_r), jnp.cos(ang_r), jnp.cos(ang_c), jnp.cos(ang_c)], axis=-1)
    sin = jnp.concatenate([-jnp.sin(ang_r), jnp.sin(ang_r), -jnp.sin(ang_c), jnp.sin(ang_c)], axis=-1)
    return jnp.tile(cos, (1, LANES // HEAD)), jnp.tile(sin, (1, LANES // HEAD))


def _head_block_matrix(width, value):
    idx = jnp.arange(width) // HEAD
    return jnp.where(idx[:, None] == idx[None, :], value, 0.0).astype(BF16)


def _pair_states(s):
    b = s.shape[0]
    s = s.reshape(b, 2, N_HEADS // 2, 2, HEAD, HEAD)
    z = jnp.zeros_like(s[:, :, :, 0])
    top = jnp.concatenate([z, s[:, :, :, 1]], axis=-1)
    bot = jnp.concatenate([s[:, :, :, 0], z], axis=-1)
    return jnp.concatenate([top, bot], axis=-2)


def _unpair_states(sp):
    b = sp.shape[0]
    s = jnp.stack([sp[..., HEAD:, :HEAD], sp[..., :HEAD, HEAD:]], axis=3)
    return s.reshape(b, 2, N_HEADS, HEAD, HEAD)


def _pick_tile(pref, *sizes):
    return min(pref, functools.reduce(math.gcd, sizes))


def kernel(x_prompt, x_sample, cache_k, cache_v, state_wkv, c, c_ctx, w_ada, b_ada, w_in, sgu_ln_g, sgu_ln_b, sgu_w, sgu_b, q_norm, k_norm, rwkv_mu, rwkv_w0, rwkv_w2, rwkv_a0, rwkv_a2, rwkv_k_k, rwkv_k_a, rwkv_r_k, rwkv_g2, rwkv_lnx_g, rwkv_lnx_b, w_branch, w_out, ln1_g, ln1_b, w_up, w_down, ln2_g, ln2_b):
    depth = w_in.shape[0]
    b_ctx, l_ctx, _ = x_prompt.shape
    b_lat, l_lat, _ = x_sample.shape
    past = cache_k.shape[2]
    t_ctx = b_ctx * l_ctx
    t_lat = b_lat * l_lat
    alpha = (2 * depth) ** 0.25

    tm = _pick_tile(512, t_ctx, l_lat)
    tm_ffn = _pick_tile(1024, t_ctx, l_lat)
    tk = _pick_tile(512, l_ctx, l_lat)
    tq_ctx = _pick_tile(256, l_ctx)
    tq_lat = _pick_tile(128, l_lat)
    c_ctx_chunk = _pick_tile(128, l_ctx)
    c_lat_chunk = _pick_tile(128, l_lat)

    n_rows = 1 + b_lat
    pad_rows = -n_rows % 16
    cvec = jnp.concatenate([c_ctx[None, :], c, jnp.zeros((pad_rows, D_MODEL), F32)], axis=0)
    mod_all = _ada(cvec, w_ada.astype(BF16), b_ada[:, None, :])

    w_in_b = w_in.astype(BF16)
    w_branch_b = w_branch.astype(BF16)
    w_out_b = w_out.astype(BF16)
    w_up_b = w_up.astype(BF16)
    w_down_b = w_down.astype(BF16)
    sgu_w_b = sgu_w.astype(BF16)
    g2_b = rwkv_g2.astype(BF16)
    seg_mean2 = _head_block_matrix(KV_W, 1.0 / HEAD)
    seg_mean8 = _head_block_matrix(HALF, 1.0 / HEAD)
    seg_ones8 = _head_block_matrix(HALF, 1.0)
    cos_ctx, sin_ctx = _rope_tables(l_ctx)
    cos_lat, sin_lat = _rope_tables(l_lat)
    zeros_lora = jnp.zeros((depth, 2, LORA, HALF), F32)
    wwa = jnp.concatenate([jnp.concatenate([rwkv_w2, zeros_lora], axis=-1),
                           jnp.concatenate([zeros_lora, rwkv_a2], axis=-1)], axis=-2).astype(BF16)

    x = jnp.concatenate([x_prompt.reshape(t_ctx, D_MODEL), x_sample.reshape(t_lat, D_MODEL)], axis=0)
    new_k, new_v, new_s = [], [], []
    for l in range(depth):
        mod = mod_all[l][:, None, :]
        proj = functools.partial(_modmm, x, mod, 0, 1, tm=tm, t_ctx=t_ctx, l_lat=l_lat)
        seg_a = proj(w_in_b[l, :, :SEG_A])
        seg_b = proj(w_in_b[l, :, SEG_A:SEG_A + SEG_B])
        seg_c = proj(w_in_b[l, :, SEG_A + SEG_B:SEG_A + SEG_B + SEG_C])
        seg_g = proj(w_in_b[l, :, SEG_A + SEG_B + SEG_C:])

        b_s_full = jnp.repeat(sgu_b[l].T, HALF // SGU_GROUPS, axis=1)
        o_a = _sgu(seg_a, sgu_ln_g[l][None], sgu_ln_b[l][None], sgu_w_b[l], b_s_full, tm)

        qn2 = jnp.tile(q_norm[l], LANES // HEAD)[None]
        kn2 = jnp.tile(k_norm[l], KV_W // HEAD)[None]
        kn_ctx, kr_ctx, vb_ctx = _kvprep(seg_b, 0, b_ctx, l_ctx, kn2, seg_mean2, cos_ctx, sin_ctx, False, tk)
        kr_lat, vb_lat = _kvprep(seg_b, t_ctx, b_lat, l_lat, kn2, seg_mean2, cos_lat, sin_lat, True, tk)
        new_k.append(kn_ctx.reshape(b_ctx, l_ctx, N_KV, HEAD))
        new_v.append(seg_b[:t_ctx, HALF + KV_W:].reshape(b_ctx, l_ctx, N_KV, HEAD))
        k_lat = jnp.concatenate([cache_k[:, l].reshape(b_lat, past, KV_W).astype(BF16),
                                 kr_lat.reshape(b_lat, l_lat, KV_W)], axis=1)
        v_lat = jnp.concatenate([cache_v[:, l].reshape(b_lat, past, KV_W).astype(BF16),
                                 vb_lat.reshape(b_lat, l_lat, KV_W)], axis=1)
        ob_ctx = _attention(seg_b, 0, b_ctx, l_ctx, kr_ctx.reshape(b_ctx, l_ctx, KV_W),
                            vb_ctx.reshape(b_ctx, l_ctx, KV_W), qn2, seg_mean2, cos_ctx, sin_ctx, False, tq_ctx)
        ob_lat = _attention(seg_b, t_ctx, b_lat, l_lat, k_lat, v_lat, qn2, seg_mean2, cos_lat, sin_lat, True, tq_lat)
        o_b = jnp.concatenate([ob_ctx, ob_lat], axis=0)

        rw = dict(mu=rwkv_mu[l][:, None, :], w0=rwkv_w0[l][:, None, :], a0=rwkv_a0[l][:, None, :], wwa=wwa[l],
                  k_k=rwkv_k_k[l][None], k_a=rwkv_k_a[l][None], r_k=rwkv_r_k[l].reshape(1, HALF), seg1=seg_ones8)
        y0c, y1c, s_ctx = _wkv(seg_c, 0, b_ctx, l_ctx, c_ctx_chunk, None, **rw)
        y0l, y1l, _ = _wkv(seg_c, t_ctx, b_lat, l_lat, c_lat_chunk, _pair_states(state_wkv[:, l]), **rw)
        new_s.append(_unpair_states(s_ctx))
        post = functools.partial(_rwkv_post, segm=seg_mean8, g2=g2_b[l], lnx_g=rwkv_lnx_g[l][None],
                                 lnx_b=rwkv_lnx_b[l][None], tm=tm)
        o_c = jnp.concatenate([post(y0c, y1c, seg_c, 0), post(y0l, y1l, seg_c, t_ctx)], axis=0)

        x = _merge(x, o_a, o_b, o_c, seg_g, w_branch_b[l], w_out_b[l], mod, ln1_g[l][None], ln1_b[l][None],
                   tm, t_ctx, l_lat, alpha)
        x = _ffn(x, mod, w_up_b[l], w_down_b[l], ln2_g[l][None], ln2_b[l][None], tm_ffn, 1024, t_ctx, l_lat, alpha)

    y = x[:t_ctx].reshape(b_ctx, l_ctx, D_MODEL)
    z = x[t_ctx:].reshape(b_lat, l_lat, D_MODEL)
    return (y, z, jnp.stack(new_k, axis=1), jnp.stack(new_v, axis=1), jnp.stack(new_s, axis=1))
```

```python
import functools
import math

import jax
import jax.numpy as jnp
from jax import lax
from jax.experimental import pallas as pl
from jax.experimental.pallas import tpu as pltpu

F32 = jnp.float32
BF16 = jnp.bfloat16

D_MODEL = 1024
HALF = D_MODEL // 2
HEAD = 64
N_HEADS = HALF // HEAD
N_KV = 2
GQA = N_HEADS // N_KV
KV_W = N_KV * HEAD
GRID_W = 64
SGU_CHUNK = 128
SGU_GROUPS = 4
LORA = 64
GATE_LORA = 128
D_FF = 4 * D_MODEL
ROPE_THETA = 10000.0
GN_EPS = 64e-5
SEG_A = 2 * HALF
SEG_B = HALF + 2 * KV_W
SEG_C = 3 * HALF + 4 * LORA + GATE_LORA
SEG_G = 3 * D_MODEL
RKV_W = 3 * HALF
B_OFF = 2048
SEG_BC = B_OFF + SEG_B
LANES = 128
SUBLANES = 8
VMEM_LIMIT = 48 * 1024 * 1024


def _cparams(*sem):
    return pltpu.CompilerParams(dimension_semantics=sem, vmem_limit_bytes=VMEM_LIMIT)


def _dot(a, b):
    return jnp.dot(a, b, preferred_element_type=F32)


def _dot_nt(a, b):
    return lax.dot_general(a, b, (((1,), (1,)), ((), ())), preferred_element_type=F32)


def _dot_tn(a, b):
    return lax.dot_general(a, b, (((0,), (0,)), ((), ())), preferred_element_type=F32)


def _layer_norm(x, g, b, eps=1e-5):
    mu = jnp.mean(x, axis=-1, keepdims=True)
    xc = x - mu
    var = jnp.mean(xc * xc, axis=-1, keepdims=True)
    return xc * lax.rsqrt(var + eps) * g + b


def _ada_kernel(c_ref, w_ref, b_ref, o_ref):
    c = c_ref[...]
    s = (c * jax.nn.sigmoid(c)).astype(BF16)
    o_ref[0] = _dot(s, w_ref[0]) + b_ref[0]


def _ada(cvec, w_ada, b_ada):
    depth, _, n = w_ada.shape
    r = cvec.shape[0]
    tn = 1536
    return pl.pallas_call(
        _ada_kernel,
        grid=(depth, n // tn),
        in_specs=[pl.BlockSpec((r, D_MODEL), lambda l, j: (0, 0)),
                  pl.BlockSpec((1, D_MODEL, tn), lambda l, j: (l, 0, j)),
                  pl.BlockSpec((1, 1, tn), lambda l, j: (l, 0, j))],
        out_specs=pl.BlockSpec((1, r, tn), lambda l, j: (l, 0, j)),
        out_shape=jax.ShapeDtypeStruct((depth, r, n), F32),
        compiler_params=_cparams("parallel", "parallel"),
        name="ada",
    )(cvec, w_ada, b_ada)


def _mod_row(i, tm, t_ctx, l_lat):
    r = i * tm
    return jnp.where(r < t_ctx, 0, 1 + (r - t_ctx) // l_lat)


def _modmm_kernel(x_ref, sh_ref, sc_ref, w_ref, o_ref, h_ref):
    @pl.when(pl.program_id(1) == 0)
    def _():
        h_ref[...] = (x_ref[...] * (1.0 + sc_ref[0]) + sh_ref[0]).astype(BF16)

    o_ref[...] = _dot(h_ref[...], w_ref[0]).astype(o_ref.dtype)


def _modmm(x, mod, sh_blk, sc_blk, w, l, tm, t_ctx, l_lat, out_dtype=F32):
    t = x.shape[0]
    n = w.shape[2]
    tn = n if n <= 2048 else n // 2
    row = functools.partial(_mod_row, tm=tm, t_ctx=t_ctx, l_lat=l_lat)
    return pl.pallas_call(
        _modmm_kernel,
        grid=(t // tm, n // tn),
        in_specs=[pl.BlockSpec((tm, D_MODEL), lambda i, j: (i, 0)),
                  pl.BlockSpec((1, 1, D_MODEL), lambda i, j: (row(i), 0, sh_blk)),
                  pl.BlockSpec((1, 1, D_MODEL), lambda i, j: (row(i), 0, sc_blk)),
                  pl.BlockSpec((1, D_MODEL, tn), lambda i, j: (l, 0, j))],
        out_specs=pl.BlockSpec((tm, tn), lambda i, j: (i, j)),
        out_shape=jax.ShapeDtypeStruct((t, n), out_dtype),
        scratch_shapes=[pltpu.VMEM((tm, D_MODEL), BF16)],
        compiler_params=_cparams("parallel", "arbitrary"),
        name="modmm",
    )(x, mod, mod, w)


def _sgu_kernel(x_ref, sh_ref, sc_ref, wa_ref, g_ref, b_ref, ws_ref, bs_ref, o_ref, *, tm):
    h = (x_ref[...] * (1.0 + sc_ref[0]) + sh_ref[0]).astype(BF16)
    uv = _dot(h, wa_ref[0])
    vn = _layer_norm(uv[:, HALF:], g_ref[0], b_ref[0]).astype(BF16)
    gc = HALF // SGU_GROUPS
    for n in range(tm // SGU_CHUNK):
        rows = slice(n * SGU_CHUNK, (n + 1) * SGU_CHUNK)
        for g in range(SGU_GROUPS):
            cols = slice(g * gc, (g + 1) * gc)
            s = _dot(ws_ref[0, g], vn[rows, cols]) + bs_ref[0, :, cols]
            o_ref[rows, cols] = (uv[rows, cols] * s).astype(o_ref.dtype)


def _sgu(x, mod, w_a, ln_g, ln_b, w_s, b_s_full, l, tm, t_ctx, l_lat):
    t = x.shape[0]
    row = functools.partial(_mod_row, tm=tm, t_ctx=t_ctx, l_lat=l_lat)
    return pl.pallas_call(
        functools.partial(_sgu_kernel, tm=tm),
        grid=(t // tm,),
        in_specs=[pl.BlockSpec((tm, D_MODEL), lambda i: (i, 0)),
                  pl.BlockSpec((1, 1, D_MODEL), lambda i: (row(i), 0, 0)),
                  pl.BlockSpec((1, 1, D_MODEL), lambda i: (row(i), 0, 1)),
                  pl.BlockSpec((1, D_MODEL, SEG_A), lambda i: (l, 0, 0)),
                  pl.BlockSpec((1, 1, HALF), lambda i: (l, 0, 0)),
                  pl.BlockSpec((1, 1, HALF), lambda i: (l, 0, 0)),
                  pl.BlockSpec((1, SGU_GROUPS, SGU_CHUNK, SGU_CHUNK), lambda i: (l, 0, 0, 0)),
                  pl.BlockSpec((1, SGU_CHUNK, HALF), lambda i: (l, 0, 0))],
        out_specs=pl.BlockSpec((tm, HALF), lambda i: (i, 0)),
        out_shape=jax.ShapeDtypeStruct((t, HALF), BF16),
        compiler_params=_cparams("parallel"),
        name="sgu",
    )(x, mod, mod, w_a, ln_g, ln_b, w_s, b_s_full)


def _rope_swap(x):
    lane = lax.broadcasted_iota(jnp.int32, x.shape, 1)
    up = pltpu.roll(x, LANES - 16, axis=1)
    dn = pltpu.roll(x, 16, axis=1)
    return jnp.where((lane & 16) == 0, up, dn)


def _head_rms(x, seg_ref, g):
    ms = _dot((x * x).astype(BF16), seg_ref[...])
    return x * lax.rsqrt(ms + 1e-6) * g


def _kvprep_kernel(k_ref, v_ref, g_ref, seg_ref, cos_ref, sin_ref, *out_refs, rope):
    kn = _head_rms(k_ref[...], seg_ref, g_ref[...])
    if rope:
        kr_ref, vb_ref = out_refs
        kn = kn * cos_ref[...] + _rope_swap(kn) * sin_ref[...]
    else:
        kn_ref, kr_ref, vb_ref = out_refs
        kn_ref[...] = kn
    kr_ref[...] = kn.astype(BF16)
    vb_ref[...] = v_ref[...].astype(BF16)


def _kvprep(seg_b, row_off, b, l, k_norm2, seg_mat, cos, sin, rope, tk):
    t = b * l
    off = row_off // tk
    lb = l // tk
    out_shape = [jax.ShapeDtypeStruct((t, KV_W), BF16), jax.ShapeDtypeStruct((t, KV_W), BF16)]
    out_specs = [pl.BlockSpec((tk, KV_W), lambda i: (i, 0)), pl.BlockSpec((tk, KV_W), lambda i: (i, 0))]
    if not rope:
        out_shape = [jax.ShapeDtypeStruct((t, KV_W), F32)] + out_shape
        out_specs = [pl.BlockSpec((tk, KV_W), lambda i: (i, 0))] + out_specs
    return pl.pallas_call(
        functools.partial(_kvprep_kernel, rope=rope),
        grid=(t // tk,),
        in_specs=[pl.BlockSpec((tk, KV_W), lambda i: (off + i, (B_OFF + HALF) // KV_W)),
                  pl.BlockSpec((tk, KV_W), lambda i: (off + i, (B_OFF + HALF) // KV_W + 1)),
                  pl.BlockSpec((1, KV_W), lambda i: (0, 0)),
                  pl.BlockSpec((KV_W, KV_W), lambda i: (0, 0)),
                  pl.BlockSpec((tk, KV_W), lambda i: (i % lb, 0)),
                  pl.BlockSpec((tk, KV_W), lambda i: (i % lb, 0))],
        out_specs=out_specs,
        out_shape=out_shape,
        compiler_params=_cparams("parallel"),
        name="kvprep",
    )(seg_b, seg_b, k_norm2, seg_mat, cos, sin)


def _attn_kernel(q_ref, g_ref, seg_ref, cos_ref, sin_ref, k_ref, v_ref, *rest, rope, tq):
    o_ref = rest[-1]
    qs = []
    for s in range(HALF // LANES):
        q = _head_rms(q_ref[:, s * LANES:(s + 1) * LANES], seg_ref, g_ref[...])
        if rope:
            q = q * cos_ref[...] + _rope_swap(q) * sin_ref[...]
        qs.append((q * (HEAD ** -0.5 * math.log2(math.e))).astype(BF16))
    k = k_ref[0]
    v = v_ref[0]
    for g in range(N_KV):
        kg = k[:, g * HEAD:(g + 1) * HEAD]
        vg = v[:, g * HEAD:(g + 1) * HEAD]
        heads = range(g * GQA, (g + 1) * GQA)
        qg = jnp.concatenate([qs[h // 2][:, (h % 2) * HEAD:(h % 2 + 1) * HEAD] for h in heads], axis=0)
        s = _dot_nt(qg, kg)
        e = jnp.exp2(s - jnp.max(s, axis=-1, keepdims=True))
        o = _dot(e.astype(BF16), vg) / jnp.sum(e, axis=-1, keepdims=True)
        for j, h in enumerate(heads):
            o_ref[:, h * HEAD:(h + 1) * HEAD] = o[j * tq:(j + 1) * tq].astype(o_ref.dtype)


def _attention(seg_b, row_off, b, l, kfull, vfull, q_norm2, seg_mat, cos, sin, rope, tq, prev=None):
    t_all = seg_b.shape[0]
    off = row_off // tq
    lb = l // tq
    lk = kfull.shape[1]
    in_specs = [pl.BlockSpec((tq, HALF), lambda bi, i: (off + bi * lb + i, B_OFF // HALF)),
                pl.BlockSpec((1, LANES), lambda bi, i: (0, 0)),
                pl.BlockSpec((KV_W, KV_W), lambda bi, i: (0, 0)),
                pl.BlockSpec((tq, LANES), lambda bi, i: (i, 0)),
                pl.BlockSpec((tq, LANES), lambda bi, i: (i, 0)),
                pl.BlockSpec((1, lk, KV_W), lambda bi, i: (bi, 0, 0)),
                pl.BlockSpec((1, lk, KV_W), lambda bi, i: (bi, 0, 0))]
    args = [seg_b, q_norm2, seg_mat, cos, sin, kfull, vfull]
    aliases = {}
    if prev is not None:
        aliases = {len(args): 0}
        in_specs.append(pl.BlockSpec(memory_space=pl.ANY))
        args.append(prev)
    return pl.pallas_call(
        functools.partial(_attn_kernel, rope=rope, tq=tq),
        grid=(b, lb),
        in_specs=in_specs,
        out_specs=pl.BlockSpec((tq, HALF), lambda bi, i: (off + bi * lb + i, 0)),
        out_shape=jax.ShapeDtypeStruct((t_all, HALF), BF16),
        input_output_aliases=aliases,
        compiler_params=_cparams("parallel", "parallel"),
        name="attention",
    )(*args)


def _split3(x):
    h = x.astype(BF16)
    r1 = x - h.astype(F32)
    m = r1.astype(BF16)
    lo = (r1 - m.astype(F32)).astype(BF16)
    return h, m, lo


def _wkv_prep(x, nb_row, d, c, mu_rkv, mu_lo, w0, a0, wwa, k_k, k_a, r_k, seg1, y_ref, rev):
    rows = lax.broadcasted_iota(jnp.int32, (c, 1), 0)
    edge = (c - 1) if rev else 0

    def shifted(cur, nb):
        rolled = pltpu.roll(cur, (c - 1) if rev else 1, axis=0)
        return jnp.where(rows == edge, nb, rolled)

    rkv = x[:, :RKV_W]
    lo = x[:, RKV_W + 2 * LORA * d:RKV_W + 2 * LORA * (d + 1)]
    f = rkv + mu_rkv * (shifted(rkv, nb_row[:, :RKV_W]) - rkv)
    fl = lo + mu_lo * (shifted(lo, nb_row[:, RKV_W + 2 * LORA * d:RKV_W + 2 * LORA * (d + 1)]) - lo)
    r = f[:, :HALF]
    k = f[:, HALF:2 * HALF]
    v = f[:, 2 * HALF:]
    lane = lax.broadcasted_iota(jnp.int32, fl.shape, 1)
    lin = _dot(jnp.where(lane < LORA, jnp.tanh(fl), fl).astype(BF16), wwa)
    lw = (-math.exp(-0.5)) * jax.nn.sigmoid(w0 + lin[:, :HALF])
    asig = jax.nn.sigmoid(a0 + lin[:, HALF:])
    kk = k * k_k
    ss = _dot((kk * kk).astype(BF16), seg1)
    kkn = kk / jnp.maximum(jnp.sqrt(ss), 1e-12)
    kmod = k * (1.0 + (asig - 1.0) * k_a)
    bonus = _dot((r * kmod * r_k).astype(BF16), seg1) * v
    y_ref[:, HALF:] = bonus

    ti = lax.broadcasted_iota(jnp.int32, (c, c), 0)
    si = lax.broadcasted_iota(jnp.int32, (c, c), 1)
    incl = (si >= ti) if rev else (si <= ti)
    strict = (si > ti) if rev else (si < ti)
    tri = incl.astype(BF16)
    h3, m3, l3 = _split3(lw)
    cum = _dot(tri, h3) + _dot(tri, m3) + _dot(tri, l3)
    ref = cum[c // 2:c // 2 + 1]
    end = 0 if rev else c - 1
    cum_end = cum[end:end + 1]
    g = cum - ref
    e_pos = jnp.exp(g)
    e_neg = jnp.exp(-g)
    e_ref = jnp.exp(ref)
    e_tot = jnp.exp(cum_end)
    e_end = jnp.exp(cum_end - ref)
    at_c = -kkn * jnp.exp(g - lw)
    rt_c = r * e_pos
    bt = kkn * asig * e_neg
    kt = kmod * e_neg
    return dict(at_c=at_c, rt_c=rt_c, bt=bt, kt=kt, at_true=at_c * e_ref, rt_true=rt_c * e_ref,
                bh=bt * e_end, kh=kt * e_end, v=v, e_tot=e_tot, incl=incl, strict=strict)


def _wkv_chains(preps, c, s_ref, y_refs):
    lane = lax.broadcasted_iota(jnp.int32, (c, LANES), 1)
    lo = lane < HEAD
    hi = jnp.logical_not(lo)
    chains = [(d, p) for d in range(2) for p in range(N_HEADS // 2)]

    def slab(d, p, name):
        return preps[d][name][:, p * LANES:(p + 1) * LANES]

    def keep(mask, x):
        return jnp.where(mask, x, 0.0)

    n_pow, a_ak, m_all, vsw, x_cur = {}, {}, {}, {}, {}
    for ch in chains:
        d, p = ch
        at_c, rt_c = slab(d, p, "at_c"), slab(d, p, "rt_c")
        lhs = jnp.concatenate([keep(lo, at_c), keep(hi, at_c), keep(lo, rt_c), keep(hi, rt_c)], axis=0)
        rhs = jnp.concatenate([slab(d, p, "bt"), slab(d, p, "kt")], axis=0)
        g = _dot_nt(lhs.astype(BF16), rhs.astype(BF16))
        strict, incl = preps[d]["strict"], preps[d]["incl"]
        incl2 = jnp.concatenate([incl, incl], axis=1)
        n_pow[ch] = [keep(strict, g[h * c:(h + 1) * c, :c]).astype(BF16) for h in range(2)]
        a_ak[ch] = [keep(strict, g[h * c:(h + 1) * c, c:]).astype(BF16) for h in range(2)]
        m_all[ch] = [keep(incl2, g[(2 + h) * c:(3 + h) * c, :]).astype(BF16) for h in range(2)]
        v_sw = pltpu.roll(slab(d, p, "v"), HEAD, axis=1)
        vsw[ch] = [keep(hi, v_sw).astype(BF16), keep(lo, v_sw).astype(BF16)]
    for ch in chains:
        d, p = ch
        at_true = slab(d, p, "at_true")
        x_cur[ch] = [keep(lo, at_true) + _dot(a_ak[ch][0], vsw[ch][0]),
                     keep(hi, at_true) + _dot(a_ak[ch][1], vsw[ch][1])]
    steps = int(math.log2(c))
    for j in range(steps):
        for ch in chains:
            for h in range(2):
                xb = x_cur[ch][h].astype(BF16)
                if j + 1 < steps:
                    out = _dot(n_pow[ch][h], jnp.concatenate([xb, n_pow[ch][h]], axis=1))
                    x_cur[ch][h] = x_cur[ch][h] + out[:, :LANES]
                    n_pow[ch][h] = out[:, LANES:].astype(BF16)
                else:
                    x_cur[ch][h] = x_cur[ch][h] + _dot(n_pow[ch][h], xb)
    for ch in chains:
        d, p = ch
        s0 = s_ref[d, p]
        x0, x1 = x_cur[ch]
        st = _dot_nt(jnp.concatenate([x0, x1, slab(d, p, "rt_true")], axis=0).astype(BF16), s0.astype(BF16))
        w0 = jnp.concatenate([keep(hi, st[:c] + x0).astype(BF16), vsw[ch][0]], axis=0)
        w1 = jnp.concatenate([keep(lo, st[c:2 * c] + x1).astype(BF16), vsw[ch][1]], axis=0)
        y_sw = st[2 * c:] + _dot(m_all[ch][0], w0) + _dot(m_all[ch][1], w1)
        y_refs[d][:, p * LANES:(p + 1) * LANES] = pltpu.roll(y_sw, HEAD, axis=1)
        bh, kh = slab(d, p, "bh"), slab(d, p, "kh")
        kb = jnp.concatenate([keep(lo, bh), keep(lo, kh), keep(hi, bh), keep(hi, kh)], axis=0).astype(BF16)
        e_tot = preps[d]["e_tot"][:, p * LANES:(p + 1) * LANES]
        s_ref[d, p] = s0 * e_tot + _dot_tn(jnp.concatenate([w0, w1], axis=0), kb)


def _wkv_kernel(*refs, c, n_c, latent):
    if latent:
        (x0_ref, x1_ref, p0_ref, n1_ref, s0_ref, mu_ref, w0_ref, a0_ref, wwa_ref, kk_ref, ka_ref, rk_ref,
         seg_ref, y0_ref, y1_ref, sf_ref, s_ref) = refs
    else:
        (x0_ref, x1_ref, p0_ref, n1_ref, mu_ref, w0_ref, a0_ref, wwa_ref, kk_ref, ka_ref, rk_ref,
         seg_ref, y0_ref, y1_ref, sf_ref, s_ref) = refs
    i = pl.program_id(1)

    @pl.when(i == 0)
    def _():
        s_ref[...] = s0_ref[0] if latent else jnp.zeros(s_ref.shape, F32)

    inner = (i > 0).astype(F32)
    preps = []
    for d, (x_ref, nb_ref, y_ref) in enumerate(((x0_ref, p0_ref, y0_ref), (x1_ref, n1_ref, y1_ref))):
        nb = nb_ref[SUBLANES - 1:SUBLANES, :] if d == 0 else nb_ref[0:1, :]
        preps.append(_wkv_prep(x_ref[...], nb * inner, d, c,
                               mu_ref[d, :, :RKV_W], mu_ref[d, :, RKV_W:], w0_ref[d], a0_ref[d], wwa_ref[d],
                               kk_ref[...], ka_ref[...], rk_ref[...], seg_ref[...], y_ref, rev=(d == 1)))
    _wkv_chains(preps, c, s_ref, (y0_ref, y1_ref))

    @pl.when(i == n_c - 1)
    def _():
        sf_ref[0] = s_ref[...]


def _wkv(seg_c, row_off, b, l, c, s0, mu, w0, a0, wwa, k_k, k_a, r_k, seg1):
    t = b * l
    n_c = l // c
    t_all = seg_c.shape[0]
    cb = row_off // c
    c8 = c // SUBLANES
    r8 = row_off // SUBLANES
    last8 = t_all // SUBLANES - 1
    latent = s0 is not None
    const2 = lambda bi, i: (0, 0)
    const3 = lambda bi, i: (0, 0, 0)
    in_specs = [pl.BlockSpec((c, SEG_C), lambda bi, i: (cb + bi * n_c + i, 0)),
                pl.BlockSpec((c, SEG_C), lambda bi, i: (cb + bi * n_c + n_c - 1 - i, 0)),
                pl.BlockSpec((SUBLANES, SEG_C),
                             lambda bi, i: (jnp.maximum(r8 + (bi * n_c + i) * c8 - 1, 0), 0)),
                pl.BlockSpec((SUBLANES, SEG_C),
                             lambda bi, i: (jnp.minimum(r8 + (bi * n_c + n_c - i) * c8, last8), 0))]
    args = [seg_c, seg_c, seg_c, seg_c]
    if latent:
        in_specs.append(pl.BlockSpec((1, 2, N_HEADS // 2, LANES, LANES), lambda bi, i: (bi, 0, 0, 0, 0)))
        args.append(s0)
    in_specs += [pl.BlockSpec(mu.shape, const3), pl.BlockSpec(w0.shape, const3), pl.BlockSpec(a0.shape, const3),
                 pl.BlockSpec(wwa.shape, const3), pl.BlockSpec(k_k.shape, const2), pl.BlockSpec(k_a.shape, const2),
                 pl.BlockSpec(r_k.shape, const2), pl.BlockSpec(seg1.shape, const2)]
    args += [mu, w0, a0, wwa, k_k, k_a, r_k, seg1]
    return pl.pallas_call(
        functools.partial(_wkv_kernel, c=c, n_c=n_c, latent=latent),
        grid=(b, n_c),
        in_specs=in_specs,
        out_specs=[pl.BlockSpec((c, 2 * HALF), lambda bi, i: (bi * n_c + i, 0)),
                   pl.BlockSpec((c, 2 * HALF), lambda bi, i: (bi * n_c + n_c - 1 - i, 0)),
                   pl.BlockSpec((1, 2, N_HEADS // 2, LANES, LANES), lambda bi, i: (bi, 0, 0, 0, 0))],
        out_shape=[jax.ShapeDtypeStruct((t, 2 * HALF), F32), jax.ShapeDtypeStruct((t, 2 * HALF), F32),
                   jax.ShapeDtypeStruct((b, 2, N_HEADS // 2, LANES, LANES), F32)],
        scratch_shapes=[pltpu.VMEM((2, N_HEADS // 2, LANES, LANES), F32)],
        compiler_params=_cparams("parallel", "arbitrary"),
        name="wkv",
    )(*args)


def _rwkv_post_kernel(y0_ref, y1_ref, gd_ref, segm_ref, g2_ref, lg_ref, lb_ref, *rest):
    o_ref = rest[-1]
    ys = y0_ref[:, :HALF] + y1_ref[:, :HALF]
    bonus = y0_ref[:, HALF:] + y1_ref[:, HALF:]
    mu = _dot(ys.astype(BF16), segm_ref[...])
    yc = ys - mu
    var = _dot((yc * yc).astype(BF16), segm_ref[...])
    gn = yc * lax.rsqrt(var + GN_EPS) * lg_ref[0] + lb_ref[0]
    gate = _dot(jax.nn.sigmoid(gd_ref[...]).astype(BF16), g2_ref[0])
    o_ref[...] = ((gn + bonus) * gate).astype(o_ref.dtype)


def _rwkv_post(y0, y1, seg_c, row_off, segm, g2, lnx_g, lnx_b, l, tm, prev=None):
    t = y0.shape[0]
    t_all = seg_c.shape[0]
    off = row_off // tm
    in_specs = [pl.BlockSpec((tm, 2 * HALF), lambda i: (i, 0)),
                pl.BlockSpec((tm, 2 * HALF), lambda i: (i, 0)),
                pl.BlockSpec((tm, GATE_LORA), lambda i: (off + i, (SEG_C - GATE_LORA) // GATE_LORA)),
                pl.BlockSpec((HALF, HALF), lambda i: (0, 0)),
                pl.BlockSpec((1, GATE_LORA, HALF), lambda i: (l, 0, 0)),
                pl.BlockSpec((1, 1, HALF), lambda i: (l, 0, 0)),
                pl.BlockSpec((1, 1, HALF), lambda i: (l, 0, 0))]
    args = [y0, y1, seg_c, segm, g2, lnx_g, lnx_b]
    aliases = {}
    if prev is not None:
        aliases = {len(args): 0}
        in_specs.append(pl.BlockSpec(memory_space=pl.ANY))
        args.append(prev)
    return pl.pallas_call(
        _rwkv_post_kernel,
        grid=(t // tm,),
        in_specs=in_specs,
        out_specs=pl.BlockSpec((tm, HALF), lambda i: (off + i, 0)),
        out_shape=jax.ShapeDtypeStruct((t_all, HALF), BF16),
        input_output_aliases=aliases,
        compiler_params=_cparams("parallel"),
        name="rwkv_post",
    )(*args)


def _merge_kernel(x_ref, sh_ref, sc_ref, g1_ref, oa_ref, ob_ref, oc_ref, wg_ref, wb_ref, wo_ref, lg_ref, lb_ref,
                  o_ref, *, alpha):
    x = x_ref[...]
    h = (x * (1.0 + sc_ref[0]) + sh_ref[0]).astype(BF16)
    acc = None
    for j, br in enumerate((oa_ref, ob_ref, oc_ref)):
        gate = jax.nn.sigmoid(_dot(h, wg_ref[0, :, j * D_MODEL:(j + 1) * D_MODEL]))
        term = gate * _dot(br[...], wb_ref[0, j])
        acc = term if acc is None else acc + term
    mixed = _dot(acc.astype(BF16), wo_ref[0])
    o_ref[...] = _layer_norm(alpha * x + g1_ref[0] * mixed, lg_ref[0], lb_ref[0])


def _merge(x, o_a, o_b, o_c, mod, w_g, w_branch, w_out, ln_g, ln_b, l, tm, t_ctx, l_lat, alpha):
    t = x.shape[0]
    row = functools.partial(_mod_row, tm=tm, t_ctx=t_ctx, l_lat=l_lat)
    tok = lambda w: pl.BlockSpec((tm, w), lambda i: (i, 0))
    modspec = lambda blk: pl.BlockSpec((1, 1, D_MODEL), lambda i: (row(i), 0, blk))
    return pl.pallas_call(
        functools.partial(_merge_kernel, alpha=alpha),
        grid=(t // tm,),
        in_specs=[tok(D_MODEL), modspec(0), modspec(1), modspec(2), tok(HALF), tok(HALF), tok(HALF),
                  pl.BlockSpec((1, D_MODEL, SEG_G), lambda i: (l, 0, 0)),
                  pl.BlockSpec((1, 3, HALF, D_MODEL), lambda i: (l, 0, 0, 0)),
                  pl.BlockSpec((1, D_MODEL, D_MODEL), lambda i: (l, 0, 0)),
                  pl.BlockSpec((1, 1, D_MODEL), lambda i: (l, 0, 0)),
                  pl.BlockSpec((1, 1, D_MODEL), lambda i: (l, 0, 0))],
        out_specs=tok(D_MODEL),
        out_shape=jax.ShapeDtypeStruct((t, D_MODEL), F32),
        compiler_params=_cparams("parallel"),
        name="merge",
    )(x, mod, mod, mod, o_a, o_b, o_c, w_g, w_branch, w_out, ln_g, ln_b)


def _ffn_kernel(x_ref, sh_ref, sc_ref, g2_ref, wu_ref, wd_ref, lg_ref, lb_ref, o_ref, h_ref, acc_ref, *, alpha, n_f):
    j = pl.program_id(1)

    @pl.when(j == 0)
    def _():
        h_ref[...] = (x_ref[...] * (1.0 + sc_ref[0]) + sh_ref[0]).astype(BF16)
        acc_ref[...] = jnp.zeros(acc_ref.shape, F32)

    u = jnp.maximum(_dot(h_ref[...], wu_ref[0]), 0.0)
    acc_ref[...] += _dot((u * u).astype(BF16), wd_ref[0])

    @pl.when(j == n_f - 1)
    def _():
        o_ref[...] = _layer_norm(alpha * x_ref[...] + g2_ref[0] * acc_ref[...], lg_ref[0], lb_ref[0])


def _ffn(x, mod, w_up, w_down, ln_g, ln_b, l, tm, tf, t_ctx, l_lat, alpha):
    t = x.shape[0]
    n_f = D_FF // tf
    row = functools.partial(_mod_row, tm=tm, t_ctx=t_ctx, l_lat=l_lat)
    modspec = lambda blk: pl.BlockSpec((1, 1, D_MODEL), lambda i, j: (row(i), 0, blk))
    return pl.pallas_call(
        functools.partial(_ffn_kernel, alpha=alpha, n_f=n_f),
        grid=(t // tm, n_f),
        in_specs=[pl.BlockSpec((tm, D_MODEL), lambda i, j: (i, 0)),
                  modspec(3), modspec(4), modspec(5),
                  pl.BlockSpec((1, D_MODEL, tf), lambda i, j: (l, 0, j)),
                  pl.BlockSpec((1, tf, D_MODEL), lambda i, j: (l, j, 0)),
                  pl.BlockSpec((1, 1, D_MODEL), lambda i, j: (l, 0, 0)),
                  pl.BlockSpec((1, 1, D_MODEL), lambda i, j: (l, 0, 0))],
        out_specs=pl.BlockSpec((tm, D_MODEL), lambda i, j: (i, 0)),
        out_shape=jax.ShapeDtypeStruct((t, D_MODEL), F32),
        scratch_shapes=[pltpu.VMEM((tm, D_MODEL), BF16), pltpu.VMEM((tm, D_MODEL), F32)],
        compiler_params=_cparams("parallel", "arbitrary"),
        name="ffn",
    )(x, mod, mod, mod, w_up, w_down, ln_g, ln_b)


def _rope_tables(l):
    pos = jnp.arange(l, dtype=jnp.int32)
    row = (pos // GRID_W).astype(F32)
    col = (pos % GRID_W).astype(F32)
    half = HEAD // 2
    inv_freq = ROPE_THETA ** (-jnp.arange(0, half, 2, dtype=F32) / half)
    ang_r = row[:, None] * inv_freq[None, :]
    ang_c = col[:, None] * inv_freq[None, :]
    cos = jnp.concatenate([jnp.cos(ang_r), jnp.cos(ang_r), jnp.cos(ang_c), jnp.cos(ang_c)], axis=-1)
    sin = jnp.concatenate([-jnp.sin(ang_r), jnp.sin(ang_r), -jnp.sin(ang_c), jnp.sin(ang_c)], axis=-1)
    return jnp.tile(cos, (1, LANES // HEAD)), jnp.tile(sin, (1, LANES // HEAD))


def _head_block_matrix(width, value):
    idx = jnp.arange(width) // HEAD
    return jnp.where(idx[:, None] == idx[None, :], value, 0.0).astype(BF16)


def _pair_states(s):
    b = s.shape[0]
    s = s.reshape(b, 2, N_HEADS // 2, 2, HEAD, HEAD)
    z = jnp.zeros_like(s[:, :, :, 0])
    top = jnp.concatenate([z, s[:, :, :, 1]], axis=-1)
    bot = jnp.concatenate([s[:, :, :, 0], z], axis=-1)
    return jnp.concatenate([top, bot], axis=-2)


def _unpair_states(sp):
    b = sp.shape[0]
    s = jnp.stack([sp[..., HEAD:, :HEAD], sp[..., :HEAD, HEAD:]], axis=3)
    return s.reshape(b, 2, N_HEADS, HEAD, HEAD)


def _pick_tile(pref, *sizes):
    return min(pref, functools.reduce(math.gcd, sizes))


def kernel(x_prompt, x_sample, cache_k, cache_v, state_wkv, c, c_ctx, w_ada, b_ada, w_in, sgu_ln_g, sgu_ln_b, sgu_w, sgu_b, q_norm, k_norm, rwkv_mu, rwkv_w0, rwkv_w2, rwkv_a0, rwkv_a2, rwkv_k_k, rwkv_k_a, rwkv_r_k, rwkv_g2, rwkv_lnx_g, rwkv_lnx_b, w_branch, w_out, ln1_g, ln1_b, w_up, w_down, ln2_g, ln2_b):
    depth = w_in.shape[0]
    b_ctx, l_ctx, _ = x_prompt.shape
    b_lat, l_lat, _ = x_sample.shape
    past = cache_k.shape[2]
    t_ctx = b_ctx * l_ctx
    t_lat = b_lat * l_lat
    alpha = (2 * depth) ** 0.25

    tm = _pick_tile(512, t_ctx, l_lat)
    tm_ffn = _pick_tile(1024, t_ctx, l_lat)
    tk = _pick_tile(512, l_ctx, l_lat)
    tq_ctx = _pick_tile(256, l_ctx)
    tq_lat = _pick_tile(256, l_lat)
    c_ctx_chunk = _pick_tile(128, l_ctx)
    c_lat_chunk = _pick_tile(128, l_lat)

    n_rows = 1 + b_lat
    pad_rows = -n_rows % 16
    cvec = jnp.concatenate([c_ctx[None, :], c, jnp.zeros((pad_rows, D_MODEL), F32)], axis=0)
    mod_all = _ada(cvec, w_ada.astype(BF16), b_ada[:, None, :])

    c_lo = SEG_A + SEG_B
    w_a_b = w_in[:, :, :SEG_A].astype(BF16)
    w_g_b = w_in[:, :, c_lo + SEG_C:].astype(BF16)
    w_bc_b = jnp.concatenate([w_in[:, :, c_lo:c_lo + SEG_C], jnp.zeros((depth, D_MODEL, B_OFF - SEG_C), F32),
                              w_in[:, :, SEG_A:c_lo]], axis=-1).astype(BF16)
    w_branch_b = w_branch.astype(BF16)
    w_out_b = w_out.astype(BF16)
    w_up_b = w_up.astype(BF16)
    w_down_b = w_down.astype(BF16)
    sgu_w_b = sgu_w.astype(BF16)
    g2_b = rwkv_g2.astype(BF16)
    seg_mean2 = _head_block_matrix(KV_W, 1.0 / HEAD)
    seg_mean8 = _head_block_matrix(HALF, 1.0 / HEAD)
    seg_ones8 = _head_block_matrix(HALF, 1.0)
    cos_ctx, sin_ctx = _rope_tables(l_ctx)
    cos_lat, sin_lat = _rope_tables(l_lat)
    zeros_lora = jnp.zeros((depth, 2, LORA, HALF), F32)
    wwa = jnp.concatenate([jnp.concatenate([rwkv_w2, zeros_lora], axis=-1),
                           jnp.concatenate([zeros_lora, rwkv_a2], axis=-1)], axis=-2).astype(BF16)

    x = jnp.concatenate([x_prompt.reshape(t_ctx, D_MODEL), x_sample.reshape(t_lat, D_MODEL)], axis=0)
    b_s_full = jnp.repeat(jnp.swapaxes(sgu_b, 1, 2), HALF // SGU_GROUPS, axis=2)
    vec = lambda p: p[:, None, :]
    new_k, new_v, new_s = [], [], []
    for l in range(depth):
        mod = mod_all[l][:, None, :]
        seg_b = _modmm(x, mod, 0, 1, w_bc_b, l, tm=tm, t_ctx=t_ctx, l_lat=l_lat)
        seg_c = seg_b
        o_a = _sgu(x, mod, w_a_b, vec(sgu_ln_g), vec(sgu_ln_b), sgu_w_b, b_s_full, l, tm, t_ctx, l_lat)

        qn2 = jnp.tile(q_norm[l], LANES // HEAD)[None]
        kn2 = jnp.tile(k_norm[l], KV_W // HEAD)[None]
        kn_ctx, kr_ctx, vb_ctx = _kvprep(seg_b, 0, b_ctx, l_ctx, kn2, seg_mean2, cos_ctx, sin_ctx, False, tk)
        kr_lat, vb_lat = _kvprep(seg_b, t_ctx, b_lat, l_lat, kn2, seg_mean2, cos_lat, sin_lat, True, tk)
        new_k.append(kn_ctx.reshape(b_ctx, l_ctx, N_KV, HEAD))
        new_v.append(seg_b[:t_ctx, B_OFF + HALF + KV_W:].reshape(b_ctx, l_ctx, N_KV, HEAD))
        k_lat = jnp.concatenate([cache_k[:, l].reshape(b_lat, past, KV_W).astype(BF16),
                                 kr_lat.reshape(b_lat, l_lat, KV_W)], axis=1)
        v_lat = jnp.concatenate([cache_v[:, l].reshape(b_lat, past, KV_W).astype(BF16),
                                 vb_lat.reshape(b_lat, l_lat, KV_W)], axis=1)
        ob_ctx = _attention(seg_b, 0, b_ctx, l_ctx, kr_ctx.reshape(b_ctx, l_ctx, KV_W),
                            vb_ctx.reshape(b_ctx, l_ctx, KV_W), qn2, seg_mean2, cos_ctx, sin_ctx, False, tq_ctx)
        o_b = _attention(seg_b, t_ctx, b_lat, l_lat, k_lat, v_lat, qn2, seg_mean2, cos_lat, sin_lat, True, tq_lat,
                         prev=ob_ctx)

        rw = dict(mu=rwkv_mu[l][:, None, :], w0=rwkv_w0[l][:, None, :], a0=rwkv_a0[l][:, None, :], wwa=wwa[l],
                  k_k=rwkv_k_k[l][None], k_a=rwkv_k_a[l][None], r_k=rwkv_r_k[l].reshape(1, HALF), seg1=seg_ones8)
        y0c, y1c, s_ctx = _wkv(seg_c, 0, b_ctx, l_ctx, c_ctx_chunk, None, **rw)
        y0l, y1l, _ = _wkv(seg_c, t_ctx, b_lat, l_lat, c_lat_chunk, _pair_states(state_wkv[:, l]), **rw)
        new_s.append(_unpair_states(s_ctx))
        post = functools.partial(_rwkv_post, segm=seg_mean8, g2=g2_b, lnx_g=vec(rwkv_lnx_g), lnx_b=vec(rwkv_lnx_b),
                                 l=l, tm=tm)
        o_c = post(y0l, y1l, seg_c, t_ctx, prev=post(y0c, y1c, seg_c, 0))

        x = _merge(x, o_a, o_b, o_c, mod, w_g_b, w_branch_b, w_out_b, vec(ln1_g), vec(ln1_b), l, tm, t_ctx, l_lat, alpha)
        x = _ffn(x, mod, w_up_b, w_down_b, vec(ln2_g), vec(ln2_b), l, tm_ffn, 1024, t_ctx, l_lat, alpha)

    y = x[:t_ctx].reshape(b_ctx, l_ctx, D_MODEL)
    z = x[t_ctx:].reshape(b_lat, l_lat, D_MODEL)
    return (y, z, jnp.stack(new_k, axis=1), jnp.stack(new_v, axis=1), jnp.stack(new_s, axis=1))
```

```python
import functools
import math

import jax
import jax.numpy as jnp
from jax import lax
from jax.experimental import pallas as pl
from jax.experimental.pallas import tpu as pltpu

F32 = jnp.float32
BF16 = jnp.bfloat16

D_MODEL = 1024
HALF = D_MODEL // 2
HEAD = 64
N_HEADS = HALF // HEAD
N_KV = 2
GQA = N_HEADS // N_KV
KV_W = N_KV * HEAD
GRID_W = 64
SGU_CHUNK = 128
SGU_GROUPS = 4
LORA = 64
GATE_LORA = 128
D_FF = 4 * D_MODEL
ROPE_THETA = 10000.0
GN_EPS = 64e-5
SEG_A = 2 * HALF
SEG_B = HALF + 2 * KV_W
SEG_C = 3 * HALF + 4 * LORA + GATE_LORA
SEG_G = 3 * D_MODEL
RKV_W = 3 * HALF
B_OFF = 2048
SEG_BC = B_OFF + SEG_B
FFN_GROUP = 512
LANES = 128
NB_ROWS = 16
VMEM_LIMIT = 48 * 1024 * 1024


def _cparams(*sem):
    return pltpu.CompilerParams(dimension_semantics=sem, vmem_limit_bytes=VMEM_LIMIT)


def _dot(a, b):
    return jnp.dot(a, b, preferred_element_type=F32)


def _dot_nt(a, b):
    return lax.dot_general(a, b, (((1,), (1,)), ((), ())), preferred_element_type=F32)


def _dot_tn(a, b):
    return lax.dot_general(a, b, (((0,), (0,)), ((), ())), preferred_element_type=F32)


def _layer_norm(x, g, b, eps=1e-5):
    mu = jnp.mean(x, axis=-1, keepdims=True)
    xc = x - mu
    var = jnp.mean(xc * xc, axis=-1, keepdims=True)
    return xc * lax.rsqrt(var + eps) * g + b


def _ada_kernel(c_ref, w_ref, b_ref, o_ref):
    c = c_ref[...]
    s = (c * jax.nn.sigmoid(c)).astype(BF16)
    o_ref[0] = _dot(s, w_ref[0]) + b_ref[0]


def _ada(cvec, w_ada, b_ada):
    depth, _, n = w_ada.shape
    r = cvec.shape[0]
    tn = 1536
    return pl.pallas_call(
        _ada_kernel,
        grid=(depth, n // tn),
        in_specs=[pl.BlockSpec((r, D_MODEL), lambda l, j: (0, 0)),
                  pl.BlockSpec((1, D_MODEL, tn), lambda l, j: (l, 0, j)),
                  pl.BlockSpec((1, 1, tn), lambda l, j: (l, 0, j))],
        out_specs=pl.BlockSpec((1, r, tn), lambda l, j: (l, 0, j)),
        out_shape=jax.ShapeDtypeStruct((depth, r, n), F32),
        compiler_params=_cparams("parallel", "parallel"),
        name="ada",
    )(cvec, w_ada, b_ada)


def _mod_row(i, tm, t_ctx, l_lat):
    r = i * tm
    return jnp.where(r < t_ctx, 0, 1 + (r - t_ctx) // l_lat)


def _modmm_kernel(x_ref, sh_ref, sc_ref, w_ref, o_ref, h_ref):
    @pl.when(pl.program_id(1) == 0)
    def _():
        h_ref[...] = (x_ref[...] * (1.0 + sc_ref[0]) + sh_ref[0]).astype(BF16)

    o_ref[...] = _dot(h_ref[...], w_ref[0]).astype(o_ref.dtype)


def _modmm(x, mod, sh_blk, sc_blk, w, l, tm, t_ctx, l_lat, out_dtype=F32):
    t = x.shape[0]
    n = w.shape[2]
    tn = n if n <= 2048 else n // 2
    row = functools.partial(_mod_row, tm=tm, t_ctx=t_ctx, l_lat=l_lat)
    return pl.pallas_call(
        _modmm_kernel,
        grid=(t // tm, n // tn),
        in_specs=[pl.BlockSpec((tm, D_MODEL), lambda i, j: (i, 0)),
                  pl.BlockSpec((1, 1, D_MODEL), lambda i, j: (row(i), 0, sh_blk)),
                  pl.BlockSpec((1, 1, D_MODEL), lambda i, j: (row(i), 0, sc_blk)),
                  pl.BlockSpec((1, D_MODEL, tn), lambda i, j: (l, 0, j))],
        out_specs=pl.BlockSpec((tm, tn), lambda i, j: (i, j)),
        out_shape=jax.ShapeDtypeStruct((t, n), out_dtype),
        scratch_shapes=[pltpu.VMEM((tm, D_MODEL), BF16)],
        compiler_params=_cparams("parallel", "arbitrary"),
        name="modmm",
    )(x, mod, mod, w)


def _sgu_kernel(x_ref, sh_ref, sc_ref, wa_ref, g_ref, b_ref, ws_ref, bs_ref, o_ref, *, tm):
    h = (x_ref[...] * (1.0 + sc_ref[0]) + sh_ref[0]).astype(BF16)
    uv = _dot(h, wa_ref[0])
    vn = _layer_norm(uv[:, HALF:], g_ref[0], b_ref[0]).astype(BF16)
    gc = HALF // SGU_GROUPS
    for n in range(tm // SGU_CHUNK):
        rows = slice(n * SGU_CHUNK, (n + 1) * SGU_CHUNK)
        for g in range(SGU_GROUPS):
            cols = slice(g * gc, (g + 1) * gc)
            s = _dot(ws_ref[0, g], vn[rows, cols]) + bs_ref[0, :, cols]
            o_ref[rows, cols] = (uv[rows, cols] * s).astype(o_ref.dtype)


def _sgu(x, mod, w_a, ln_g, ln_b, w_s, b_s_full, l, tm, t_ctx, l_lat):
    t = x.shape[0]
    row = functools.partial(_mod_row, tm=tm, t_ctx=t_ctx, l_lat=l_lat)
    return pl.pallas_call(
        functools.partial(_sgu_kernel, tm=tm),
        grid=(t // tm,),
        in_specs=[pl.BlockSpec((tm, D_MODEL), lambda i: (i, 0)),
                  pl.BlockSpec((1, 1, D_MODEL), lambda i: (row(i), 0, 0)),
                  pl.BlockSpec((1, 1, D_MODEL), lambda i: (row(i), 0, 1)),
                  pl.BlockSpec((1, D_MODEL, SEG_A), lambda i: (l, 0, 0)),
                  pl.BlockSpec((1, 1, HALF), lambda i: (l, 0, 0)),
                  pl.BlockSpec((1, 1, HALF), lambda i: (l, 0, 0)),
                  pl.BlockSpec((1, SGU_GROUPS, SGU_CHUNK, SGU_CHUNK), lambda i: (l, 0, 0, 0)),
                  pl.BlockSpec((1, SGU_CHUNK, HALF), lambda i: (l, 0, 0))],
        out_specs=pl.BlockSpec((tm, HALF), lambda i: (i, 0)),
        out_shape=jax.ShapeDtypeStruct((t, HALF), BF16),
        compiler_params=_cparams("parallel"),
        name="sgu",
    )(x, mod, mod, w_a, ln_g, ln_b, w_s, b_s_full)


def _rope_swap(x):
    lane = lax.broadcasted_iota(jnp.int32, x.shape, 1)
    up = pltpu.roll(x, LANES - 16, axis=1)
    dn = pltpu.roll(x, 16, axis=1)
    return jnp.where((lane & 16) == 0, up, dn)


def _head_rms(x, seg_ref, g):
    ms = _dot((x * x).astype(BF16), seg_ref[...])
    return x * lax.rsqrt(ms + 1e-6) * g


def _kvprep_kernel(k_ref, v_ref, g_ref, seg_ref, cos_ref, sin_ref, *out_refs, rope):
    kn = _head_rms(k_ref[...].astype(F32), seg_ref, g_ref[...])
    if rope:
        kt_ref, vb_ref = out_refs
        kn = kn * cos_ref[...] + _rope_swap(kn) * sin_ref[...]
    else:
        kn_ref, kt_ref, vb_ref = out_refs
        kn_ref[...] = kn
    kt_ref[0] = kn.T.astype(BF16)
    vb_ref[...] = v_ref[...].astype(BF16)


def _kvprep(seg_b, row_off, b, l, k_norm2, seg_mat, cos, sin, rope, tk):
    t = b * l
    off = row_off // tk
    lb = l // tk
    out_shape = [jax.ShapeDtypeStruct((b, KV_W, l), BF16), jax.ShapeDtypeStruct((t, KV_W), BF16)]
    out_specs = [pl.BlockSpec((1, KV_W, tk), lambda i: (i // lb, 0, i % lb)),
                 pl.BlockSpec((tk, KV_W), lambda i: (i, 0))]
    if not rope:
        out_shape = [jax.ShapeDtypeStruct((t, KV_W), F32)] + out_shape
        out_specs = [pl.BlockSpec((tk, KV_W), lambda i: (i, 0))] + out_specs
    return pl.pallas_call(
        functools.partial(_kvprep_kernel, rope=rope),
        grid=(t // tk,),
        in_specs=[pl.BlockSpec((tk, KV_W), lambda i: (off + i, (B_OFF + HALF) // KV_W)),
                  pl.BlockSpec((tk, KV_W), lambda i: (off + i, (B_OFF + HALF) // KV_W + 1)),
                  pl.BlockSpec((1, KV_W), lambda i: (0, 0)),
                  pl.BlockSpec((KV_W, KV_W), lambda i: (0, 0)),
                  pl.BlockSpec((tk, KV_W), lambda i: (i % lb, 0)),
                  pl.BlockSpec((tk, KV_W), lambda i: (i % lb, 0))],
        out_specs=out_specs,
        out_shape=out_shape,
        compiler_params=_cparams("parallel"),
        name="kvprep",
    )(seg_b, seg_b, k_norm2, seg_mat, cos, sin)


def _attn_kernel(q_ref, g_ref, seg_ref, cos_ref, sin_ref, kt_ref, v_ref, *rest, rope, tq):
    o_ref = rest[-1]
    qs = []
    for s in range(HALF // LANES):
        q = _head_rms(q_ref[:, s * LANES:(s + 1) * LANES].astype(F32), seg_ref, g_ref[...])
        if rope:
            q = q * cos_ref[...] + _rope_swap(q) * sin_ref[...]
        qs.append((q * (HEAD ** -0.5 * math.log2(math.e))).astype(BF16))
    lo = lax.broadcasted_iota(jnp.int32, (tq, LANES), 1) < HEAD
    scores, probs, ratios = {}, {}, {}

    def qk(h):
        scores[h] = _dot(qs[h // 2], kt_ref[0, 2 * (h // GQA) + h % 2])

    def softmax(h):
        s = scores.pop(h)
        probs[h] = jnp.exp2(s - jnp.max(s, axis=-1, keepdims=True)).astype(BF16)

    def pv(h):
        g = h // GQA
        oe = _dot(probs.pop(h), v_ref[0, :, g * LANES:(g + 1) * LANES])
        sw = pltpu.roll(oe, HEAD, axis=1)
        ratios[h] = oe / sw if h % 2 == 0 else sw / oe
        if h % 2 == 1:
            pair = jnp.where(lo, ratios.pop(h - 1), ratios.pop(h))
            o_ref[:, (h // 2) * LANES:(h // 2 + 1) * LANES] = pair.astype(o_ref.dtype)

    for t in range(N_HEADS + 2):
        if t < N_HEADS:
            qk(t)
        if 0 <= t - 1 < N_HEADS:
            softmax(t - 1)
        if 0 <= t - 2 < N_HEADS:
            pv(t - 2)


def _attention(seg_b, row_off, b, l, kt4, v_ext, q_norm2, seg_mat, cos, sin, rope, tq, prev=None):
    t_all = seg_b.shape[0]
    off = row_off // tq
    lb = l // tq
    lk = kt4.shape[-1]
    in_specs = [pl.BlockSpec((tq, HALF), lambda bi, i: (off + bi * lb + i, B_OFF // HALF)),
                pl.BlockSpec((1, LANES), lambda bi, i: (0, 0)),
                pl.BlockSpec((KV_W, KV_W), lambda bi, i: (0, 0)),
                pl.BlockSpec((tq, LANES), lambda bi, i: (i, 0)),
                pl.BlockSpec((tq, LANES), lambda bi, i: (i, 0)),
                pl.BlockSpec((1, 2 * N_KV, LANES, lk), lambda bi, i: (bi, 0, 0, 0)),
                pl.BlockSpec((1, lk, N_KV * LANES), lambda bi, i: (bi, 0, 0))]
    args = [seg_b, q_norm2, seg_mat, cos, sin, kt4, v_ext]
    aliases = {}
    if prev is not None:
        aliases = {len(args): 0}
        in_specs.append(pl.BlockSpec(memory_space=pl.ANY))
        args.append(prev)
    return pl.pallas_call(
        functools.partial(_attn_kernel, rope=rope, tq=tq),
        grid=(b, lb),
        in_specs=in_specs,
        out_specs=pl.BlockSpec((tq, HALF), lambda bi, i: (off + bi * lb + i, 0)),
        out_shape=jax.ShapeDtypeStruct((t_all, HALF), BF16),
        input_output_aliases=aliases,
        compiler_params=_cparams("parallel", "parallel"),
        name="attention",
    )(*args)


def _split3(x):
    h = x.astype(BF16)
    r1 = x - h.astype(F32)
    m = r1.astype(BF16)
    lo = (r1 - m.astype(F32)).astype(BF16)
    return h, m, lo


def _wkv_prep(x, nb_row, d, c, mu_rkv, mu_lo, w0, a0, wwa, k_k, k_a, r_k, seg1, y_ref, rev):
    rows = lax.broadcasted_iota(jnp.int32, (c, 1), 0)
    edge = (c - 1) if rev else 0

    def shifted(cur, nb):
        rolled = pltpu.roll(cur, (c - 1) if rev else 1, axis=0)
        return jnp.where(rows == edge, nb, rolled)

    rkv = x[:, :RKV_W]
    lo = x[:, RKV_W + 2 * LORA * d:RKV_W + 2 * LORA * (d + 1)]
    f = rkv + mu_rkv * (shifted(rkv, nb_row[:, :RKV_W]) - rkv)
    fl = lo + mu_lo * (shifted(lo, nb_row[:, RKV_W + 2 * LORA * d:RKV_W + 2 * LORA * (d + 1)]) - lo)
    r = f[:, :HALF]
    k = f[:, HALF:2 * HALF]
    v = f[:, 2 * HALF:]
    lane = lax.broadcasted_iota(jnp.int32, fl.shape, 1)
    lin = _dot(jnp.where(lane < LORA, jnp.tanh(fl), fl).astype(BF16), wwa)
    lw = (-math.exp(-0.5)) * jax.nn.sigmoid(w0 + lin[:, :HALF])
    asig = jax.nn.sigmoid(a0 + lin[:, HALF:])
    kk = k * k_k
    ss = _dot((kk * kk).astype(BF16), seg1)
    kkn = kk / jnp.maximum(jnp.sqrt(ss), 1e-12)
    kmod = k * (1.0 + (asig - 1.0) * k_a)
    bonus = _dot((r * kmod * r_k).astype(BF16), seg1) * v
    y_ref[:, HALF:] = bonus

    ti = lax.broadcasted_iota(jnp.int32, (c, c), 0)
    si = lax.broadcasted_iota(jnp.int32, (c, c), 1)
    incl = (si >= ti) if rev else (si <= ti)
    strict = (si > ti) if rev else (si < ti)
    tri = incl.astype(BF16)
    h3, m3, l3 = _split3(lw)
    cum = _dot(tri, h3) + _dot(tri, m3) + _dot(tri, l3)
    ref = cum[c // 2:c // 2 + 1]
    end = 0 if rev else c - 1
    cum_end = cum[end:end + 1]
    g = cum - ref
    e_pos = jnp.exp(g)
    e_neg = jnp.exp(-g)
    e_ref = jnp.exp(ref)
    e_tot = jnp.exp(cum_end)
    e_end = jnp.exp(cum_end - ref)
    at_c = -kkn * jnp.exp(g - lw)
    rt_c = r * e_pos
    bt = kkn * asig * e_neg
    kt = kmod * e_neg
    return dict(at_c=at_c, rt_c=rt_c, bt=bt, kt=kt, at_true=at_c * e_ref, rt_true=rt_c * e_ref,
                bh=bt * e_end, kh=kt * e_end, v=v, e_tot=e_tot, incl=incl, strict=strict)


def _wkv_chains(preps, c, s_ref, y_refs):
    lane = lax.broadcasted_iota(jnp.int32, (c, LANES), 1)
    lo = lane < HEAD
    hi = jnp.logical_not(lo)
    chains = [(d, p) for d in range(2) for p in range(N_HEADS // 2)]

    def slab(d, p, name):
        return preps[d][name][:, p * LANES:(p + 1) * LANES]

    def keep(mask, x):
        return jnp.where(mask, x, 0.0)

    n_pow, a_ak, m_all, vsw, x_cur = {}, {}, {}, {}, {}
    for ch in chains:
        d, p = ch
        at_c, rt_c = slab(d, p, "at_c"), slab(d, p, "rt_c")
        lhs = jnp.concatenate([keep(lo, at_c), keep(hi, at_c), keep(lo, rt_c), keep(hi, rt_c)], axis=0)
        rhs = jnp.concatenate([slab(d, p, "bt"), slab(d, p, "kt")], axis=0)
        g = _dot_nt(lhs.astype(BF16), rhs.astype(BF16))
        strict, incl = preps[d]["strict"], preps[d]["incl"]
        incl2 = jnp.concatenate([incl, incl], axis=1)
        n_pow[ch] = [keep(strict, g[h * c:(h + 1) * c, :c]).astype(BF16) for h in range(2)]
        a_ak[ch] = [keep(strict, g[h * c:(h + 1) * c, c:]).astype(BF16) for h in range(2)]
        m_all[ch] = [keep(incl2, g[(2 + h) * c:(3 + h) * c, :]).astype(BF16) for h in range(2)]
        v_sw = pltpu.roll(slab(d, p, "v"), HEAD, axis=1)
        vsw[ch] = [keep(hi, v_sw).astype(BF16), keep(lo, v_sw).astype(BF16)]
    for ch in chains:
        d, p = ch
        at_true = slab(d, p, "at_true")
        x_cur[ch] = [keep(lo, at_true) + _dot(a_ak[ch][0], vsw[ch][0]),
                     keep(hi, at_true) + _dot(a_ak[ch][1], vsw[ch][1])]
    steps = int(math.log2(c))
    for j in range(steps):
        for ch in chains:
            for h in range(2):
                xb = x_cur[ch][h].astype(BF16)
                if j + 1 < steps:
                    out = _dot(n_pow[ch][h], jnp.concatenate([xb, n_pow[ch][h]], axis=1))
                    x_cur[ch][h] = x_cur[ch][h] + out[:, :LANES]
                    n_pow[ch][h] = out[:, LANES:].astype(BF16)
                else:
                    x_cur[ch][h] = x_cur[ch][h] + _dot(n_pow[ch][h], xb)
    for ch in chains:
        d, p = ch
        s0 = s_ref[d, p]
        x0, x1 = x_cur[ch]
        st = _dot_nt(jnp.concatenate([x0, x1, slab(d, p, "rt_true")], axis=0).astype(BF16), s0.astype(BF16))
        w0 = jnp.concatenate([keep(hi, st[:c] + x0).astype(BF16), vsw[ch][0]], axis=0)
        w1 = jnp.concatenate([keep(lo, st[c:2 * c] + x1).astype(BF16), vsw[ch][1]], axis=0)
        y_sw = st[2 * c:] + _dot(m_all[ch][0], w0) + _dot(m_all[ch][1], w1)
        y_refs[d][:, p * LANES:(p + 1) * LANES] = pltpu.roll(y_sw, HEAD, axis=1)
        bh, kh = slab(d, p, "bh"), slab(d, p, "kh")
        kb = jnp.concatenate([keep(lo, bh), keep(lo, kh), keep(hi, bh), keep(hi, kh)], axis=0).astype(BF16)
        e_tot = preps[d]["e_tot"][:, p * LANES:(p + 1) * LANES]
        s_ref[d, p] = s0 * e_tot + _dot_tn(jnp.concatenate([w0, w1], axis=0), kb)


def _wkv_kernel(*refs, c, n_c, latent):
    if latent:
        (x0_ref, x1_ref, p0_ref, n1_ref, s0_ref, mu_ref, w0_ref, a0_ref, wwa_ref, kk_ref, ka_ref, rk_ref,
         seg_ref, y0_ref, y1_ref, sf_ref, s_ref) = refs
    else:
        (x0_ref, x1_ref, p0_ref, n1_ref, mu_ref, w0_ref, a0_ref, wwa_ref, kk_ref, ka_ref, rk_ref,
         seg_ref, y0_ref, y1_ref, sf_ref, s_ref) = refs
    i = pl.program_id(1)

    @pl.when(i == 0)
    def _():
        s_ref[...] = s0_ref[0] if latent else jnp.zeros(s_ref.shape, F32)

    inner = (i > 0).astype(F32)
    preps = []
    for d, (x_ref, nb_ref, y_ref) in enumerate(((x0_ref, p0_ref, y0_ref), (x1_ref, n1_ref, y1_ref))):
        nb = nb_ref[NB_ROWS - 1:NB_ROWS, :] if d == 0 else nb_ref[0:1, :]
        preps.append(_wkv_prep(x_ref[...].astype(F32), nb.astype(F32) * inner, d, c,
                               mu_ref[d, :, :RKV_W], mu_ref[d, :, RKV_W:], w0_ref[d], a0_ref[d], wwa_ref[d],
                               kk_ref[...], ka_ref[...], rk_ref[...], seg_ref[...], y_ref, rev=(d == 1)))
    _wkv_chains(preps, c, s_ref, (y0_ref, y1_ref))

    @pl.when(i == n_c - 1)
    def _():
        sf_ref[0] = s_ref[...]


def _wkv(seg_c, row_off, b, l, c, s0, mu, w0, a0, wwa, k_k, k_a, r_k, seg1):
    t = b * l
    n_c = l // c
    t_all = seg_c.shape[0]
    cb = row_off // c
    c8 = c // NB_ROWS
    r8 = row_off // NB_ROWS
    last8 = t_all // NB_ROWS - 1
    latent = s0 is not None
    const2 = lambda bi, i: (0, 0)
    const3 = lambda bi, i: (0, 0, 0)
    in_specs = [pl.BlockSpec((c, SEG_C), lambda bi, i: (cb + bi * n_c + i, 0)),
                pl.BlockSpec((c, SEG_C), lambda bi, i: (cb + bi * n_c + n_c - 1 - i, 0)),
                pl.BlockSpec((NB_ROWS, SEG_C),
                             lambda bi, i: (jnp.maximum(r8 + (bi * n_c + i) * c8 - 1, 0), 0)),
                pl.BlockSpec((NB_ROWS, SEG_C),
                             lambda bi, i: (jnp.minimum(r8 + (bi * n_c + n_c - i) * c8, last8), 0))]
    args = [seg_c, seg_c, seg_c, seg_c]
    if latent:
        in_specs.append(pl.BlockSpec((1, 2, N_HEADS // 2, LANES, LANES), lambda bi, i: (bi, 0, 0, 0, 0)))
        args.append(s0)
    in_specs += [pl.BlockSpec(mu.shape, const3), pl.BlockSpec(w0.shape, const3), pl.BlockSpec(a0.shape, const3),
                 pl.BlockSpec(wwa.shape, const3), pl.BlockSpec(k_k.shape, const2), pl.BlockSpec(k_a.shape, const2),
                 pl.BlockSpec(r_k.shape, const2), pl.BlockSpec(seg1.shape, const2)]
    args += [mu, w0, a0, wwa, k_k, k_a, r_k, seg1]
    return pl.pallas_call(
        functools.partial(_wkv_kernel, c=c, n_c=n_c, latent=latent),
        grid=(b, n_c),
        in_specs=in_specs,
        out_specs=[pl.BlockSpec((c, 2 * HALF), lambda bi, i: (bi * n_c + i, 0)),
                   pl.BlockSpec((c, 2 * HALF), lambda bi, i: (bi * n_c + n_c - 1 - i, 0)),
                   pl.BlockSpec((1, 2, N_HEADS // 2, LANES, LANES), lambda bi, i: (bi, 0, 0, 0, 0))],
        out_shape=[jax.ShapeDtypeStruct((t, 2 * HALF), F32), jax.ShapeDtypeStruct((t, 2 * HALF), F32),
                   jax.ShapeDtypeStruct((b, 2, N_HEADS // 2, LANES, LANES), F32)],
        scratch_shapes=[pltpu.VMEM((2, N_HEADS // 2, LANES, LANES), F32)],
        compiler_params=_cparams("parallel", "arbitrary"),
        name="wkv",
    )(*args)


def _rwkv_post_kernel(y0_ref, y1_ref, gd_ref, segm_ref, g2_ref, lg_ref, lb_ref, *rest):
    o_ref = rest[-1]
    ys = y0_ref[:, :HALF] + y1_ref[:, :HALF]
    bonus = y0_ref[:, HALF:] + y1_ref[:, HALF:]
    mu = _dot(ys.astype(BF16), segm_ref[...])
    yc = ys - mu
    var = _dot((yc * yc).astype(BF16), segm_ref[...])
    gn = yc * lax.rsqrt(var + GN_EPS) * lg_ref[0] + lb_ref[0]
    gate = _dot(jax.nn.sigmoid(gd_ref[...].astype(F32)).astype(BF16), g2_ref[0])
    o_ref[...] = ((gn + bonus) * gate).astype(o_ref.dtype)


def _rwkv_post(y0, y1, seg_c, row_off, segm, g2, lnx_g, lnx_b, l, tm, prev=None):
    t = y0.shape[0]
    t_all = seg_c.shape[0]
    off = row_off // tm
    in_specs = [pl.BlockSpec((tm, 2 * HALF), lambda i: (i, 0)),
                pl.BlockSpec((tm, 2 * HALF), lambda i: (i, 0)),
                pl.BlockSpec((tm, GATE_LORA), lambda i: (off + i, (SEG_C - GATE_LORA) // GATE_LORA)),
                pl.BlockSpec((HALF, HALF), lambda i: (0, 0)),
                pl.BlockSpec((1, GATE_LORA, HALF), lambda i: (l, 0, 0)),
                pl.BlockSpec((1, 1, HALF), lambda i: (l, 0, 0)),
                pl.BlockSpec((1, 1, HALF), lambda i: (l, 0, 0))]
    args = [y0, y1, seg_c, segm, g2, lnx_g, lnx_b]
    aliases = {}
    if prev is not None:
        aliases = {len(args): 0}
        in_specs.append(pl.BlockSpec(memory_space=pl.ANY))
        args.append(prev)
    return pl.pallas_call(
        _rwkv_post_kernel,
        grid=(t // tm,),
        in_specs=in_specs,
        out_specs=pl.BlockSpec((tm, HALF), lambda i: (off + i, 0)),
        out_shape=jax.ShapeDtypeStruct((t_all, HALF), BF16),
        input_output_aliases=aliases,
        compiler_params=_cparams("parallel"),
        name="rwkv_post",
    )(*args)


def _merge_kernel(x_ref, sh_ref, sc_ref, g1_ref, oa_ref, ob_ref, oc_ref, wg_ref, wb_ref, wo_ref, lg_ref, lb_ref,
                  o_ref, *, alpha):
    x = x_ref[...]
    h = (x * (1.0 + sc_ref[0]) + sh_ref[0]).astype(BF16)
    acc = None
    for j, br in enumerate((oa_ref, ob_ref, oc_ref)):
        gate = jax.nn.sigmoid(_dot(h, wg_ref[0, :, j * D_MODEL:(j + 1) * D_MODEL]))
        term = gate * _dot(br[...], wb_ref[0, j])
        acc = term if acc is None else acc + term
    mixed = _dot(acc.astype(BF16), wo_ref[0])
    o_ref[...] = _layer_norm(alpha * x + g1_ref[0] * mixed, lg_ref[0], lb_ref[0])


def _merge(x, o_a, o_b, o_c, mod, w_g, w_branch, w_out, ln_g, ln_b, l, tm, t_ctx, l_lat, alpha):
    t = x.shape[0]
    row = functools.partial(_mod_row, tm=tm, t_ctx=t_ctx, l_lat=l_lat)
    tok = lambda w: pl.BlockSpec((tm, w), lambda i: (i, 0))
    modspec = lambda blk: pl.BlockSpec((1, 1, D_MODEL), lambda i: (row(i), 0, blk))
    return pl.pallas_call(
        functools.partial(_merge_kernel, alpha=alpha),
        grid=(t // tm,),
        in_specs=[tok(D_MODEL), modspec(0), modspec(1), modspec(2), tok(HALF), tok(HALF), tok(HALF),
                  pl.BlockSpec((1, D_MODEL, SEG_G), lambda i: (l, 0, 0)),
                  pl.BlockSpec((1, 3, HALF, D_MODEL), lambda i: (l, 0, 0, 0)),
                  pl.BlockSpec((1, D_MODEL, D_MODEL), lambda i: (l, 0, 0)),
                  pl.BlockSpec((1, 1, D_MODEL), lambda i: (l, 0, 0)),
                  pl.BlockSpec((1, 1, D_MODEL), lambda i: (l, 0, 0))],
        out_specs=tok(D_MODEL),
        out_shape=jax.ShapeDtypeStruct((t, D_MODEL), F32),
        compiler_params=_cparams("parallel"),
        name="merge",
    )(x, mod, mod, mod, o_a, o_b, o_c, w_g, w_branch, w_out, ln_g, ln_b)


def _ffn_kernel(x_ref, sh_ref, sc_ref, g2_ref, wu_ref, wd_ref, lg_ref, lb_ref, o_ref, h_ref, acc_ref, *, alpha, n_f):
    j = pl.program_id(1)

    @pl.when(j == 0)
    def _():
        h_ref[...] = (x_ref[...] * (1.0 + sc_ref[0]) + sh_ref[0]).astype(BF16)
        acc_ref[...] = jnp.zeros(acc_ref.shape, F32)

    h = h_ref[...]
    n_g = wu_ref.shape[2] // FFN_GROUP
    cols = lambda g: slice(g * FFN_GROUP, (g + 1) * FFN_GROUP)
    ups, acts, downs = {}, {}, []
    for t in range(n_g + 2):
        if t < n_g:
            ups[t] = _dot(h, wu_ref[0, :, cols(t)])
        if 0 <= t - 1 < n_g:
            u = jnp.maximum(ups.pop(t - 1), 0.0)
            acts[t - 1] = (u * u).astype(BF16)
        if 0 <= t - 2 < n_g:
            downs.append(_dot(acts.pop(t - 2), wd_ref[0, cols(t - 2), :]))
    acc_ref[...] += functools.reduce(lambda a, b: a + b, downs)

    @pl.when(j == n_f - 1)
    def _():
        o_ref[...] = _layer_norm(alpha * x_ref[...] + g2_ref[0] * acc_ref[...], lg_ref[0], lb_ref[0])


def _ffn(x, mod, w_up, w_down, ln_g, ln_b, l, tm, tf, t_ctx, l_lat, alpha):
    t = x.shape[0]
    n_f = D_FF // tf
    row = functools.partial(_mod_row, tm=tm, t_ctx=t_ctx, l_lat=l_lat)
    modspec = lambda blk: pl.BlockSpec((1, 1, D_MODEL), lambda i, j: (row(i), 0, blk))
    return pl.pallas_call(
        functools.partial(_ffn_kernel, alpha=alpha, n_f=n_f),
        grid=(t // tm, n_f),
        in_specs=[pl.BlockSpec((tm, D_MODEL), lambda i, j: (i, 0)),
                  modspec(3), modspec(4), modspec(5),
                  pl.BlockSpec((1, D_MODEL, tf), lambda i, j: (l, 0, j)),
                  pl.BlockSpec((1, tf, D_MODEL), lambda i, j: (l, j, 0)),
                  pl.BlockSpec((1, 1, D_MODEL), lambda i, j: (l, 0, 0)),
                  pl.BlockSpec((1, 1, D_MODEL), lambda i, j: (l, 0, 0))],
        out_specs=pl.BlockSpec((tm, D_MODEL), lambda i, j: (i, 0)),
        out_shape=jax.ShapeDtypeStruct((t, D_MODEL), F32),
        scratch_shapes=[pltpu.VMEM((tm, D_MODEL), BF16), pltpu.VMEM((tm, D_MODEL), F32)],
        compiler_params=_cparams("parallel", "arbitrary"),
        name="ffn",
    )(x, mod, mod, mod, w_up, w_down, ln_g, ln_b)


def _rope_tables(l):
    pos = jnp.arange(l, dtype=jnp.int32)
    row = (pos // GRID_W).astype(F32)
    col = (pos % GRID_W).astype(F32)
    half = HEAD // 2
    inv_freq = ROPE_THETA ** (-jnp.arange(0, half, 2, dtype=F32) / half)
    ang_r = row[:, None] * inv_freq[None, :]
    ang_c = col[:, None] * inv_freq[None, :]
    cos = jnp.concatenate([jnp.cos(ang_r), jnp.cos(ang_r), jnp.cos(ang_c), jnp.cos(ang_c)], axis=-1)
    sin = jnp.concatenate([-jnp.sin(ang_r), jnp.sin(ang_r), -jnp.sin(ang_c), jnp.sin(ang_c)], axis=-1)
    return jnp.tile(cos, (1, LANES // HEAD)), jnp.tile(sin, (1, LANES // HEAD))


def _head_block_matrix(width, value):
    idx = jnp.arange(width) // HEAD
    return jnp.where(idx[:, None] == idx[None, :], value, 0.0).astype(BF16)


def _key_slabs(kt):
    z = jnp.zeros_like(kt[:, :HEAD])
    slabs = []
    for g in range(N_KV):
        kg = kt[:, g * HEAD:(g + 1) * HEAD]
        slabs += [jnp.concatenate([kg, z], axis=1), jnp.concatenate([z, kg], axis=1)]
    return jnp.stack(slabs, axis=1)


def _value_slabs(v):
    ones = jnp.ones_like(v[..., :HEAD])
    return jnp.concatenate([v[..., :HEAD], ones, v[..., HEAD:], ones], axis=-1)


def _pair_states(s):
    b = s.shape[0]
    s = s.reshape(b, 2, N_HEADS // 2, 2, HEAD, HEAD)
    z = jnp.zeros_like(s[:, :, :, 0])
    top = jnp.concatenate([z, s[:, :, :, 1]], axis=-1)
    bot = jnp.concatenate([s[:, :, :, 0], z], axis=-1)
    return jnp.concatenate([top, bot], axis=-2)


def _unpair_states(sp):
    b = sp.shape[0]
    s = jnp.stack([sp[..., HEAD:, :HEAD], sp[..., :HEAD, HEAD:]], axis=3)
    return s.reshape(b, 2, N_HEADS, HEAD, HEAD)


def _pick_tile(pref, *sizes):
    return min(pref, functools.reduce(math.gcd, sizes))


def kernel(x_prompt, x_sample, cache_k, cache_v, state_wkv, c, c_ctx, w_ada, b_ada, w_in, sgu_ln_g, sgu_ln_b, sgu_w, sgu_b, q_norm, k_norm, rwkv_mu, rwkv_w0, rwkv_w2, rwkv_a0, rwkv_a2, rwkv_k_k, rwkv_k_a, rwkv_r_k, rwkv_g2, rwkv_lnx_g, rwkv_lnx_b, w_branch, w_out, ln1_g, ln1_b, w_up, w_down, ln2_g, ln2_b):
    depth = w_in.shape[0]
    b_ctx, l_ctx, _ = x_prompt.shape
    b_lat, l_lat, _ = x_sample.shape
    past = cache_k.shape[2]
    t_ctx = b_ctx * l_ctx
    t_lat = b_lat * l_lat
    alpha = (2 * depth) ** 0.25

    tm = _pick_tile(512, t_ctx, l_lat)
    tm_ffn = _pick_tile(1024, t_ctx, l_lat)
    tk = _pick_tile(512, l_ctx, l_lat)
    tq_ctx = _pick_tile(256, l_ctx)
    tq_lat = _pick_tile(512, l_lat)
    c_ctx_chunk = _pick_tile(128, l_ctx)
    c_lat_chunk = _pick_tile(128, l_lat)

    n_rows = 1 + b_lat
    pad_rows = -n_rows % 16
    cvec = jnp.concatenate([c_ctx[None, :], c, jnp.zeros((pad_rows, D_MODEL), F32)], axis=0)
    mod_all = _ada(cvec, w_ada.astype(BF16), b_ada[:, None, :])

    c_lo = SEG_A + SEG_B
    w_a_b = w_in[:, :, :SEG_A].astype(BF16)
    w_g_b = w_in[:, :, c_lo + SEG_C:].astype(BF16)
    w_bc_b = jnp.concatenate([w_in[:, :, c_lo:c_lo + SEG_C], jnp.zeros((depth, D_MODEL, B_OFF - SEG_C), F32),
                              w_in[:, :, SEG_A:c_lo]], axis=-1).astype(BF16)
    w_branch_b = w_branch.astype(BF16)
    w_out_b = w_out.astype(BF16)
    w_up_b = w_up.astype(BF16)
    w_down_b = w_down.astype(BF16)
    sgu_w_b = sgu_w.astype(BF16)
    g2_b = rwkv_g2.astype(BF16)
    seg_mean2 = _head_block_matrix(KV_W, 1.0 / HEAD)
    seg_mean8 = _head_block_matrix(HALF, 1.0 / HEAD)
    seg_ones8 = _head_block_matrix(HALF, 1.0)
    cos_ctx, sin_ctx = _rope_tables(l_ctx)
    cos_lat, sin_lat = _rope_tables(l_lat)
    zeros_lora = jnp.zeros((depth, 2, LORA, HALF), F32)
    wwa = jnp.concatenate([jnp.concatenate([rwkv_w2, zeros_lora], axis=-1),
                           jnp.concatenate([zeros_lora, rwkv_a2], axis=-1)], axis=-2).astype(BF16)

    x = jnp.concatenate([x_prompt.reshape(t_ctx, D_MODEL), x_sample.reshape(t_lat, D_MODEL)], axis=0)
    b_s_full = jnp.repeat(jnp.swapaxes(sgu_b, 1, 2), HALF // SGU_GROUPS, axis=2)
    vec = lambda p: p[:, None, :]
    new_k, new_v, new_s = [], [], []
    for l in range(depth):
        mod = mod_all[l][:, None, :]
        seg_b = _modmm(x, mod, 0, 1, w_bc_b, l, tm=tm, t_ctx=t_ctx, l_lat=l_lat, out_dtype=BF16)
        seg_c = seg_b
        o_a = _sgu(x, mod, w_a_b, vec(sgu_ln_g), vec(sgu_ln_b), sgu_w_b, b_s_full, l, tm, t_ctx, l_lat)

        qn2 = jnp.tile(q_norm[l], LANES // HEAD)[None]
        kn2 = jnp.tile(k_norm[l], KV_W // HEAD)[None]
        kn_ctx, kr_ctx, vb_ctx = _kvprep(seg_b, 0, b_ctx, l_ctx, kn2, seg_mean2, cos_ctx, sin_ctx, False, tk)
        kr_lat, vb_lat = _kvprep(seg_b, t_ctx, b_lat, l_lat, kn2, seg_mean2, cos_lat, sin_lat, True, tk)
        new_k.append(kn_ctx.reshape(b_ctx, l_ctx, N_KV, HEAD))
        new_v.append(seg_b[:t_ctx, B_OFF + HALF + KV_W:].astype(F32).reshape(b_ctx, l_ctx, N_KV, HEAD))
        kt_lat = jnp.concatenate([jnp.swapaxes(cache_k[:, l].reshape(b_lat, past, KV_W), 1, 2).astype(BF16),
                                  kr_lat], axis=2)
        v_lat = jnp.concatenate([cache_v[:, l].reshape(b_lat, past, KV_W).astype(BF16),
                                 vb_lat.reshape(b_lat, l_lat, KV_W)], axis=1)
        ob_ctx = _attention(seg_b, 0, b_ctx, l_ctx, _key_slabs(kr_ctx), _value_slabs(vb_ctx.reshape(b_ctx, l_ctx, KV_W)),
                            qn2, seg_mean2, cos_ctx, sin_ctx, False, tq_ctx)
        o_b = _attention(seg_b, t_ctx, b_lat, l_lat, _key_slabs(kt_lat), _value_slabs(v_lat),
                         qn2, seg_mean2, cos_lat, sin_lat, True, tq_lat, prev=ob_ctx)

        rw = dict(mu=rwkv_mu[l][:, None, :], w0=rwkv_w0[l][:, None, :], a0=rwkv_a0[l][:, None, :], wwa=wwa[l],
                  k_k=rwkv_k_k[l][None], k_a=rwkv_k_a[l][None], r_k=rwkv_r_k[l].reshape(1, HALF), seg1=seg_ones8)
        y0c, y1c, s_ctx = _wkv(seg_c, 0, b_ctx, l_ctx, c_ctx_chunk, None, **rw)
        y0l, y1l, _ = _wkv(seg_c, t_ctx, b_lat, l_lat, c_lat_chunk, _pair_states(state_wkv[:, l]), **rw)
        new_s.append(_unpair_states(s_ctx))
        post = functools.partial(_rwkv_post, segm=seg_mean8, g2=g2_b, lnx_g=vec(rwkv_lnx_g), lnx_b=vec(rwkv_lnx_b),
                                 l=l, tm=tm)
        o_c = post(y0l, y1l, seg_c, t_ctx, prev=post(y0c, y1c, seg_c, 0))

        x = _merge(x, o_a, o_b, o_c, mod, w_g_b, w_branch_b, w_out_b, vec(ln1_g), vec(ln1_b), l, tm, t_ctx, l_lat, alpha)
        x = _ffn(x, mod, w_up_b, w_down_b, vec(ln2_g), vec(ln2_b), l, tm_ffn, 1024, t_ctx, l_lat, alpha)

    y = x[:t_ctx].reshape(b_ctx, l_ctx, D_MODEL)
    z = x[t_ctx:].reshape(b_lat, l_lat, D_MODEL)
    return (y, z, jnp.stack(new_k, axis=1), jnp.stack(new_v, axis=1), jnp.stack(new_s, axis=1))
```

```python
import functools
import math

import jax
import jax.numpy as jnp
from jax import lax
from jax.experimental import pallas as pl
from jax.experimental.pallas import tpu as pltpu

F32 = jnp.float32
BF16 = jnp.bfloat16

D_MODEL = 1024
HALF = D_MODEL // 2
HEAD = 64
N_HEADS = HALF // HEAD
N_KV = 2
GQA = N_HEADS // N_KV
KV_W = N_KV * HEAD
GRID_W = 64
SGU_CHUNK = 128
SGU_GROUPS = 4
LORA = 64
GATE_LORA = 128
D_FF = 4 * D_MODEL
ROPE_THETA = 10000.0
GN_EPS = 64e-5
SEG_A = 2 * HALF
SEG_B = HALF + 2 * KV_W
SEG_C = 3 * HALF + 4 * LORA + GATE_LORA
SEG_G = 3 * D_MODEL
RKV_W = 3 * HALF
B_OFF = 2048
SEG_BC = B_OFF + SEG_B
FFN_GROUP = 512
LANES = 128
NB_ROWS = 16
VMEM_LIMIT = 48 * 1024 * 1024


def _cparams(*sem):
    return pltpu.CompilerParams(dimension_semantics=sem, vmem_limit_bytes=VMEM_LIMIT)


def _dot(a, b):
    return jnp.dot(a, b, preferred_element_type=F32)


def _dot_nt(a, b):
    return lax.dot_general(a, b, (((1,), (1,)), ((), ())), preferred_element_type=F32)


def _dot_tn(a, b):
    return lax.dot_general(a, b, (((0,), (0,)), ((), ())), preferred_element_type=F32)


def _layer_norm(x, g, b, eps=1e-5):
    mu = jnp.mean(x, axis=-1, keepdims=True)
    xc = x - mu
    var = jnp.mean(xc * xc, axis=-1, keepdims=True)
    return xc * lax.rsqrt(var + eps) * g + b


def _ada_kernel(c_ref, w_ref, b_ref, o_ref):
    c = c_ref[...]
    s = (c * jax.nn.sigmoid(c)).astype(BF16)
    o_ref[0] = _dot(s, w_ref[0].astype(BF16)) + b_ref[0]


def _ada(cvec, w_ada, b_ada):
    depth, _, n = w_ada.shape
    r = cvec.shape[0]
    tn = 1536
    return pl.pallas_call(
        _ada_kernel,
        grid=(depth, n // tn),
        in_specs=[pl.BlockSpec((r, D_MODEL), lambda l, j: (0, 0)),
                  pl.BlockSpec((1, D_MODEL, tn), lambda l, j: (l, 0, j)),
                  pl.BlockSpec((1, 1, tn), lambda l, j: (l, 0, j))],
        out_specs=pl.BlockSpec((1, r, tn), lambda l, j: (l, 0, j)),
        out_shape=jax.ShapeDtypeStruct((depth, r, n), F32),
        compiler_params=_cparams("parallel", "parallel"),
        name="ada",
    )(cvec, w_ada, b_ada)


def _mod_row(i, tm, t_ctx, l_lat):
    r = i * tm
    return jnp.where(r < t_ctx, 0, 1 + (r - t_ctx) // l_lat)


def _proj_kernel(x_ref, sh_ref, sc_ref, wa_ref, wbc_ref, g_ref, b_ref, ws_ref, bs_ref, oa_ref, obc_ref, *, tm):
    h = (x_ref[...] * (1.0 + sc_ref[0]) + sh_ref[0]).astype(BF16)
    n_bc = wbc_ref.shape[2]
    for j in range(2):
        cols = slice(j * (n_bc // 2), (j + 1) * (n_bc // 2))
        obc_ref[:, cols] = _dot(h, wbc_ref[0, :, cols]).astype(obc_ref.dtype)
    uv = _dot(h, wa_ref[0])
    vn = _layer_norm(uv[:, HALF:], g_ref[0], b_ref[0]).astype(BF16)
    gc = HALF // SGU_GROUPS
    for n in range(tm // SGU_CHUNK):
        rows = slice(n * SGU_CHUNK, (n + 1) * SGU_CHUNK)
        for g in range(SGU_GROUPS):
            cols = slice(g * gc, (g + 1) * gc)
            s = _dot(ws_ref[0, g], vn[rows, cols]) + bs_ref[0, :, cols]
            oa_ref[rows, cols] = (uv[rows, cols] * s).astype(oa_ref.dtype)


def _proj(x, mod, w_a, w_bc, ln_g, ln_b, w_s, b_s_full, l, tm, t_ctx, l_lat):
    t = x.shape[0]
    row = functools.partial(_mod_row, tm=tm, t_ctx=t_ctx, l_lat=l_lat)
    return pl.pallas_call(
        functools.partial(_proj_kernel, tm=tm),
        grid=(t // tm,),
        in_specs=[pl.BlockSpec((tm, D_MODEL), lambda i: (i, 0)),
                  pl.BlockSpec((1, 1, D_MODEL), lambda i: (row(i), 0, 0)),
                  pl.BlockSpec((1, 1, D_MODEL), lambda i: (row(i), 0, 1)),
                  pl.BlockSpec((1, D_MODEL, SEG_A), lambda i: (l, 0, 0)),
                  pl.BlockSpec((1, D_MODEL, SEG_BC), lambda i: (l, 0, 0)),
                  pl.BlockSpec((1, 1, HALF), lambda i: (l, 0, 0)),
                  pl.BlockSpec((1, 1, HALF), lambda i: (l, 0, 0)),
                  pl.BlockSpec((1, SGU_GROUPS, SGU_CHUNK, SGU_CHUNK), lambda i: (l, 0, 0, 0)),
                  pl.BlockSpec((1, SGU_CHUNK, HALF), lambda i: (l, 0, 0))],
        out_specs=[pl.BlockSpec((tm, HALF), lambda i: (i, 0)),
                   pl.BlockSpec((tm, SEG_BC), lambda i: (i, 0))],
        out_shape=[jax.ShapeDtypeStruct((t, HALF), BF16), jax.ShapeDtypeStruct((t, SEG_BC), BF16)],
        compiler_params=_cparams("parallel"),
        name="proj",
    )(x, mod, mod, w_a, w_bc, ln_g, ln_b, w_s, b_s_full)


def _rope_swap(x):
    lane = lax.broadcasted_iota(jnp.int32, x.shape, 1)
    up = pltpu.roll(x, LANES - 16, axis=1)
    dn = pltpu.roll(x, 16, axis=1)
    return jnp.where((lane & 16) == 0, up, dn)


def _head_rms(x, seg_ref, g):
    ms = _dot((x * x).astype(BF16), seg_ref[...])
    return x * lax.rsqrt(ms + 1e-6) * g


def _kvprep_kernel(k_ref, v_ref, g_ref, seg_ref, cos_ref, sin_ref, *out_refs, rope):
    kn = _head_rms(k_ref[...].astype(F32), seg_ref, g_ref[...])
    if rope:
        kt_ref, vb_ref = out_refs
        kn = kn * cos_ref[...] + _rope_swap(kn) * sin_ref[...]
    else:
        kn_ref, kt_ref, vb_ref = out_refs
        kn_ref[...] = kn
    kt_ref[0] = kn.T.astype(BF16)
    vb_ref[...] = v_ref[...].astype(BF16)


def _kvprep(seg_b, row_off, b, l, k_norm2, seg_mat, cos, sin, rope, tk):
    t = b * l
    off = row_off // tk
    lb = l // tk
    out_shape = [jax.ShapeDtypeStruct((b, KV_W, l), BF16), jax.ShapeDtypeStruct((t, KV_W), BF16)]
    out_specs = [pl.BlockSpec((1, KV_W, tk), lambda i: (i // lb, 0, i % lb)),
                 pl.BlockSpec((tk, KV_W), lambda i: (i, 0))]
    if not rope:
        out_shape = [jax.ShapeDtypeStruct((t, KV_W), F32)] + out_shape
        out_specs = [pl.BlockSpec((tk, KV_W), lambda i: (i, 0))] + out_specs
    return pl.pallas_call(
        functools.partial(_kvprep_kernel, rope=rope),
        grid=(t // tk,),
        in_specs=[pl.BlockSpec((tk, KV_W), lambda i: (off + i, (B_OFF + HALF) // KV_W)),
                  pl.BlockSpec((tk, KV_W), lambda i: (off + i, (B_OFF + HALF) // KV_W + 1)),
                  pl.BlockSpec((1, KV_W), lambda i: (0, 0)),
                  pl.BlockSpec((KV_W, KV_W), lambda i: (0, 0)),
                  pl.BlockSpec((tk, KV_W), lambda i: (i % lb, 0)),
                  pl.BlockSpec((tk, KV_W), lambda i: (i % lb, 0))],
        out_specs=out_specs,
        out_shape=out_shape,
        compiler_params=_cparams("parallel"),
        name="kvprep",
    )(seg_b, seg_b, k_norm2, seg_mat, cos, sin)


def _attn_kernel(q_ref, g_ref, seg_ref, cos_ref, sin_ref, kt_ref, v_ref, *rest, rope, tq):
    o_ref = rest[-1]
    qs = []
    for s in range(HALF // LANES):
        q = _head_rms(q_ref[:, s * LANES:(s + 1) * LANES].astype(F32), seg_ref, g_ref[...])
        if rope:
            q = q * cos_ref[...] + _rope_swap(q) * sin_ref[...]
        qs.append((q * (HEAD ** -0.5 * math.log2(math.e))).astype(BF16))
    lo = lax.broadcasted_iota(jnp.int32, (tq, LANES), 1) < HEAD
    scores, probs, ratios = {}, {}, {}

    def qk(h):
        scores[h] = _dot(qs[h // 2], kt_ref[0, 2 * (h // GQA) + h % 2])

    def softmax(h):
        s = scores.pop(h)
        probs[h] = jnp.exp2(s - jnp.max(s, axis=-1, keepdims=True)).astype(BF16)

    def pv(h):
        g = h // GQA
        oe = _dot(probs.pop(h), v_ref[0, :, g * LANES:(g + 1) * LANES])
        sw = pltpu.roll(oe, HEAD, axis=1)
        ratios[h] = oe / sw if h % 2 == 0 else sw / oe
        if h % 2 == 1:
            pair = jnp.where(lo, ratios.pop(h - 1), ratios.pop(h))
            o_ref[:, (h // 2) * LANES:(h // 2 + 1) * LANES] = pair.astype(o_ref.dtype)

    for t in range(N_HEADS + 2):
        if t < N_HEADS:
            qk(t)
        if 0 <= t - 1 < N_HEADS:
            softmax(t - 1)
        if 0 <= t - 2 < N_HEADS:
            pv(t - 2)


def _attention(seg_b, row_off, b, l, kt4, v_ext, q_norm2, seg_mat, cos, sin, rope, tq, prev=None):
    t_all = seg_b.shape[0]
    off = row_off // tq
    lb = l // tq
    lk = kt4.shape[-1]
    in_specs = [pl.BlockSpec((tq, HALF), lambda bi, i: (off + bi * lb + i, B_OFF // HALF)),
                pl.BlockSpec((1, LANES), lambda bi, i: (0, 0)),
                pl.BlockSpec((KV_W, KV_W), lambda bi, i: (0, 0)),
                pl.BlockSpec((tq, LANES), lambda bi, i: (i, 0)),
                pl.BlockSpec((tq, LANES), lambda bi, i: (i, 0)),
                pl.BlockSpec((1, 2 * N_KV, LANES, lk), lambda bi, i: (bi, 0, 0, 0)),
                pl.BlockSpec((1, lk, N_KV * LANES), lambda bi, i: (bi, 0, 0))]
    args = [seg_b, q_norm2, seg_mat, cos, sin, kt4, v_ext]
    aliases = {}
    if prev is not None:
        aliases = {len(args): 0}
        in_specs.append(pl.BlockSpec(memory_space=pl.ANY))
        args.append(prev)
    return pl.pallas_call(
        functools.partial(_attn_kernel, rope=rope, tq=tq),
        grid=(b, lb),
        in_specs=in_specs,
        out_specs=pl.BlockSpec((tq, HALF), lambda bi, i: (off + bi * lb + i, 0)),
        out_shape=jax.ShapeDtypeStruct((t_all, HALF), BF16),
        input_output_aliases=aliases,
        compiler_params=_cparams("parallel", "parallel"),
        name="attention",
    )(*args)


def _split3(x):
    h = x.astype(BF16)
    r1 = x - h.astype(F32)
    m = r1.astype(BF16)
    lo = (r1 - m.astype(F32)).astype(BF16)
    return h, m, lo


def _wkv_prep(x, nb_row, d, c, mu_rkv, mu_lo, w0, a0, wwa, k_k, k_a, r_k, seg1, y_ref, rev):
    rows = lax.broadcasted_iota(jnp.int32, (c, 1), 0)
    edge = (c - 1) if rev else 0

    def shifted(cur, nb):
        rolled = pltpu.roll(cur, (c - 1) if rev else 1, axis=0)
        return jnp.where(rows == edge, nb, rolled)

    rkv = x[:, :RKV_W]
    lo = x[:, RKV_W + 2 * LORA * d:RKV_W + 2 * LORA * (d + 1)]
    f = rkv + mu_rkv * (shifted(rkv, nb_row[:, :RKV_W]) - rkv)
    fl = lo + mu_lo * (shifted(lo, nb_row[:, RKV_W + 2 * LORA * d:RKV_W + 2 * LORA * (d + 1)]) - lo)
    r = f[:, :HALF]
    k = f[:, HALF:2 * HALF]
    v = f[:, 2 * HALF:]
    lane = lax.broadcasted_iota(jnp.int32, fl.shape, 1)
    lin = _dot(jnp.where(lane < LORA, jnp.tanh(fl), fl).astype(BF16), wwa)
    lw = (-math.exp(-0.5)) * jax.nn.sigmoid(w0 + lin[:, :HALF])
    asig = jax.nn.sigmoid(a0 + lin[:, HALF:])
    kk = k * k_k
    ss = _dot((kk * kk).astype(BF16), seg1)
    kkn = kk / jnp.maximum(jnp.sqrt(ss), 1e-12)
    kmod = k * (1.0 + (asig - 1.0) * k_a)
    bonus = _dot((r * kmod * r_k).astype(BF16), seg1) * v
    y_ref[:, HALF:] = bonus

    ti = lax.broadcasted_iota(jnp.int32, (c, c), 0)
    si = lax.broadcasted_iota(jnp.int32, (c, c), 1)
    incl = (si >= ti) if rev else (si <= ti)
    strict = (si > ti) if rev else (si < ti)
    tri = incl.astype(BF16)
    h3, m3, l3 = _split3(lw)
    cum = _dot(tri, h3) + _dot(tri, m3) + _dot(tri, l3)
    ref = cum[c // 2:c // 2 + 1]
    end = 0 if rev else c - 1
    cum_end = cum[end:end + 1]
    g = cum - ref
    e_pos = jnp.exp(g)
    e_neg = jnp.exp(-g)
    e_ref = jnp.exp(ref)
    e_tot = jnp.exp(cum_end)
    e_end = jnp.exp(cum_end - ref)
    at_c = -kkn * jnp.exp(g - lw)
    rt_c = r * e_pos
    bt = kkn * asig * e_neg
    kt = kmod * e_neg
    return dict(at_c=at_c, rt_c=rt_c, bt=bt, kt=kt, at_true=at_c * e_ref, rt_true=rt_c * e_ref,
                bh=bt * e_end, kh=kt * e_end, v=v, e_tot=e_tot, incl=incl, strict=strict)


def _wkv_chains(preps, c, s_ref, y_refs):
    lane = lax.broadcasted_iota(jnp.int32, (c, LANES), 1)
    lo = lane < HEAD
    hi = jnp.logical_not(lo)
    chains = [(d, p) for d in range(2) for p in range(N_HEADS // 2)]

    def slab(d, p, name):
        return preps[d][name][:, p * LANES:(p + 1) * LANES]

    def keep(mask, x):
        return jnp.where(mask, x, 0.0)

    n_pow, a_ak, m_all, vsw, x_cur = {}, {}, {}, {}, {}
    for ch in chains:
        d, p = ch
        at_c, rt_c = slab(d, p, "at_c"), slab(d, p, "rt_c")
        lhs = jnp.concatenate([keep(lo, at_c), keep(hi, at_c), keep(lo, rt_c), keep(hi, rt_c)], axis=0)
        rhs = jnp.concatenate([slab(d, p, "bt"), slab(d, p, "kt")], axis=0)
        g = _dot_nt(lhs.astype(BF16), rhs.astype(BF16))
        strict, incl = preps[d]["strict"], preps[d]["incl"]
        incl2 = jnp.concatenate([incl, incl], axis=1)
        n_pow[ch] = [keep(strict, g[h * c:(h + 1) * c, :c]).astype(BF16) for h in range(2)]
        a_ak[ch] = [keep(strict, g[h * c:(h + 1) * c, c:]).astype(BF16) for h in range(2)]
        m_all[ch] = [keep(incl2, g[(2 + h) * c:(3 + h) * c, :]).astype(BF16) for h in range(2)]
        v_sw = pltpu.roll(slab(d, p, "v"), HEAD, axis=1)
        vsw[ch] = [keep(hi, v_sw).astype(BF16), keep(lo, v_sw).astype(BF16)]
    for ch in chains:
        d, p = ch
        at_true = slab(d, p, "at_true")
        x_cur[ch] = [keep(lo, at_true) + _dot(a_ak[ch][0], vsw[ch][0]),
                     keep(hi, at_true) + _dot(a_ak[ch][1], vsw[ch][1])]
    steps = int(math.log2(c))
    for j in range(steps):
        for ch in chains:
            for h in range(2):
                xb = x_cur[ch][h].astype(BF16)
                if j + 1 < steps:
                    out = _dot(n_pow[ch][h], jnp.concatenate([xb, n_pow[ch][h]], axis=1))
                    x_cur[ch][h] = x_cur[ch][h] + out[:, :LANES]
                    n_pow[ch][h] = out[:, LANES:].astype(BF16)
                else:
                    x_cur[ch][h] = x_cur[ch][h] + _dot(n_pow[ch][h], xb)
    for ch in chains:
        d, p = ch
        s0 = s_ref[d, p]
        x0, x1 = x_cur[ch]
        st = _dot_nt(jnp.concatenate([x0, x1, slab(d, p, "rt_true")], axis=0).astype(BF16), s0.astype(BF16))
        w0 = jnp.concatenate([keep(hi, st[:c] + x0).astype(BF16), vsw[ch][0]], axis=0)
        w1 = jnp.concatenate([keep(lo, st[c:2 * c] + x1).astype(BF16), vsw[ch][1]], axis=0)
        y_sw = st[2 * c:] + _dot(m_all[ch][0], w0) + _dot(m_all[ch][1], w1)
        y_refs[d][:, p * LANES:(p + 1) * LANES] = pltpu.roll(y_sw, HEAD, axis=1)
        bh, kh = slab(d, p, "bh"), slab(d, p, "kh")
        kb = jnp.concatenate([keep(lo, bh), keep(lo, kh), keep(hi, bh), keep(hi, kh)], axis=0).astype(BF16)
        e_tot = preps[d]["e_tot"][:, p * LANES:(p + 1) * LANES]
        s_ref[d, p] = s0 * e_tot + _dot_tn(jnp.concatenate([w0, w1], axis=0), kb)


def _wkv_kernel(*refs, c, n_c, latent):
    if latent:
        (x0_ref, x1_ref, p0_ref, n1_ref, s0_ref, mu_ref, w0_ref, a0_ref, wwa_ref, kk_ref, ka_ref, rk_ref,
         seg_ref, y0_ref, y1_ref, sf_ref, s_ref) = refs
    else:
        (x0_ref, x1_ref, p0_ref, n1_ref, mu_ref, w0_ref, a0_ref, wwa_ref, kk_ref, ka_ref, rk_ref,
         seg_ref, y0_ref, y1_ref, sf_ref, s_ref) = refs
    i = pl.program_id(1)

    @pl.when(i == 0)
    def _():
        s_ref[...] = s0_ref[0] if latent else jnp.zeros(s_ref.shape, F32)

    inner = (i > 0).astype(F32)
    preps = []
    for d, (x_ref, nb_ref, y_ref) in enumerate(((x0_ref, p0_ref, y0_ref), (x1_ref, n1_ref, y1_ref))):
        nb = nb_ref[NB_ROWS - 1:NB_ROWS, :] if d == 0 else nb_ref[0:1, :]
        preps.append(_wkv_prep(x_ref[...].astype(F32), nb.astype(F32) * inner, d, c,
                               mu_ref[d, :, :RKV_W], mu_ref[d, :, RKV_W:], w0_ref[d], a0_ref[d], wwa_ref[d],
                               kk_ref[...], ka_ref[...], rk_ref[...], seg_ref[...], y_ref, rev=(d == 1)))
    _wkv_chains(preps, c, s_ref, (y0_ref, y1_ref))

    @pl.when(i == n_c - 1)
    def _():
        sf_ref[0] = s_ref[...]


def _wkv(seg_c, row_off, b, l, c, s0, mu, w0, a0, wwa, k_k, k_a, r_k, seg1):
    t = b * l
    n_c = l // c
    t_all = seg_c.shape[0]
    cb = row_off // c
    c8 = c // NB_ROWS
    r8 = row_off // NB_ROWS
    last8 = t_all // NB_ROWS - 1
    latent = s0 is not None
    const2 = lambda bi, i: (0, 0)
    const3 = lambda bi, i: (0, 0, 0)
    in_specs = [pl.BlockSpec((c, SEG_C), lambda bi, i: (cb + bi * n_c + i, 0)),
                pl.BlockSpec((c, SEG_C), lambda bi, i: (cb + bi * n_c + n_c - 1 - i, 0)),
                pl.BlockSpec((NB_ROWS, SEG_C),
                             lambda bi, i: (jnp.maximum(r8 + (bi * n_c + i) * c8 - 1, 0), 0)),
                pl.BlockSpec((NB_ROWS, SEG_C),
                             lambda bi, i: (jnp.minimum(r8 + (bi * n_c + n_c - i) * c8, last8), 0))]
    args = [seg_c, seg_c, seg_c, seg_c]
    if latent:
        in_specs.append(pl.BlockSpec((1, 2, N_HEADS // 2, LANES, LANES), lambda bi, i: (bi, 0, 0, 0, 0)))
        args.append(s0)
    in_specs += [pl.BlockSpec(mu.shape, const3), pl.BlockSpec(w0.shape, const3), pl.BlockSpec(a0.shape, const3),
                 pl.BlockSpec(wwa.shape, const3), pl.BlockSpec(k_k.shape, const2), pl.BlockSpec(k_a.shape, const2),
                 pl.BlockSpec(r_k.shape, const2), pl.BlockSpec(seg1.shape, const2)]
    args += [mu, w0, a0, wwa, k_k, k_a, r_k, seg1]
    return pl.pallas_call(
        functools.partial(_wkv_kernel, c=c, n_c=n_c, latent=latent),
        grid=(b, n_c),
        in_specs=in_specs,
        out_specs=[pl.BlockSpec((c, 2 * HALF), lambda bi, i: (bi * n_c + i, 0)),
                   pl.BlockSpec((c, 2 * HALF), lambda bi, i: (bi * n_c + n_c - 1 - i, 0)),
                   pl.BlockSpec((1, 2, N_HEADS // 2, LANES, LANES), lambda bi, i: (bi, 0, 0, 0, 0))],
        out_shape=[jax.ShapeDtypeStruct((t, 2 * HALF), F32), jax.ShapeDtypeStruct((t, 2 * HALF), F32),
                   jax.ShapeDtypeStruct((b, 2, N_HEADS // 2, LANES, LANES), F32)],
        scratch_shapes=[pltpu.VMEM((2, N_HEADS // 2, LANES, LANES), F32)],
        compiler_params=_cparams("parallel", "arbitrary"),
        name="wkv",
    )(*args)


def _rwkv_post_kernel(y0_ref, y1_ref, gd_ref, segm_ref, g2_ref, lg_ref, lb_ref, *rest):
    o_ref = rest[-1]
    ys = y0_ref[:, :HALF] + y1_ref[:, :HALF]
    bonus = y0_ref[:, HALF:] + y1_ref[:, HALF:]
    mu = _dot(ys.astype(BF16), segm_ref[...])
    yc = ys - mu
    var = _dot((yc * yc).astype(BF16), segm_ref[...])
    gn = yc * lax.rsqrt(var + GN_EPS) * lg_ref[0] + lb_ref[0]
    gate = _dot(jax.nn.sigmoid(gd_ref[...].astype(F32)).astype(BF16), g2_ref[0])
    o_ref[...] = ((gn + bonus) * gate).astype(o_ref.dtype)


def _rwkv_post(y0, y1, seg_c, row_off, segm, g2, lnx_g, lnx_b, l, tm, prev=None):
    t = y0.shape[0]
    t_all = seg_c.shape[0]
    off = row_off // tm
    in_specs = [pl.BlockSpec((tm, 2 * HALF), lambda i: (i, 0)),
                pl.BlockSpec((tm, 2 * HALF), lambda i: (i, 0)),
                pl.BlockSpec((tm, GATE_LORA), lambda i: (off + i, (SEG_C - GATE_LORA) // GATE_LORA)),
                pl.BlockSpec((HALF, HALF), lambda i: (0, 0)),
                pl.BlockSpec((1, GATE_LORA, HALF), lambda i: (l, 0, 0)),
                pl.BlockSpec((1, 1, HALF), lambda i: (l, 0, 0)),
                pl.BlockSpec((1, 1, HALF), lambda i: (l, 0, 0))]
    args = [y0, y1, seg_c, segm, g2, lnx_g, lnx_b]
    aliases = {}
    if prev is not None:
        aliases = {len(args): 0}
        in_specs.append(pl.BlockSpec(memory_space=pl.ANY))
        args.append(prev)
    return pl.pallas_call(
        _rwkv_post_kernel,
        grid=(t // tm,),
        in_specs=in_specs,
        out_specs=pl.BlockSpec((tm, HALF), lambda i: (off + i, 0)),
        out_shape=jax.ShapeDtypeStruct((t_all, HALF), BF16),
        input_output_aliases=aliases,
        compiler_params=_cparams("parallel"),
        name="rwkv_post",
    )(*args)


def _merge_kernel(x_ref, sh_ref, sc_ref, g1_ref, oa_ref, ob_ref, oc_ref, wg_ref, wb_ref, wo_ref, lg_ref, lb_ref,
                  o_ref, *, alpha):
    x = x_ref[...]
    h = (x * (1.0 + sc_ref[0]) + sh_ref[0]).astype(BF16)
    acc = None
    for j, br in enumerate((oa_ref, ob_ref, oc_ref)):
        gate = jax.nn.sigmoid(_dot(h, wg_ref[0, :, j * D_MODEL:(j + 1) * D_MODEL]))
        term = gate * _dot(br[...], wb_ref[0, j])
        acc = term if acc is None else acc + term
    mixed = _dot(acc.astype(BF16), wo_ref[0])
    o_ref[...] = _layer_norm(alpha * x + g1_ref[0] * mixed, lg_ref[0], lb_ref[0])


def _merge(x, o_a, o_b, o_c, mod, w_g, w_branch, w_out, ln_g, ln_b, l, tm, t_ctx, l_lat, alpha):
    t = x.shape[0]
    row = functools.partial(_mod_row, tm=tm, t_ctx=t_ctx, l_lat=l_lat)
    tok = lambda w: pl.BlockSpec((tm, w), lambda i: (i, 0))
    modspec = lambda blk: pl.BlockSpec((1, 1, D_MODEL), lambda i: (row(i), 0, blk))
    return pl.pallas_call(
        functools.partial(_merge_kernel, alpha=alpha),
        grid=(t // tm,),
        in_specs=[tok(D_MODEL), modspec(0), modspec(1), modspec(2), tok(HALF), tok(HALF), tok(HALF),
                  pl.BlockSpec((1, D_MODEL, SEG_G), lambda i: (l, 0, 0)),
                  pl.BlockSpec((1, 3, HALF, D_MODEL), lambda i: (l, 0, 0, 0)),
                  pl.BlockSpec((1, D_MODEL, D_MODEL), lambda i: (l, 0, 0)),
                  pl.BlockSpec((1, 1, D_MODEL), lambda i: (l, 0, 0)),
                  pl.BlockSpec((1, 1, D_MODEL), lambda i: (l, 0, 0))],
        out_specs=tok(D_MODEL),
        out_shape=jax.ShapeDtypeStruct((t, D_MODEL), F32),
        compiler_params=_cparams("parallel"),
        name="merge",
    )(x, mod, mod, mod, o_a, o_b, o_c, w_g, w_branch, w_out, ln_g, ln_b)


def _ffn_kernel(x_ref, sh_ref, sc_ref, g2_ref, wu_ref, wd_ref, lg_ref, lb_ref, o_ref, h_ref, acc_ref, *, alpha, n_f):
    j = pl.program_id(1)

    @pl.when(j == 0)
    def _():
        h_ref[...] = (x_ref[...] * (1.0 + sc_ref[0]) + sh_ref[0]).astype(BF16)
        acc_ref[...] = jnp.zeros(acc_ref.shape, F32)

    h = h_ref[...]
    n_g = wu_ref.shape[2] // FFN_GROUP
    cols = lambda g: slice(g * FFN_GROUP, (g + 1) * FFN_GROUP)
    ups, acts, downs = {}, {}, []
    for t in range(n_g + 2):
        if t < n_g:
            ups[t] = _dot(h, wu_ref[0, :, cols(t)])
        if 0 <= t - 1 < n_g:
            u = jnp.maximum(ups.pop(t - 1), 0.0)
            acts[t - 1] = (u * u).astype(BF16)
        if 0 <= t - 2 < n_g:
            downs.append(_dot(acts.pop(t - 2), wd_ref[0, cols(t - 2), :]))
    acc_ref[...] += functools.reduce(lambda a, b: a + b, downs)

    @pl.when(j == n_f - 1)
    def _():
        o_ref[...] = _layer_norm(alpha * x_ref[...] + g2_ref[0] * acc_ref[...], lg_ref[0], lb_ref[0])


def _ffn(x, mod, w_up, w_down, ln_g, ln_b, l, tm, tf, t_ctx, l_lat, alpha):
    t = x.shape[0]
    n_f = D_FF // tf
    row = functools.partial(_mod_row, tm=tm, t_ctx=t_ctx, l_lat=l_lat)
    modspec = lambda blk: pl.BlockSpec((1, 1, D_MODEL), lambda i, j: (row(i), 0, blk))
    return pl.pallas_call(
        functools.partial(_ffn_kernel, alpha=alpha, n_f=n_f),
        grid=(t // tm, n_f),
        in_specs=[pl.BlockSpec((tm, D_MODEL), lambda i, j: (i, 0)),
                  modspec(3), modspec(4), modspec(5),
                  pl.BlockSpec((1, D_MODEL, tf), lambda i, j: (l, 0, j)),
                  pl.BlockSpec((1, tf, D_MODEL), lambda i, j: (l, j, 0)),
                  pl.BlockSpec((1, 1, D_MODEL), lambda i, j: (l, 0, 0)),
                  pl.BlockSpec((1, 1, D_MODEL), lambda i, j: (l, 0, 0))],
        out_specs=pl.BlockSpec((tm, D_MODEL), lambda i, j: (i, 0)),
        out_shape=jax.ShapeDtypeStruct((t, D_MODEL), F32),
        scratch_shapes=[pltpu.VMEM((tm, D_MODEL), BF16), pltpu.VMEM((tm, D_MODEL), F32)],
        compiler_params=_cparams("parallel", "arbitrary"),
        name="ffn",
    )(x, mod, mod, mod, w_up, w_down, ln_g, ln_b)


def _rope_tables(l):
    pos = jnp.arange(l, dtype=jnp.int32)
    row = (pos // GRID_W).astype(F32)
    col = (pos % GRID_W).astype(F32)
    half = HEAD // 2
    inv_freq = ROPE_THETA ** (-jnp.arange(0, half, 2, dtype=F32) / half)
    ang_r = row[:, None] * inv_freq[None, :]
    ang_c = col[:, None] * inv_freq[None, :]
    cos = jnp.concatenate([jnp.cos(ang_r), jnp.cos(ang_r), jnp.cos(ang_c), jnp.cos(ang_c)], axis=-1)
    sin = jnp.concatenate([-jnp.sin(ang_r), jnp.sin(ang_r), -jnp.sin(ang_c), jnp.sin(ang_c)], axis=-1)
    return jnp.tile(cos, (1, LANES // HEAD)), jnp.tile(sin, (1, LANES // HEAD))


def _head_block_matrix(width, value):
    idx = jnp.arange(width) // HEAD
    return jnp.where(idx[:, None] == idx[None, :], value, 0.0).astype(BF16)


def _key_slabs(kt):
    z = jnp.zeros_like(kt[:, :HEAD])
    slabs = []
    for g in range(N_KV):
        kg = kt[:, g * HEAD:(g + 1) * HEAD]
        slabs += [jnp.concatenate([kg, z], axis=1), jnp.concatenate([z, kg], axis=1)]
    return jnp.stack(slabs, axis=1)


def _value_slabs(v):
    ones = jnp.ones_like(v[..., :HEAD])
    return jnp.concatenate([v[..., :HEAD], ones, v[..., HEAD:], ones], axis=-1)


def _pair_states(s):
    b = s.shape[0]
    s = s.reshape(b, 2, N_HEADS // 2, 2, HEAD, HEAD)
    z = jnp.zeros_like(s[:, :, :, 0])
    top = jnp.concatenate([z, s[:, :, :, 1]], axis=-1)
    bot = jnp.concatenate([s[:, :, :, 0], z], axis=-1)
    return jnp.concatenate([top, bot], axis=-2)


def _unpair_states(sp):
    b = sp.shape[0]
    s = jnp.stack([sp[..., HEAD:, :HEAD], sp[..., :HEAD, HEAD:]], axis=3)
    return s.reshape(b, 2, N_HEADS, HEAD, HEAD)


def _pick_tile(pref, *sizes):
    return min(pref, functools.reduce(math.gcd, sizes))


def kernel(x_prompt, x_sample, cache_k, cache_v, state_wkv, c, c_ctx, w_ada, b_ada, w_in, sgu_ln_g, sgu_ln_b, sgu_w, sgu_b, q_norm, k_norm, rwkv_mu, rwkv_w0, rwkv_w2, rwkv_a0, rwkv_a2, rwkv_k_k, rwkv_k_a, rwkv_r_k, rwkv_g2, rwkv_lnx_g, rwkv_lnx_b, w_branch, w_out, ln1_g, ln1_b, w_up, w_down, ln2_g, ln2_b):
    depth = w_in.shape[0]
    b_ctx, l_ctx, _ = x_prompt.shape
    b_lat, l_lat, _ = x_sample.shape
    past = cache_k.shape[2]
    t_ctx = b_ctx * l_ctx
    t_lat = b_lat * l_lat
    alpha = (2 * depth) ** 0.25

    tm = _pick_tile(512, t_ctx, l_lat)
    tm_ffn = _pick_tile(1024, t_ctx, l_lat)
    tk = _pick_tile(512, l_ctx, l_lat)
    tq_ctx = _pick_tile(256, l_ctx)
    tq_lat = _pick_tile(512, l_lat)
    c_ctx_chunk = _pick_tile(128, l_ctx)
    c_lat_chunk = _pick_tile(128, l_lat)

    n_rows = 1 + b_lat
    pad_rows = -n_rows % 16
    cvec = jnp.concatenate([c_ctx[None, :], c, jnp.zeros((pad_rows, D_MODEL), F32)], axis=0)
    mod_all = _ada(cvec, w_ada, b_ada[:, None, :])

    c_lo = SEG_A + SEG_B
    w_a_b = w_in[:, :, :SEG_A].astype(BF16)
    w_g_b = w_in[:, :, c_lo + SEG_C:].astype(BF16)
    w_bc_b = jnp.concatenate([w_in[:, :, c_lo:c_lo + SEG_C], jnp.zeros((depth, D_MODEL, B_OFF - SEG_C), F32),
                              w_in[:, :, SEG_A:c_lo]], axis=-1).astype(BF16)
    w_branch_b = w_branch.astype(BF16)
    w_out_b = w_out.astype(BF16)
    w_up_b = w_up.astype(BF16)
    w_down_b = w_down.astype(BF16)
    sgu_w_b = sgu_w.astype(BF16)
    g2_b = rwkv_g2.astype(BF16)
    seg_mean2 = _head_block_matrix(KV_W, 1.0 / HEAD)
    seg_mean8 = _head_block_matrix(HALF, 1.0 / HEAD)
    seg_ones8 = _head_block_matrix(HALF, 1.0)
    cos_ctx, sin_ctx = _rope_tables(l_ctx)
    cos_lat, sin_lat = _rope_tables(l_lat)
    zeros_lora = jnp.zeros((depth, 2, LORA, HALF), F32)
    wwa = jnp.concatenate([jnp.concatenate([rwkv_w2, zeros_lora], axis=-1),
                           jnp.concatenate([zeros_lora, rwkv_a2], axis=-1)], axis=-2).astype(BF16)

    x = jnp.concatenate([x_prompt.reshape(t_ctx, D_MODEL), x_sample.reshape(t_lat, D_MODEL)], axis=0)
    b_s_full = jnp.repeat(jnp.swapaxes(sgu_b, 1, 2), HALF // SGU_GROUPS, axis=2)
    vec = lambda p: p[:, None, :]
    new_k, new_v, new_s = [], [], []
    for l in range(depth):
        mod = mod_all[l][:, None, :]
        o_a, seg_b = _proj(x, mod, w_a_b, w_bc_b, vec(sgu_ln_g), vec(sgu_ln_b), sgu_w_b, b_s_full, l, tm, t_ctx, l_lat)
        seg_c = seg_b

        qn2 = jnp.tile(q_norm[l], LANES // HEAD)[None]
        kn2 = jnp.tile(k_norm[l], KV_W // HEAD)[None]
        kn_ctx, kr_ctx, vb_ctx = _kvprep(seg_b, 0, b_ctx, l_ctx, kn2, seg_mean2, cos_ctx, sin_ctx, False, tk)
        kr_lat, vb_lat = _kvprep(seg_b, t_ctx, b_lat, l_lat, kn2, seg_mean2, cos_lat, sin_lat, True, tk)
        new_k.append(kn_ctx.reshape(b_ctx, l_ctx, N_KV, HEAD))
        new_v.append(seg_b[:t_ctx, B_OFF + HALF + KV_W:].astype(F32).reshape(b_ctx, l_ctx, N_KV, HEAD))
        kt_lat = jnp.concatenate([jnp.swapaxes(cache_k[:, l].reshape(b_lat, past, KV_W), 1, 2).astype(BF16),
                                  kr_lat], axis=2)
        v_lat = jnp.concatenate([cache_v[:, l].reshape(b_lat, past, KV_W).astype(BF16),
                                 vb_lat.reshape(b_lat, l_lat, KV_W)], axis=1)
        ob_ctx = _attention(seg_b, 0, b_ctx, l_ctx, _key_slabs(kr_ctx), _value_slabs(vb_ctx.reshape(b_ctx, l_ctx, KV_W)),
                            qn2, seg_mean2, cos_ctx, sin_ctx, False, tq_ctx)
        o_b = _attention(seg_b, t_ctx, b_lat, l_lat, _key_slabs(kt_lat), _value_slabs(v_lat),
                         qn2, seg_mean2, cos_lat, sin_lat, True, tq_lat, prev=ob_ctx)

        rw = dict(mu=rwkv_mu[l][:, None, :], w0=rwkv_w0[l][:, None, :], a0=rwkv_a0[l][:, None, :], wwa=wwa[l],
                  k_k=rwkv_k_k[l][None], k_a=rwkv_k_a[l][None], r_k=rwkv_r_k[l].reshape(1, HALF), seg1=seg_ones8)
        y0c, y1c, s_ctx = _wkv(seg_c, 0, b_ctx, l_ctx, c_ctx_chunk, None, **rw)
        y0l, y1l, _ = _wkv(seg_c, t_ctx, b_lat, l_lat, c_lat_chunk, _pair_states(state_wkv[:, l]), **rw)
        new_s.append(_unpair_states(s_ctx))
        post = functools.partial(_rwkv_post, segm=seg_mean8, g2=g2_b, lnx_g=vec(rwkv_lnx_g), lnx_b=vec(rwkv_lnx_b),
                                 l=l, tm=tm)
        o_c = post(y0l, y1l, seg_c, t_ctx, prev=post(y0c, y1c, seg_c, 0))

        x = _merge(x, o_a, o_b, o_c, mod, w_g_b, w_branch_b, w_out_b, vec(ln1_g), vec(ln1_b), l, tm, t_ctx, l_lat, alpha)
        x = _ffn(x, mod, w_up_b, w_down_b, vec(ln2_g), vec(ln2_b), l, tm_ffn, 1024, t_ctx, l_lat, alpha)

    y = x[:t_ctx].reshape(b_ctx, l_ctx, D_MODEL)
    z = x[t_ctx:].reshape(b_lat, l_lat, D_MODEL)
    return (y, z, jnp.stack(new_k, axis=1), jnp.stack(new_v, axis=1), jnp.stack(new_s, axis=1))
```

```python
import functools
import math

import jax
import jax.numpy as jnp
from jax import lax
from jax.experimental import pallas as pl
from jax.experimental.pallas import tpu as pltpu

F32 = jnp.float32
BF16 = jnp.bfloat16

D_MODEL = 1024
HALF = D_MODEL // 2
HEAD = 64
N_HEADS = HALF // HEAD
N_KV = 2
GQA = N_HEADS // N_KV
KV_W = N_KV * HEAD
GRID_W = 64
SGU_CHUNK = 128
SGU_GROUPS = 4
LORA = 64
GATE_LORA = 128
D_FF = 4 * D_MODEL
ROPE_THETA = 10000.0
GN_EPS = 64e-5
SEG_A = 2 * HALF
SEG_B = HALF + 2 * KV_W
SEG_C = 3 * HALF + 4 * LORA + GATE_LORA
SEG_G = 3 * D_MODEL
RKV_W = 3 * HALF
B_OFF = 2048
SEG_BC = B_OFF + SEG_B
FFN_GROUP = 512
LANES = 128
NB_ROWS = 16
VMEM_LIMIT = 48 * 1024 * 1024


def _cparams(*sem):
    return pltpu.CompilerParams(dimension_semantics=sem, vmem_limit_bytes=VMEM_LIMIT)


def _dot(a, b):
    return jnp.dot(a, b, preferred_element_type=F32)


def _dot_nt(a, b):
    return lax.dot_general(a, b, (((1,), (1,)), ((), ())), preferred_element_type=F32)


def _dot_tn(a, b):
    return lax.dot_general(a, b, (((0,), (0,)), ((), ())), preferred_element_type=F32)


def _layer_norm(x, g, b, eps=1e-5):
    mu = jnp.mean(x, axis=-1, keepdims=True)
    xc = x - mu
    var = jnp.mean(xc * xc, axis=-1, keepdims=True)
    return xc * lax.rsqrt(var + eps) * g + b


def _ada_kernel(c_ref, w_ref, b_ref, o_ref):
    c = c_ref[...]
    s = (c * jax.nn.sigmoid(c)).astype(BF16)
    o_ref[0] = _dot(s, w_ref[0].astype(BF16)) + b_ref[0]


def _ada(cvec, w_ada, b_ada):
    depth, _, n = w_ada.shape
    r = cvec.shape[0]
    tn = 1536
    return pl.pallas_call(
        _ada_kernel,
        grid=(depth, n // tn),
        in_specs=[pl.BlockSpec((r, D_MODEL), lambda l, j: (0, 0)),
                  pl.BlockSpec((1, D_MODEL, tn), lambda l, j: (l, 0, j)),
                  pl.BlockSpec((1, 1, tn), lambda l, j: (l, 0, j))],
        out_specs=pl.BlockSpec((1, r, tn), lambda l, j: (l, 0, j)),
        out_shape=jax.ShapeDtypeStruct((depth, r, n), F32),
        compiler_params=_cparams("parallel", "parallel"),
        name="ada",
    )(cvec, w_ada, b_ada)


def _token_tile(x_refs, n_first):
    if len(x_refs) == 1:
        return x_refs[0][...]
    return jnp.where(pl.program_id(0) < n_first, x_refs[0][...], x_refs[1][...])


def _token_specs(xs, tm, n_first, index=lambda i, *_: i):
    if len(xs) == 1:
        return [pl.BlockSpec((tm, D_MODEL), lambda *g: (index(*g), 0))]
    return [pl.BlockSpec((tm, D_MODEL), lambda *g: (jnp.minimum(index(*g), n_first - 1), 0)),
            pl.BlockSpec((tm, D_MODEL), lambda *g: (jnp.maximum(index(*g) - n_first, 0), 0))]


def _mod_row(i, tm, t_ctx, l_lat):
    r = i * tm
    return jnp.where(r < t_ctx, 0, 1 + (r - t_ctx) // l_lat)


def _proj_kernel(*refs, tm, n_x, n_first):
    sh_ref, sc_ref, wa_ref, wbc_ref, g_ref, b_ref, ws_ref, bs_ref, oa_ref, obc_ref = refs[n_x:]
    h = (_token_tile(refs[:n_x], n_first) * (1.0 + sc_ref[0]) + sh_ref[0]).astype(BF16)
    n_bc = wbc_ref.shape[2]
    for j in range(2):
        cols = slice(j * (n_bc // 2), (j + 1) * (n_bc // 2))
        obc_ref[:, cols] = _dot(h, wbc_ref[0, :, cols]).astype(obc_ref.dtype)
    uv = _dot(h, wa_ref[0])
    vn = _layer_norm(uv[:, HALF:], g_ref[0], b_ref[0]).astype(BF16)
    gc = HALF // SGU_GROUPS
    for n in range(tm // SGU_CHUNK):
        rows = slice(n * SGU_CHUNK, (n + 1) * SGU_CHUNK)
        for g in range(SGU_GROUPS):
            cols = slice(g * gc, (g + 1) * gc)
            s = _dot(ws_ref[0, g], vn[rows, cols]) + bs_ref[0, :, cols]
            oa_ref[rows, cols] = (uv[rows, cols] * s).astype(oa_ref.dtype)


def _proj(xs, mod, w_a, w_bc, ln_g, ln_b, w_s, b_s_full, l, tm, t_ctx, l_lat):
    t = sum(x.shape[0] for x in xs)
    n_first = t_ctx // tm
    row = functools.partial(_mod_row, tm=tm, t_ctx=t_ctx, l_lat=l_lat)
    return pl.pallas_call(
        functools.partial(_proj_kernel, tm=tm, n_x=len(xs), n_first=n_first),
        grid=(t // tm,),
        in_specs=_token_specs(xs, tm, n_first) + [
                  pl.BlockSpec((1, 1, D_MODEL), lambda i: (row(i), 0, 0)),
                  pl.BlockSpec((1, 1, D_MODEL), lambda i: (row(i), 0, 1)),
                  pl.BlockSpec((1, D_MODEL, SEG_A), lambda i: (l, 0, 0)),
                  pl.BlockSpec((1, D_MODEL, SEG_BC), lambda i: (l, 0, 0)),
                  pl.BlockSpec((1, 1, HALF), lambda i: (l, 0, 0)),
                  pl.BlockSpec((1, 1, HALF), lambda i: (l, 0, 0)),
                  pl.BlockSpec((1, SGU_GROUPS, SGU_CHUNK, SGU_CHUNK), lambda i: (l, 0, 0, 0)),
                  pl.BlockSpec((1, SGU_CHUNK, HALF), lambda i: (l, 0, 0))],
        out_specs=[pl.BlockSpec((tm, HALF), lambda i: (i, 0)),
                   pl.BlockSpec((tm, SEG_BC), lambda i: (i, 0))],
        out_shape=[jax.ShapeDtypeStruct((t, HALF), BF16), jax.ShapeDtypeStruct((t, SEG_BC), BF16)],
        compiler_params=_cparams("parallel"),
        name="proj",
    )(*xs, mod, mod, w_a, w_bc, ln_g, ln_b, w_s, b_s_full)


def _rope_swap(x):
    lane = lax.broadcasted_iota(jnp.int32, x.shape, 1)
    up = pltpu.roll(x, LANES - 16, axis=1)
    dn = pltpu.roll(x, 16, axis=1)
    return jnp.where((lane & 16) == 0, up, dn)


def _head_rms(x, seg_ref, g):
    ms = _dot((x * x).astype(BF16), seg_ref[...])
    return x * lax.rsqrt(ms + 1e-6) * g


def _kvprep_kernel(k_ref, v_ref, g_ref, seg_ref, cos_ref, sin_ref, *out_refs, rope):
    kn = _head_rms(k_ref[...].astype(F32), seg_ref, g_ref[...])
    if rope:
        kt_ref, vb_ref = out_refs
        kn = kn * cos_ref[...] + _rope_swap(kn) * sin_ref[...]
    else:
        kn_ref, kt_ref, vb_ref = out_refs
        kn_ref[...] = kn
    kt_ref[0] = kn.T.astype(BF16)
    vb_ref[...] = v_ref[...].astype(BF16)


def _kvprep(seg_b, row_off, b, l, k_norm2, seg_mat, cos, sin, rope, tk):
    t = b * l
    off = row_off // tk
    lb = l // tk
    out_shape = [jax.ShapeDtypeStruct((b, KV_W, l), BF16), jax.ShapeDtypeStruct((t, KV_W), BF16)]
    out_specs = [pl.BlockSpec((1, KV_W, tk), lambda i: (i // lb, 0, i % lb)),
                 pl.BlockSpec((tk, KV_W), lambda i: (i, 0))]
    if not rope:
        out_shape = [jax.ShapeDtypeStruct((t, KV_W), F32)] + out_shape
        out_specs = [pl.BlockSpec((tk, KV_W), lambda i: (i, 0))] + out_specs
    return pl.pallas_call(
        functools.partial(_kvprep_kernel, rope=rope),
        grid=(t // tk,),
        in_specs=[pl.BlockSpec((tk, KV_W), lambda i: (off + i, (B_OFF + HALF) // KV_W)),
                  pl.BlockSpec((tk, KV_W), lambda i: (off + i, (B_OFF + HALF) // KV_W + 1)),
                  pl.BlockSpec((1, KV_W), lambda i: (0, 0)),
                  pl.BlockSpec((KV_W, KV_W), lambda i: (0, 0)),
                  pl.BlockSpec((tk, KV_W), lambda i: (i % lb, 0)),
                  pl.BlockSpec((tk, KV_W), lambda i: (i % lb, 0))],
        out_specs=out_specs,
        out_shape=out_shape,
        compiler_params=_cparams("parallel"),
        name="kvprep",
    )(seg_b, seg_b, k_norm2, seg_mat, cos, sin)


def _attn_kernel(q_ref, g_ref, seg_ref, cos_ref, sin_ref, kt_ref, v_ref, *rest, rope, tq):
    o_ref = rest[-1]
    qs = []
    for s in range(HALF // LANES):
        q = _head_rms(q_ref[:, s * LANES:(s + 1) * LANES].astype(F32), seg_ref, g_ref[...])
        if rope:
            q = q * cos_ref[...] + _rope_swap(q) * sin_ref[...]
        qs.append((q * (HEAD ** -0.5 * math.log2(math.e))).astype(BF16))
    lo = lax.broadcasted_iota(jnp.int32, (tq, LANES), 1) < HEAD
    scores, probs, ratios = {}, {}, {}

    kt = kt_ref[0]
    zeros = jnp.zeros((HEAD, kt.shape[1]), BF16)
    k_slab = {}
    for g in range(N_KV):
        kg = kt[g * HEAD:(g + 1) * HEAD]
        k_slab[g, 0] = jnp.concatenate([kg, zeros], axis=0)
        k_slab[g, 1] = jnp.concatenate([zeros, kg], axis=0)
    lane_row = lax.broadcasted_iota(jnp.int32, (1, LANES), 1)
    own = [(lane_row < HEAD).astype(F32).astype(BF16), (lane_row >= HEAD).astype(F32).astype(BF16)]
    v_slab = [v_ref[0] * own[g] + own[1 - g] for g in range(N_KV)]

    def qk(h):
        scores[h] = _dot(qs[h // 2], k_slab[h // GQA, h % 2])

    def softmax(h):
        s = scores.pop(h)
        probs[h] = jnp.exp2(s - jnp.max(s, axis=-1, keepdims=True)).astype(BF16)

    def pv(h):
        g = h // GQA
        oe = _dot(probs.pop(h), v_slab[g])
        sw = pltpu.roll(oe, HEAD, axis=1)
        ratios[h] = oe / sw if (h % 2 == 0) == (g == 0) else sw / oe
        if h % 2 == 1:
            pair = jnp.where(lo, ratios.pop(h - 1), ratios.pop(h))
            o_ref[:, (h // 2) * LANES:(h // 2 + 1) * LANES] = pair.astype(o_ref.dtype)

    for t in range(N_HEADS + 2):
        if t < N_HEADS:
            qk(t)
        if 0 <= t - 1 < N_HEADS:
            softmax(t - 1)
        if 0 <= t - 2 < N_HEADS:
            pv(t - 2)


def _attention(seg_b, row_off, b, l, kt, v, q_norm2, seg_mat, cos, sin, rope, tq, prev=None):
    t_all = seg_b.shape[0]
    off = row_off // tq
    lb = l // tq
    lk = kt.shape[-1]
    in_specs = [pl.BlockSpec((tq, HALF), lambda bi, i: (off + bi * lb + i, B_OFF // HALF)),
                pl.BlockSpec((1, LANES), lambda bi, i: (0, 0)),
                pl.BlockSpec((KV_W, KV_W), lambda bi, i: (0, 0)),
                pl.BlockSpec((tq, LANES), lambda bi, i: (i, 0)),
                pl.BlockSpec((tq, LANES), lambda bi, i: (i, 0)),
                pl.BlockSpec((1, KV_W, lk), lambda bi, i: (bi, 0, 0)),
                pl.BlockSpec((1, lk, KV_W), lambda bi, i: (bi, 0, 0))]
    args = [seg_b, q_norm2, seg_mat, cos, sin, kt, v]
    aliases = {}
    if prev is not None:
        aliases = {len(args): 0}
        in_specs.append(pl.BlockSpec(memory_space=pl.ANY))
        args.append(prev)
    return pl.pallas_call(
        functools.partial(_attn_kernel, rope=rope, tq=tq),
        grid=(b, lb),
        in_specs=in_specs,
        out_specs=pl.BlockSpec((tq, HALF), lambda bi, i: (off + bi * lb + i, 0)),
        out_shape=jax.ShapeDtypeStruct((t_all, HALF), BF16),
        input_output_aliases=aliases,
        compiler_params=_cparams("parallel", "parallel"),
        name="attention",
    )(*args)


def _split3(x):
    h = x.astype(BF16)
    r1 = x - h.astype(F32)
    m = r1.astype(BF16)
    lo = (r1 - m.astype(F32)).astype(BF16)
    return h, m, lo


def _wkv_prep(x, nb_row, d, c, mu_rkv, mu_lo, w0, a0, wwa, k_k, k_a, r_k, seg1, y_ref, rev):
    rows = lax.broadcasted_iota(jnp.int32, (c, 1), 0)
    edge = (c - 1) if rev else 0

    def shifted(cur, nb):
        rolled = pltpu.roll(cur, (c - 1) if rev else 1, axis=0)
        return jnp.where(rows == edge, nb, rolled)

    rkv = x[:, :RKV_W]
    lo = x[:, RKV_W + 2 * LORA * d:RKV_W + 2 * LORA * (d + 1)]
    f = rkv + mu_rkv * (shifted(rkv, nb_row[:, :RKV_W]) - rkv)
    fl = lo + mu_lo * (shifted(lo, nb_row[:, RKV_W + 2 * LORA * d:RKV_W + 2 * LORA * (d + 1)]) - lo)
    r = f[:, :HALF]
    k = f[:, HALF:2 * HALF]
    v = f[:, 2 * HALF:]
    lane = lax.broadcasted_iota(jnp.int32, fl.shape, 1)
    lin = _dot(jnp.where(lane < LORA, jnp.tanh(fl), fl).astype(BF16), wwa)
    lw = (-math.exp(-0.5)) * jax.nn.sigmoid(w0 + lin[:, :HALF])
    asig = jax.nn.sigmoid(a0 + lin[:, HALF:])
    kk = k * k_k
    ss = _dot((kk * kk).astype(BF16), seg1)
    kkn = kk / jnp.maximum(jnp.sqrt(ss), 1e-12)
    kmod = k * (1.0 + (asig - 1.0) * k_a)
    bonus = _dot((r * kmod * r_k).astype(BF16), seg1) * v
    y_ref[:, HALF:] = bonus

    ti = lax.broadcasted_iota(jnp.int32, (c, c), 0)
    si = lax.broadcasted_iota(jnp.int32, (c, c), 1)
    incl = (si >= ti) if rev else (si <= ti)
    strict = (si > ti) if rev else (si < ti)
    tri = incl.astype(BF16)
    h3, m3, l3 = _split3(lw)
    cum = _dot(tri, h3) + _dot(tri, m3) + _dot(tri, l3)
    ref = cum[c // 2:c // 2 + 1]
    end = 0 if rev else c - 1
    cum_end = cum[end:end + 1]
    g = cum - ref
    e_pos = jnp.exp(g)
    e_neg = jnp.exp(-g)
    e_ref = jnp.exp(ref)
    e_tot = jnp.exp(cum_end)
    e_end = jnp.exp(cum_end - ref)
    at_c = -kkn * jnp.exp(g - lw)
    rt_c = r * e_pos
    bt = kkn * asig * e_neg
    kt = kmod * e_neg
    return dict(at_c=at_c, rt_c=rt_c, bt=bt, kt=kt, at_true=at_c * e_ref, rt_true=rt_c * e_ref,
                bh=bt * e_end, kh=kt * e_end, v=v, e_tot=e_tot, incl=incl, strict=strict)


def _wkv_chains(preps, c, s_ref, y_refs):
    lane = lax.broadcasted_iota(jnp.int32, (c, LANES), 1)
    lo = lane < HEAD
    hi = jnp.logical_not(lo)
    chains = [(d, p) for d in range(2) for p in range(N_HEADS // 2)]

    def slab(d, p, name):
        return preps[d][name][:, p * LANES:(p + 1) * LANES]

    def keep(mask, x):
        return jnp.where(mask, x, 0.0)

    n_pow, a_ak, m_all, vsw, x_cur = {}, {}, {}, {}, {}
    for ch in chains:
        d, p = ch
        at_c, rt_c = slab(d, p, "at_c"), slab(d, p, "rt_c")
        lhs = jnp.concatenate([keep(lo, at_c), keep(hi, at_c), keep(lo, rt_c), keep(hi, rt_c)], axis=0)
        rhs = jnp.concatenate([slab(d, p, "bt"), slab(d, p, "kt")], axis=0)
        g = _dot_nt(lhs.astype(BF16), rhs.astype(BF16))
        strict, incl = preps[d]["strict"], preps[d]["incl"]
        incl2 = jnp.concatenate([incl, incl], axis=1)
        n_pow[ch] = [keep(strict, g[h * c:(h + 1) * c, :c]).astype(BF16) for h in range(2)]
        a_ak[ch] = [keep(strict, g[h * c:(h + 1) * c, c:]).astype(BF16) for h in range(2)]
        m_all[ch] = [keep(incl2, g[(2 + h) * c:(3 + h) * c, :]).astype(BF16) for h in range(2)]
        v_sw = pltpu.roll(slab(d, p, "v"), HEAD, axis=1)
        vsw[ch] = [keep(hi, v_sw).astype(BF16), keep(lo, v_sw).astype(BF16)]
    for ch in chains:
        d, p = ch
        at_true = slab(d, p, "at_true")
        x_cur[ch] = [keep(lo, at_true) + _dot(a_ak[ch][0], vsw[ch][0]),
                     keep(hi, at_true) + _dot(a_ak[ch][1], vsw[ch][1])]
    steps = int(math.log2(c))
    for j in range(steps):
        for ch in chains:
            for h in range(2):
                xb = x_cur[ch][h].astype(BF16)
                if j + 1 < steps:
                    out = _dot(n_pow[ch][h], jnp.concatenate([xb, n_pow[ch][h]], axis=1))
                    x_cur[ch][h] = x_cur[ch][h] + out[:, :LANES]
                    n_pow[ch][h] = out[:, LANES:].astype(BF16)
                else:
                    x_cur[ch][h] = x_cur[ch][h] + _dot(n_pow[ch][h], xb)
    for ch in chains:
        d, p = ch
        s0 = s_ref[d, p]
        x0, x1 = x_cur[ch]
        st = _dot_nt(jnp.concatenate([x0, x1, slab(d, p, "rt_true")], axis=0).astype(BF16), s0.astype(BF16))
        w0 = jnp.concatenate([keep(hi, st[:c] + x0).astype(BF16), vsw[ch][0]], axis=0)
        w1 = jnp.concatenate([keep(lo, st[c:2 * c] + x1).astype(BF16), vsw[ch][1]], axis=0)
        y_sw = st[2 * c:] + _dot(m_all[ch][0], w0) + _dot(m_all[ch][1], w1)
        y_refs[d][:, p * LANES:(p + 1) * LANES] = pltpu.roll(y_sw, HEAD, axis=1)
        bh, kh = slab(d, p, "bh"), slab(d, p, "kh")
        kb = jnp.concatenate([keep(lo, bh), keep(lo, kh), keep(hi, bh), keep(hi, kh)], axis=0).astype(BF16)
        e_tot = preps[d]["e_tot"][:, p * LANES:(p + 1) * LANES]
        s_ref[d, p] = s0 * e_tot + _dot_tn(jnp.concatenate([w0, w1], axis=0), kb)


def _wkv_kernel(*refs, c, n_c, latent):
    if latent:
        (x0_ref, x1_ref, p0_ref, n1_ref, s0_ref, mu_ref, w0_ref, a0_ref, wwa_ref, kk_ref, ka_ref, rk_ref,
         seg_ref, y0_ref, y1_ref, sf_ref, s_ref) = refs
    else:
        (x0_ref, x1_ref, p0_ref, n1_ref, mu_ref, w0_ref, a0_ref, wwa_ref, kk_ref, ka_ref, rk_ref,
         seg_ref, y0_ref, y1_ref, sf_ref, s_ref) = refs
    i = pl.program_id(1)

    @pl.when(i == 0)
    def _():
        s_ref[...] = s0_ref[0] if latent else jnp.zeros(s_ref.shape, F32)

    inner = (i > 0).astype(F32)
    preps = []
    for d, (x_ref, nb_ref, y_ref) in enumerate(((x0_ref, p0_ref, y0_ref), (x1_ref, n1_ref, y1_ref))):
        nb = nb_ref[NB_ROWS - 1:NB_ROWS, :] if d == 0 else nb_ref[0:1, :]
        preps.append(_wkv_prep(x_ref[...].astype(F32), nb.astype(F32) * inner, d, c,
                               mu_ref[0, d, :, :RKV_W], mu_ref[0, d, :, RKV_W:], w0_ref[0, d], a0_ref[0, d],
                               wwa_ref[0, d], kk_ref[0], ka_ref[0], rk_ref[0], seg_ref[...], y_ref, rev=(d == 1)))
    _wkv_chains(preps, c, s_ref, (y0_ref, y1_ref))

    @pl.when(i == n_c - 1)
    def _():
        sf_ref[0] = s_ref[...]


def _wkv(seg_c, row_off, b, l, c, s0, layer, mu, w0, a0, wwa, k_k, k_a, r_k, seg1):
    t = b * l
    n_c = l // c
    t_all = seg_c.shape[0]
    cb = row_off // c
    c8 = c // NB_ROWS
    r8 = row_off // NB_ROWS
    last8 = t_all // NB_ROWS - 1
    latent = s0 is not None
    const2 = lambda bi, i: (0, 0)
    in_specs = [pl.BlockSpec((c, SEG_C), lambda bi, i: (cb + bi * n_c + i, 0)),
                pl.BlockSpec((c, SEG_C), lambda bi, i: (cb + bi * n_c + n_c - 1 - i, 0)),
                pl.BlockSpec((NB_ROWS, SEG_C),
                             lambda bi, i: (jnp.maximum(r8 + (bi * n_c + i) * c8 - 1, 0), 0)),
                pl.BlockSpec((NB_ROWS, SEG_C),
                             lambda bi, i: (jnp.minimum(r8 + (bi * n_c + n_c - i) * c8, last8), 0))]
    args = [seg_c, seg_c, seg_c, seg_c]
    if latent:
        in_specs.append(pl.BlockSpec((1, 2, N_HEADS // 2, LANES, LANES), lambda bi, i: (bi, 0, 0, 0, 0)))
        args.append(s0)
    per_layer = lambda p: pl.BlockSpec((1,) + p.shape[1:], lambda bi, i: (layer,) + (0,) * (p.ndim - 1))
    in_specs += [per_layer(p) for p in (mu, w0, a0, wwa, k_k, k_a, r_k)] + [pl.BlockSpec(seg1.shape, const2)]
    args += [mu, w0, a0, wwa, k_k, k_a, r_k, seg1]
    return pl.pallas_call(
        functools.partial(_wkv_kernel, c=c, n_c=n_c, latent=latent),
        grid=(b, n_c),
        in_specs=in_specs,
        out_specs=[pl.BlockSpec((c, 2 * HALF), lambda bi, i: (bi * n_c + i, 0)),
                   pl.BlockSpec((c, 2 * HALF), lambda bi, i: (bi * n_c + n_c - 1 - i, 0)),
                   pl.BlockSpec((1, 2, N_HEADS // 2, LANES, LANES), lambda bi, i: (bi, 0, 0, 0, 0))],
        out_shape=[jax.ShapeDtypeStruct((t, 2 * HALF), F32), jax.ShapeDtypeStruct((t, 2 * HALF), F32),
                   jax.ShapeDtypeStruct((b, 2, N_HEADS // 2, LANES, LANES), F32)],
        scratch_shapes=[pltpu.VMEM((2, N_HEADS // 2, LANES, LANES), F32)],
        compiler_params=_cparams("parallel", "arbitrary"),
        name="wkv",
    )(*args)


def _rwkv_post_kernel(y0_ref, y1_ref, gd_ref, segm_ref, g2_ref, lg_ref, lb_ref, *rest):
    o_ref = rest[-1]
    ys = y0_ref[:, :HALF] + y1_ref[:, :HALF]
    bonus = y0_ref[:, HALF:] + y1_ref[:, HALF:]
    mu = _dot(ys.astype(BF16), segm_ref[...])
    yc = ys - mu
    var = _dot((yc * yc).astype(BF16), segm_ref[...])
    gn = yc * lax.rsqrt(var + GN_EPS) * lg_ref[0] + lb_ref[0]
    gate = _dot(jax.nn.sigmoid(gd_ref[...].astype(F32)).astype(BF16), g2_ref[0])
    o_ref[...] = ((gn + bonus) * gate).astype(o_ref.dtype)


def _rwkv_post(y0, y1, seg_c, row_off, segm, g2, lnx_g, lnx_b, l, tm, prev=None):
    t = y0.shape[0]
    t_all = seg_c.shape[0]
    off = row_off // tm
    in_specs = [pl.BlockSpec((tm, 2 * HALF), lambda i: (i, 0)),
                pl.BlockSpec((tm, 2 * HALF), lambda i: (i, 0)),
                pl.BlockSpec((tm, GATE_LORA), lambda i: (off + i, (SEG_C - GATE_LORA) // GATE_LORA)),
                pl.BlockSpec((HALF, HALF), lambda i: (0, 0)),
                pl.BlockSpec((1, GATE_LORA, HALF), lambda i: (l, 0, 0)),
                pl.BlockSpec((1, 1, HALF), lambda i: (l, 0, 0)),
                pl.BlockSpec((1, 1, HALF), lambda i: (l, 0, 0))]
    args = [y0, y1, seg_c, segm, g2, lnx_g, lnx_b]
    aliases = {}
    if prev is not None:
        aliases = {len(args): 0}
        in_specs.append(pl.BlockSpec(memory_space=pl.ANY))
        args.append(prev)
    return pl.pallas_call(
        _rwkv_post_kernel,
        grid=(t // tm,),
        in_specs=in_specs,
        out_specs=pl.BlockSpec((tm, HALF), lambda i: (off + i, 0)),
        out_shape=jax.ShapeDtypeStruct((t_all, HALF), BF16),
        input_output_aliases=aliases,
        compiler_params=_cparams("parallel"),
        name="rwkv_post",
    )(*args)


def _merge_kernel(*refs, alpha, n_x, n_first):
    sh_ref, sc_ref, g1_ref, oa_ref, ob_ref, oc_ref, wg_ref, wb_ref, wo_ref, lg_ref, lb_ref, o_ref = refs[n_x:]
    x = _token_tile(refs[:n_x], n_first)
    h = (x * (1.0 + sc_ref[0]) + sh_ref[0]).astype(BF16)
    acc = None
    for j, br in enumerate((oa_ref, ob_ref, oc_ref)):
        gate = jax.nn.sigmoid(_dot(h, wg_ref[0, :, j * D_MODEL:(j + 1) * D_MODEL]))
        term = gate * _dot(br[...], wb_ref[0, j])
        acc = term if acc is None else acc + term
    mixed = _dot(acc.astype(BF16), wo_ref[0])
    o_ref[...] = _layer_norm(alpha * x + g1_ref[0] * mixed, lg_ref[0], lb_ref[0])


def _merge(xs, o_a, o_b, o_c, mod, w_g, w_branch, w_out, ln_g, ln_b, l, tm, t_ctx, l_lat, alpha):
    t = sum(x.shape[0] for x in xs)
    n_first = t_ctx // tm
    row = functools.partial(_mod_row, tm=tm, t_ctx=t_ctx, l_lat=l_lat)
    tok = lambda w: pl.BlockSpec((tm, w), lambda i: (i, 0))
    modspec = lambda blk: pl.BlockSpec((1, 1, D_MODEL), lambda i: (row(i), 0, blk))
    return pl.pallas_call(
        functools.partial(_merge_kernel, alpha=alpha, n_x=len(xs), n_first=n_first),
        grid=(t // tm,),
        in_specs=_token_specs(xs, tm, n_first) + [
                  modspec(0), modspec(1), modspec(2), tok(HALF), tok(HALF), tok(HALF),
                  pl.BlockSpec((1, D_MODEL, SEG_G), lambda i: (l, 0, 0)),
                  pl.BlockSpec((1, 3, HALF, D_MODEL), lambda i: (l, 0, 0, 0)),
                  pl.BlockSpec((1, D_MODEL, D_MODEL), lambda i: (l, 0, 0)),
                  pl.BlockSpec((1, 1, D_MODEL), lambda i: (l, 0, 0)),
                  pl.BlockSpec((1, 1, D_MODEL), lambda i: (l, 0, 0))],
        out_specs=tok(D_MODEL),
        out_shape=jax.ShapeDtypeStruct((t, D_MODEL), F32),
        compiler_params=_cparams("parallel"),
        name="merge",
    )(*xs, mod, mod, mod, o_a, o_b, o_c, w_g, w_branch, w_out, ln_g, ln_b)


def _ffn_kernel(x_ref, sh_ref, sc_ref, g2_ref, wu_ref, wd_ref, lg_ref, lb_ref, *rest, alpha, n_f, n_first):
    *o_refs, h_ref, acc_ref = rest
    i = pl.program_id(0)
    j = pl.program_id(1)

    @pl.when(j == 0)
    def _():
        h_ref[...] = (x_ref[...] * (1.0 + sc_ref[0]) + sh_ref[0]).astype(BF16)
        acc_ref[...] = jnp.zeros(acc_ref.shape, F32)

    h = h_ref[...]
    n_g = wu_ref.shape[2] // FFN_GROUP
    cols = lambda g: slice(g * FFN_GROUP, (g + 1) * FFN_GROUP)
    ups, acts, downs = {}, {}, []
    for t in range(n_g + 2):
        if t < n_g:
            ups[t] = _dot(h, wu_ref[0, :, cols(t)])
        if 0 <= t - 1 < n_g:
            u = jnp.maximum(ups.pop(t - 1), 0.0)
            acts[t - 1] = (u * u).astype(BF16)
        if 0 <= t - 2 < n_g:
            downs.append(_dot(acts.pop(t - 2), wd_ref[0, cols(t - 2), :]))
    acc_ref[...] += functools.reduce(lambda a, b: a + b, downs)

    def finish(o_ref):
        o_ref[...] = _layer_norm(alpha * x_ref[...] + g2_ref[0] * acc_ref[...], lg_ref[0], lb_ref[0])

    last = j == n_f - 1
    if len(o_refs) == 1:
        pl.when(last)(lambda: finish(o_refs[0]))
    else:
        pl.when(jnp.logical_and(last, i < n_first))(lambda: finish(o_refs[0]))
        pl.when(jnp.logical_and(last, i >= n_first))(lambda: finish(o_refs[1]))


def _ffn(x, mod, w_up, w_down, ln_g, ln_b, l, tm, tf, t_ctx, l_lat, alpha, split=False):
    t = x.shape[0]
    n_f = D_FF // tf
    n_first = t_ctx // tm
    if split:
        out_specs = [pl.BlockSpec((tm, D_MODEL), lambda i, j: (jnp.minimum(i, n_first - 1), 0)),
                     pl.BlockSpec((tm, D_MODEL), lambda i, j: (jnp.maximum(i - n_first, 0), 0))]
        out_shape = [jax.ShapeDtypeStruct((t_ctx, D_MODEL), F32), jax.ShapeDtypeStruct((t - t_ctx, D_MODEL), F32)]
    else:
        out_specs = pl.BlockSpec((tm, D_MODEL), lambda i, j: (i, 0))
        out_shape = jax.ShapeDtypeStruct((t, D_MODEL), F32)
    row = functools.partial(_mod_row, tm=tm, t_ctx=t_ctx, l_lat=l_lat)
    modspec = lambda blk: pl.BlockSpec((1, 1, D_MODEL), lambda i, j: (row(i), 0, blk))
    return pl.pallas_call(
        functools.partial(_ffn_kernel, alpha=alpha, n_f=n_f, n_first=n_first),
        grid=(t // tm, n_f),
        in_specs=[pl.BlockSpec((tm, D_MODEL), lambda i, j: (i, 0)),
                  modspec(3), modspec(4), modspec(5),
                  pl.BlockSpec((1, D_MODEL, tf), lambda i, j: (l, 0, j)),
                  pl.BlockSpec((1, tf, D_MODEL), lambda i, j: (l, j, 0)),
                  pl.BlockSpec((1, 1, D_MODEL), lambda i, j: (l, 0, 0)),
                  pl.BlockSpec((1, 1, D_MODEL), lambda i, j: (l, 0, 0))],
        out_specs=out_specs,
        out_shape=out_shape,
        scratch_shapes=[pltpu.VMEM((tm, D_MODEL), BF16), pltpu.VMEM((tm, D_MODEL), F32)],
        compiler_params=_cparams("arbitrary", "arbitrary"),
        name="ffn",
    )(x, mod, mod, mod, w_up, w_down, ln_g, ln_b)


def _rope_tables(l):
    pos = jnp.arange(l, dtype=jnp.int32)
    row = (pos // GRID_W).astype(F32)
    col = (pos % GRID_W).astype(F32)
    half = HEAD // 2
    inv_freq = ROPE_THETA ** (-jnp.arange(0, half, 2, dtype=F32) / half)
    ang_r = row[:, None] * inv_freq[None, :]
    ang_c = col[:, None] * inv_freq[None, :]
    cos = jnp.concatenate([jnp.cos(ang_r), jnp.cos(ang_r), jnp.cos(ang_c), jnp.cos(ang_c)], axis=-1)
    sin = jnp.concatenate([-jnp.sin(ang_r), jnp.sin(ang_r), -jnp.sin(ang_c), jnp.sin(ang_c)], axis=-1)
    return jnp.tile(cos, (1, LANES // HEAD)), jnp.tile(sin, (1, LANES // HEAD))


def _head_block_matrix(width, value):
    idx = jnp.arange(width) // HEAD
    return jnp.where(idx[:, None] == idx[None, :], value, 0.0).astype(BF16)


def _pair_states(s):
    b = s.shape[0]
    s = s.reshape(b, 2, N_HEADS // 2, 2, HEAD, HEAD)
    z = jnp.zeros_like(s[:, :, :, 0])
    top = jnp.concatenate([z, s[:, :, :, 1]], axis=-1)
    bot = jnp.concatenate([s[:, :, :, 0], z], axis=-1)
    return jnp.concatenate([top, bot], axis=-2)


def _unpair_states(sp):
    b = sp.shape[0]
    s = jnp.stack([sp[..., HEAD:, :HEAD], sp[..., :HEAD, HEAD:]], axis=3)
    return s.reshape(b, 2, N_HEADS, HEAD, HEAD)


def _pick_tile(pref, *sizes):
    return min(pref, functools.reduce(math.gcd, sizes))


def kernel(x_prompt, x_sample, cache_k, cache_v, state_wkv, c, c_ctx, w_ada, b_ada, w_in, sgu_ln_g, sgu_ln_b, sgu_w, sgu_b, q_norm, k_norm, rwkv_mu, rwkv_w0, rwkv_w2, rwkv_a0, rwkv_a2, rwkv_k_k, rwkv_k_a, rwkv_r_k, rwkv_g2, rwkv_lnx_g, rwkv_lnx_b, w_branch, w_out, ln1_g, ln1_b, w_up, w_down, ln2_g, ln2_b):
    depth = w_in.shape[0]
    b_ctx, l_ctx, _ = x_prompt.shape
    b_lat, l_lat, _ = x_sample.shape
    past = cache_k.shape[2]
    t_ctx = b_ctx * l_ctx
    t_lat = b_lat * l_lat
    alpha = (2 * depth) ** 0.25

    tm = _pick_tile(512, t_ctx, l_lat)
    tm_ffn = _pick_tile(1024, t_ctx, l_lat)
    tk = _pick_tile(512, l_ctx, l_lat)
    tq_ctx = _pick_tile(256, l_ctx)
    tq_lat = _pick_tile(512, l_lat)
    c_ctx_chunk = _pick_tile(128, l_ctx)
    c_lat_chunk = _pick_tile(128, l_lat)

    n_rows = 1 + b_lat
    pad_rows = -n_rows % 16
    cvec = jnp.concatenate([c_ctx[None, :], c, jnp.zeros((pad_rows, D_MODEL), F32)], axis=0)
    mod_all = _ada(cvec, w_ada, b_ada[:, None, :])

    c_lo = SEG_A + SEG_B
    w_a_b = w_in[:, :, :SEG_A].astype(BF16)
    w_g_b = w_in[:, :, c_lo + SEG_C:].astype(BF16)
    w_bc_b = jnp.concatenate([w_in[:, :, c_lo:c_lo + SEG_C], jnp.zeros((depth, D_MODEL, B_OFF - SEG_C), F32),
                              w_in[:, :, SEG_A:c_lo]], axis=-1).astype(BF16)
    w_branch_b = w_branch.astype(BF16)
    w_out_b = w_out.astype(BF16)
    w_up_b = w_up.astype(BF16)
    w_down_b = w_down.astype(BF16)
    sgu_w_b = sgu_w.astype(BF16)
    g2_b = rwkv_g2.astype(BF16)
    seg_mean2 = _head_block_matrix(KV_W, 1.0 / HEAD)
    seg_mean8 = _head_block_matrix(HALF, 1.0 / HEAD)
    seg_ones8 = _head_block_matrix(HALF, 1.0)
    cos_ctx, sin_ctx = _rope_tables(l_ctx)
    cos_lat, sin_lat = _rope_tables(l_lat)
    zeros_lora = jnp.zeros((depth, 2, LORA, HALF), F32)
    wwa = jnp.concatenate([jnp.concatenate([rwkv_w2, zeros_lora], axis=-1),
                           jnp.concatenate([zeros_lora, rwkv_a2], axis=-1)], axis=-2).astype(BF16)

    xs = (x_prompt.reshape(t_ctx, D_MODEL), x_sample.reshape(t_lat, D_MODEL))
    b_s_full = jnp.repeat(jnp.swapaxes(sgu_b, 1, 2), HALF // SGU_GROUPS, axis=2)
    vec = lambda p: p[:, None, :]
    new_k, new_v, new_s = [], [], []
    for l in range(depth):
        mod = mod_all[l][:, None, :]
        o_a, seg_b = _proj(xs, mod, w_a_b, w_bc_b, vec(sgu_ln_g), vec(sgu_ln_b), sgu_w_b, b_s_full, l, tm, t_ctx, l_lat)
        seg_c = seg_b

        qn2 = jnp.tile(q_norm[l], LANES // HEAD)[None]
        kn2 = jnp.tile(k_norm[l], KV_W // HEAD)[None]
        kn_ctx, kr_ctx, vb_ctx = _kvprep(seg_b, 0, b_ctx, l_ctx, kn2, seg_mean2, cos_ctx, sin_ctx, False, tk)
        kr_lat, vb_lat = _kvprep(seg_b, t_ctx, b_lat, l_lat, kn2, seg_mean2, cos_lat, sin_lat, True, tk)
        new_k.append(kn_ctx.reshape(b_ctx, l_ctx, N_KV, HEAD))
        new_v.append(seg_b[:t_ctx, B_OFF + HALF + KV_W:].astype(F32).reshape(b_ctx, l_ctx, N_KV, HEAD))
        kt_lat = jnp.concatenate([jnp.swapaxes(cache_k[:, l].reshape(b_lat, past, KV_W), 1, 2).astype(BF16),
                                  kr_lat], axis=2)
        v_lat = jnp.concatenate([cache_v[:, l].reshape(b_lat, past, KV_W).astype(BF16),
                                 vb_lat.reshape(b_lat, l_lat, KV_W)], axis=1)
        ob_ctx = _attention(seg_b, 0, b_ctx, l_ctx, kr_ctx, vb_ctx.reshape(b_ctx, l_ctx, KV_W),
                            qn2, seg_mean2, cos_ctx, sin_ctx, False, tq_ctx)
        o_b = _attention(seg_b, t_ctx, b_lat, l_lat, kt_lat, v_lat,
                         qn2, seg_mean2, cos_lat, sin_lat, True, tq_lat, prev=ob_ctx)

        rw = dict(layer=l, mu=rwkv_mu[:, :, None, :], w0=rwkv_w0[:, :, None, :], a0=rwkv_a0[:, :, None, :], wwa=wwa,
                  k_k=vec(rwkv_k_k), k_a=vec(rwkv_k_a), r_k=rwkv_r_k.reshape(depth, 1, HALF), seg1=seg_ones8)
        y0c, y1c, s_ctx = _wkv(seg_c, 0, b_ctx, l_ctx, c_ctx_chunk, None, **rw)
        y0l, y1l, _ = _wkv(seg_c, t_ctx, b_lat, l_lat, c_lat_chunk, _pair_states(state_wkv[:, l]), **rw)
        new_s.append(_unpair_states(s_ctx))
        post = functools.partial(_rwkv_post, segm=seg_mean8, g2=g2_b, lnx_g=vec(rwkv_lnx_g), lnx_b=vec(rwkv_lnx_b),
                                 l=l, tm=tm)
        o_c = post(y0l, y1l, seg_c, t_ctx, prev=post(y0c, y1c, seg_c, 0))

        x = _merge(xs, o_a, o_b, o_c, mod, w_g_b, w_branch_b, w_out_b, vec(ln1_g), vec(ln1_b), l, tm, t_ctx, l_lat, alpha)
        x = _ffn(x, mod, w_up_b, w_down_b, vec(ln2_g), vec(ln2_b), l, tm_ffn, 1024, t_ctx, l_lat, alpha,
                 split=(l == depth - 1))
        xs = (x,)

    y = x[0].reshape(b_ctx, l_ctx, D_MODEL)
    z = x[1].reshape(b_lat, l_lat, D_MODEL)
    return (y, z, jnp.stack(new_k, axis=1), jnp.stack(new_v, axis=1), jnp.stack(new_s, axis=1))
```

```python
import functools
import math

import jax
import jax.numpy as jnp
from jax import lax
from jax.experimental import pallas as pl
from jax.experimental.pallas import tpu as pltpu

F32 = jnp.float32
BF16 = jnp.bfloat16

D_MODEL = 1024
HALF = D_MODEL // 2
HEAD = 64
N_HEADS = HALF // HEAD
N_KV = 2
GQA = N_HEADS // N_KV
KV_W = N_KV * HEAD
GRID_W = 64
SGU_CHUNK = 128
SGU_GROUPS = 4
LORA = 64
GATE_LORA = 128
D_FF = 4 * D_MODEL
ROPE_THETA = 10000.0
GN_EPS = 64e-5
SEG_A = 2 * HALF
SEG_B = HALF + 2 * KV_W
SEG_C = 3 * HALF + 4 * LORA + GATE_LORA
SEG_G = 3 * D_MODEL
RKV_W = 3 * HALF
B_OFF = 2048
SEG_BC = B_OFF + SEG_B
FFN_GROUP = 512
LANES = 128
NB_ROWS = 16
VMEM_LIMIT = 48 * 1024 * 1024


def _cparams(*sem):
    return pltpu.CompilerParams(dimension_semantics=sem, vmem_limit_bytes=VMEM_LIMIT)


def _dot(a, b):
    return jnp.dot(a, b, preferred_element_type=F32)


def _dot_nt(a, b):
    return lax.dot_general(a, b, (((1,), (1,)), ((), ())), preferred_element_type=F32)


def _dot_tn(a, b):
    return lax.dot_general(a, b, (((0,), (0,)), ((), ())), preferred_element_type=F32)


def _layer_norm(x, g, b, eps=1e-5):
    mu = jnp.mean(x, axis=-1, keepdims=True)
    xc = x - mu
    var = jnp.mean(xc * xc, axis=-1, keepdims=True)
    return xc * lax.rsqrt(var + eps) * g + b


def _ada_kernel(c_ref, w_ref, b_ref, o_ref):
    c = c_ref[...]
    s = (c * jax.nn.sigmoid(c)).astype(BF16)
    o_ref[0] = _dot(s, w_ref[0].astype(BF16)) + b_ref[0]


def _ada(cvec, w_ada, b_ada):
    depth, _, n = w_ada.shape
    r = cvec.shape[0]
    tn = 1536
    return pl.pallas_call(
        _ada_kernel,
        grid=(depth, n // tn),
        in_specs=[pl.BlockSpec((r, D_MODEL), lambda l, j: (0, 0)),
                  pl.BlockSpec((1, D_MODEL, tn), lambda l, j: (l, 0, j)),
                  pl.BlockSpec((1, 1, tn), lambda l, j: (l, 0, j))],
        out_specs=pl.BlockSpec((1, r, tn), lambda l, j: (l, 0, j)),
        out_shape=jax.ShapeDtypeStruct((depth, r, n), F32),
        compiler_params=_cparams("parallel", "parallel"),
        name="ada",
    )(cvec, w_ada, b_ada)


def _token_tile(x_refs, n_first):
    if len(x_refs) == 1:
        return x_refs[0][...]
    return jnp.where(pl.program_id(0) < n_first, x_refs[0][...], x_refs[1][...])


def _token_specs(xs, tm, n_first, index=lambda i, *_: i):
    if len(xs) == 1:
        return [pl.BlockSpec((tm, D_MODEL), lambda *g: (index(*g), 0))]
    return [pl.BlockSpec((tm, D_MODEL), lambda *g: (jnp.minimum(index(*g), n_first - 1), 0)),
            pl.BlockSpec((tm, D_MODEL), lambda *g: (jnp.maximum(index(*g) - n_first, 0), 0))]


def _mod_row(i, tm, t_ctx, l_lat):
    r = i * tm
    return jnp.where(r < t_ctx, 0, 1 + (r - t_ctx) // l_lat)


def _proj_kernel(*refs, tm, n_x, n_first):
    sh_ref, sc_ref, wa_ref, wbc_ref, g_ref, b_ref, ws_ref, bs_ref, oa_ref, obc_ref = refs[n_x:]
    h = (_token_tile(refs[:n_x], n_first) * (1.0 + sc_ref[0]) + sh_ref[0]).astype(BF16)
    n_bc = wbc_ref.shape[2]
    for j in range(2):
        cols = slice(j * (n_bc // 2), (j + 1) * (n_bc // 2))
        obc_ref[:, cols] = _dot(h, wbc_ref[0, :, cols]).astype(obc_ref.dtype)
    uv = _dot(h, wa_ref[0])
    vn = _layer_norm(uv[:, HALF:], g_ref[0], b_ref[0]).astype(BF16)
    gc = HALF // SGU_GROUPS
    for n in range(tm // SGU_CHUNK):
        rows = slice(n * SGU_CHUNK, (n + 1) * SGU_CHUNK)
        for g in range(SGU_GROUPS):
            cols = slice(g * gc, (g + 1) * gc)
            s = _dot(ws_ref[0, g], vn[rows, cols]) + bs_ref[0, :, cols]
            oa_ref[rows, cols] = (uv[rows, cols] * s).astype(oa_ref.dtype)


def _proj(xs, mod, w_a, w_bc, ln_g, ln_b, w_s, b_s_full, l, tm, t_ctx, l_lat):
    t = sum(x.shape[0] for x in xs)
    n_first = t_ctx // tm
    row = functools.partial(_mod_row, tm=tm, t_ctx=t_ctx, l_lat=l_lat)
    return pl.pallas_call(
        functools.partial(_proj_kernel, tm=tm, n_x=len(xs), n_first=n_first),
        grid=(t // tm,),
        in_specs=_token_specs(xs, tm, n_first) + [
                  pl.BlockSpec((1, 1, D_MODEL), lambda i: (row(i), 0, 0)),
                  pl.BlockSpec((1, 1, D_MODEL), lambda i: (row(i), 0, 1)),
                  pl.BlockSpec((1, D_MODEL, SEG_A), lambda i: (l, 0, 0)),
                  pl.BlockSpec((1, D_MODEL, SEG_BC), lambda i: (l, 0, 0)),
                  pl.BlockSpec((1, 1, HALF), lambda i: (l, 0, 0)),
                  pl.BlockSpec((1, 1, HALF), lambda i: (l, 0, 0)),
                  pl.BlockSpec((1, SGU_GROUPS, SGU_CHUNK, SGU_CHUNK), lambda i: (l, 0, 0, 0)),
                  pl.BlockSpec((1, SGU_CHUNK, HALF), lambda i: (l, 0, 0))],
        out_specs=[pl.BlockSpec((tm, HALF), lambda i: (i, 0)),
                   pl.BlockSpec((tm, SEG_BC), lambda i: (i, 0))],
        out_shape=[jax.ShapeDtypeStruct((t, HALF), BF16), jax.ShapeDtypeStruct((t, SEG_BC), BF16)],
        compiler_params=_cparams("parallel"),
        name="proj",
    )(*xs, mod, mod, w_a, w_bc, ln_g, ln_b, w_s, b_s_full)


def _rope_swap(x):
    lane = lax.broadcasted_iota(jnp.int32, x.shape, 1)
    up = pltpu.roll(x, LANES - 16, axis=1)
    dn = pltpu.roll(x, 16, axis=1)
    return jnp.where((lane & 16) == 0, up, dn)


def _head_rms(x, seg_ref, g):
    ms = _dot((x * x).astype(BF16), seg_ref[...])
    return x * lax.rsqrt(ms + 1e-6) * g


def _kvprep_kernel(k_ref, v_ref, g_ref, seg_ref, cos_ref, sin_ref, *out_refs, rope):
    kn = _head_rms(k_ref[...].astype(F32), seg_ref, g_ref[...])
    if rope:
        kt_ref, vb_ref = out_refs
        kn = kn * cos_ref[...] + _rope_swap(kn) * sin_ref[...]
    else:
        kn_ref, kt_ref, vb_ref = out_refs
        kn_ref[...] = kn
    kt_ref[0] = kn.T.astype(BF16)
    vb_ref[...] = v_ref[...].astype(BF16)


def _kvprep(seg_b, row_off, b, l, k_norm2, seg_mat, cos, sin, rope, tk):
    t = b * l
    off = row_off // tk
    lb = l // tk
    out_shape = [jax.ShapeDtypeStruct((b, KV_W, l), BF16), jax.ShapeDtypeStruct((t, KV_W), BF16)]
    out_specs = [pl.BlockSpec((1, KV_W, tk), lambda i: (i // lb, 0, i % lb)),
                 pl.BlockSpec((tk, KV_W), lambda i: (i, 0))]
    if not rope:
        out_shape = [jax.ShapeDtypeStruct((t, KV_W), F32)] + out_shape
        out_specs = [pl.BlockSpec((tk, KV_W), lambda i: (i, 0))] + out_specs
    return pl.pallas_call(
        functools.partial(_kvprep_kernel, rope=rope),
        grid=(t // tk,),
        in_specs=[pl.BlockSpec((tk, KV_W), lambda i: (off + i, (B_OFF + HALF) // KV_W)),
                  pl.BlockSpec((tk, KV_W), lambda i: (off + i, (B_OFF + HALF) // KV_W + 1)),
                  pl.BlockSpec((1, KV_W), lambda i: (0, 0)),
                  pl.BlockSpec((KV_W, KV_W), lambda i: (0, 0)),
                  pl.BlockSpec((tk, KV_W), lambda i: (i % lb, 0)),
                  pl.BlockSpec((tk, KV_W), lambda i: (i % lb, 0))],
        out_specs=out_specs,
        out_shape=out_shape,
        compiler_params=_cparams("parallel"),
        name="kvprep",
    )(seg_b, seg_b, k_norm2, seg_mat, cos, sin)


def _attn_kernel(q_ref, g_ref, seg_ref, cos_ref, sin_ref, kt_ref, v_ref, *rest, rope, tq):
    o_ref = rest[-1]
    qs = []
    for s in range(HALF // LANES):
        q = _head_rms(q_ref[:, s * LANES:(s + 1) * LANES].astype(F32), seg_ref, g_ref[...])
        if rope:
            q = q * cos_ref[...] + _rope_swap(q) * sin_ref[...]
        qs.append((q * (HEAD ** -0.5 * math.log2(math.e))).astype(BF16))
    lo = lax.broadcasted_iota(jnp.int32, (tq, LANES), 1) < HEAD
    scores, probs, ratios = {}, {}, {}

    kt = kt_ref[0]
    zeros = jnp.zeros((HEAD, kt.shape[1]), BF16)
    k_slab = {}
    for g in range(N_KV):
        kg = kt[g * HEAD:(g + 1) * HEAD]
        k_slab[g, 0] = jnp.concatenate([kg, zeros], axis=0)
        k_slab[g, 1] = jnp.concatenate([zeros, kg], axis=0)
    lane_row = lax.broadcasted_iota(jnp.int32, (1, LANES), 1)
    own = [(lane_row < HEAD).astype(F32).astype(BF16), (lane_row >= HEAD).astype(F32).astype(BF16)]
    v_slab = [v_ref[0] * own[g] + own[1 - g] for g in range(N_KV)]

    def qk(h):
        scores[h] = _dot(qs[h // 2], k_slab[h // GQA, h % 2])

    def softmax(h):
        s = scores.pop(h)
        probs[h] = jnp.exp2(s - jnp.max(s, axis=-1, keepdims=True)).astype(BF16)

    def pv(h):
        g = h // GQA
        oe = _dot(probs.pop(h), v_slab[g])
        sw = pltpu.roll(oe, HEAD, axis=1)
        ratios[h] = oe / sw if (h % 2 == 0) == (g == 0) else sw / oe
        if h % 2 == 1:
            pair = jnp.where(lo, ratios.pop(h - 1), ratios.pop(h))
            o_ref[:, (h // 2) * LANES:(h // 2 + 1) * LANES] = pair.astype(o_ref.dtype)

    for t in range(N_HEADS + 2):
        if t < N_HEADS:
            qk(t)
        if 0 <= t - 1 < N_HEADS:
            softmax(t - 1)
        if 0 <= t - 2 < N_HEADS:
            pv(t - 2)


def _attention(seg_b, row_off, b, l, kt, v, q_norm2, seg_mat, cos, sin, rope, tq, prev=None):
    t_all = seg_b.shape[0]
    off = row_off // tq
    lb = l // tq
    lk = kt.shape[-1]
    in_specs = [pl.BlockSpec((tq, HALF), lambda bi, i: (off + bi * lb + i, B_OFF // HALF)),
                pl.BlockSpec((1, LANES), lambda bi, i: (0, 0)),
                pl.BlockSpec((KV_W, KV_W), lambda bi, i: (0, 0)),
                pl.BlockSpec((tq, LANES), lambda bi, i: (i, 0)),
                pl.BlockSpec((tq, LANES), lambda bi, i: (i, 0)),
                pl.BlockSpec((1, KV_W, lk), lambda bi, i: (bi, 0, 0)),
                pl.BlockSpec((1, lk, KV_W), lambda bi, i: (bi, 0, 0))]
    args = [seg_b, q_norm2, seg_mat, cos, sin, kt, v]
    aliases = {}
    if prev is not None:
        aliases = {len(args): 0}
        in_specs.append(pl.BlockSpec(memory_space=pl.ANY))
        args.append(prev)
    return pl.pallas_call(
        functools.partial(_attn_kernel, rope=rope, tq=tq),
        grid=(b, lb),
        in_specs=in_specs,
        out_specs=pl.BlockSpec((tq, HALF), lambda bi, i: (off + bi * lb + i, 0)),
        out_shape=jax.ShapeDtypeStruct((t_all, HALF), BF16),
        input_output_aliases=aliases,
        compiler_params=_cparams("parallel", "parallel"),
        name="attention",
    )(*args)


def _split2(x):
    h = x.astype(BF16)
    return h, (x - h.astype(F32)).astype(BF16)


def _sigmoid(x):
    return 0.5 * jnp.tanh(0.5 * x) + 0.5


def _wkv_prep(x, nb_row, d, c, mu_rkv, mu_lo, w0, a0, wwa, k_k, k_a, r_k, seg1, y_ref, rev):
    rows = lax.broadcasted_iota(jnp.int32, (c, 1), 0)
    edge = (c - 1) if rev else 0

    def shifted(cur, nb):
        rolled = pltpu.roll(cur, (c - 1) if rev else 1, axis=0)
        return jnp.where(rows == edge, nb, rolled)

    rkv = x[:, :RKV_W]
    lo = x[:, RKV_W + 2 * LORA * d:RKV_W + 2 * LORA * (d + 1)]
    f = rkv + mu_rkv * (shifted(rkv, nb_row[:, :RKV_W]) - rkv)
    fl = lo + mu_lo * (shifted(lo, nb_row[:, RKV_W + 2 * LORA * d:RKV_W + 2 * LORA * (d + 1)]) - lo)
    r = f[:, :HALF]
    k = f[:, HALF:2 * HALF]
    v = f[:, 2 * HALF:]
    lane = lax.broadcasted_iota(jnp.int32, fl.shape, 1)
    lin = _dot(jnp.where(lane < LORA, jnp.tanh(fl), fl).astype(BF16), wwa)
    lw = (-math.exp(-0.5)) * _sigmoid(w0 + lin[:, :HALF])
    asig = _sigmoid(a0 + lin[:, HALF:])
    kk = k * k_k
    ss = _dot((kk * kk).astype(BF16), seg1)
    kkn = kk * lax.rsqrt(jnp.maximum(ss, 1e-24))
    kmod = k * (1.0 + (asig - 1.0) * k_a)
    bonus = _dot((r * kmod * r_k).astype(BF16), seg1) * v
    y_ref[:, HALF:] = bonus

    ti = lax.broadcasted_iota(jnp.int32, (c, c), 0)
    si = lax.broadcasted_iota(jnp.int32, (c, c), 1)
    incl = (si >= ti) if rev else (si <= ti)
    strict = (si > ti) if rev else (si < ti)
    tri = incl.astype(BF16)
    hi2, lo2 = _split2(lw)
    cum = _dot(tri, hi2) + _dot(tri, lo2)
    ref = cum[c // 2:c // 2 + 1]
    end = 0 if rev else c - 1
    cum_end = cum[end:end + 1]
    g = cum - ref
    e_pos = jnp.exp(g)
    e_neg = jnp.exp(-g)
    e_ref = jnp.exp(ref)
    e_tot = jnp.exp(cum_end)
    e_end = jnp.exp(cum_end - ref)
    at_c = -kkn * jnp.exp(g - lw)
    rt_c = r * e_pos
    bt = kkn * asig * e_neg
    kt = kmod * e_neg
    return dict(at_c=at_c, rt_c=rt_c, bt=bt, kt=kt, at_true=at_c * e_ref, rt_true=rt_c * e_ref,
                bh=bt * e_end, kh=kt * e_end, v=v, e_tot=e_tot, incl=incl, strict=strict)


def _wkv_chains(preps, c, s_ref, y_refs):
    lane = lax.broadcasted_iota(jnp.int32, (c, LANES), 1)
    lo = lane < HEAD
    hi = jnp.logical_not(lo)
    chains = [(d, p) for d in range(2) for p in range(N_HEADS // 2)]

    def slab(d, p, name):
        return preps[d][name][:, p * LANES:(p + 1) * LANES]

    def keep(mask, x):
        return jnp.where(mask, x, 0.0)

    n_pow, a_ak, m_all, vsw, x_cur = {}, {}, {}, {}, {}
    for ch in chains:
        d, p = ch
        at_c, rt_c = slab(d, p, "at_c"), slab(d, p, "rt_c")
        lhs = jnp.concatenate([keep(lo, at_c), keep(hi, at_c), keep(lo, rt_c), keep(hi, rt_c)], axis=0)
        rhs = jnp.concatenate([slab(d, p, "bt"), slab(d, p, "kt")], axis=0)
        g = _dot_nt(lhs.astype(BF16), rhs.astype(BF16))
        strict, incl = preps[d]["strict"], preps[d]["incl"]
        incl2 = jnp.concatenate([incl, incl], axis=1)
        n_pow[ch] = [keep(strict, g[h * c:(h + 1) * c, :c]).astype(BF16) for h in range(2)]
        a_ak[ch] = [keep(strict, g[h * c:(h + 1) * c, c:]).astype(BF16) for h in range(2)]
        m_all[ch] = [keep(incl2, g[(2 + h) * c:(3 + h) * c, :]).astype(BF16) for h in range(2)]
        v_sw = pltpu.roll(slab(d, p, "v"), HEAD, axis=1)
        vsw[ch] = [keep(hi, v_sw).astype(BF16), keep(lo, v_sw).astype(BF16)]
    for ch in chains:
        d, p = ch
        at_true = slab(d, p, "at_true")
        x_cur[ch] = [keep(lo, at_true) + _dot(a_ak[ch][0], vsw[ch][0]),
                     keep(hi, at_true) + _dot(a_ak[ch][1], vsw[ch][1])]
    steps = int(math.log2(c))
    for j in range(steps):
        for ch in chains:
            for h in range(2):
                xb = x_cur[ch][h].astype(BF16)
                if j + 1 < steps:
                    out = _dot(n_pow[ch][h], jnp.concatenate([xb, n_pow[ch][h]], axis=1))
                    x_cur[ch][h] = x_cur[ch][h] + out[:, :LANES]
                    n_pow[ch][h] = out[:, LANES:].astype(BF16)
                else:
                    x_cur[ch][h] = x_cur[ch][h] + _dot(n_pow[ch][h], xb)
    st, w, kb = {}, {}, {}
    for ch in chains:
        d, p = ch
        x0, x1 = x_cur[ch]
        st[ch] = _dot_nt(jnp.concatenate([x0, x1, slab(d, p, "rt_true")], axis=0).astype(BF16),
                         s_ref[d, p].astype(BF16))
        bh, kh = slab(d, p, "bh"), slab(d, p, "kh")
        kb[ch] = jnp.concatenate([keep(lo, bh), keep(lo, kh), keep(hi, bh), keep(hi, kh)], axis=0).astype(BF16)
    for ch in chains:
        x0, x1 = x_cur[ch]
        w[ch] = [jnp.concatenate([keep(hi, st[ch][:c] + x0).astype(BF16), vsw[ch][0]], axis=0),
                 jnp.concatenate([keep(lo, st[ch][c:2 * c] + x1).astype(BF16), vsw[ch][1]], axis=0)]
    for ch in chains:
        d, p = ch
        e_tot = preps[d]["e_tot"][:, p * LANES:(p + 1) * LANES]
        s_ref[d, p] = s_ref[d, p] * e_tot + _dot_tn(jnp.concatenate(w[ch], axis=0), kb[ch])
    for ch in chains:
        d, p = ch
        y_sw = st[ch][2 * c:] + _dot(m_all[ch][0], w[ch][0]) + _dot(m_all[ch][1], w[ch][1])
        y_refs[d][:, p * LANES:(p + 1) * LANES] = pltpu.roll(y_sw, HEAD, axis=1)


def _wkv_kernel(*refs, c, n_c, latent):
    if latent:
        (x0_ref, x1_ref, p0_ref, n1_ref, s0_ref, mu_ref, w0_ref, a0_ref, wwa_ref, kk_ref, ka_ref, rk_ref,
         seg_ref, y0_ref, y1_ref, sf_ref, s_ref) = refs
    else:
        (x0_ref, x1_ref, p0_ref, n1_ref, mu_ref, w0_ref, a0_ref, wwa_ref, kk_ref, ka_ref, rk_ref,
         seg_ref, y0_ref, y1_ref, sf_ref, s_ref) = refs
    i = pl.program_id(1)

    @pl.when(i == 0)
    def _():
        s_ref[...] = s0_ref[0] if latent else jnp.zeros(s_ref.shape, F32)

    inner = (i > 0).astype(F32)
    preps = []
    for d, (x_ref, nb_ref, y_ref) in enumerate(((x0_ref, p0_ref, y0_ref), (x1_ref, n1_ref, y1_ref))):
        nb = nb_ref[NB_ROWS - 1:NB_ROWS, :] if d == 0 else nb_ref[0:1, :]
        preps.append(_wkv_prep(x_ref[...].astype(F32), nb.astype(F32) * inner, d, c,
                               mu_ref[0, d, :, :RKV_W], mu_ref[0, d, :, RKV_W:], w0_ref[0, d], a0_ref[0, d],
                               wwa_ref[0, d], kk_ref[0], ka_ref[0], rk_ref[0], seg_ref[...], y_ref, rev=(d == 1)))
    _wkv_chains(preps, c, s_ref, (y0_ref, y1_ref))

    @pl.when(i == n_c - 1)
    def _():
        sf_ref[0] = s_ref[...]


def _wkv(seg_c, row_off, b, l, c, s0, layer, mu, w0, a0, wwa, k_k, k_a, r_k, seg1):
    t = b * l
    n_c = l // c
    t_all = seg_c.shape[0]
    cb = row_off // c
    c8 = c // NB_ROWS
    r8 = row_off // NB_ROWS
    last8 = t_all // NB_ROWS - 1
    latent = s0 is not None
    const2 = lambda bi, i: (0, 0)
    in_specs = [pl.BlockSpec((c, SEG_C), lambda bi, i: (cb + bi * n_c + i, 0)),
                pl.BlockSpec((c, SEG_C), lambda bi, i: (cb + bi * n_c + n_c - 1 - i, 0)),
                pl.BlockSpec((NB_ROWS, SEG_C),
                             lambda bi, i: (jnp.maximum(r8 + (bi * n_c + i) * c8 - 1, 0), 0)),
                pl.BlockSpec((NB_ROWS, SEG_C),
                             lambda bi, i: (jnp.minimum(r8 + (bi * n_c + n_c - i) * c8, last8), 0))]
    args = [seg_c, seg_c, seg_c, seg_c]
    if latent:
        in_specs.append(pl.BlockSpec((1, 2, N_HEADS // 2, LANES, LANES), lambda bi, i: (bi, 0, 0, 0, 0)))
        args.append(s0)
    per_layer = lambda p: pl.BlockSpec((1,) + p.shape[1:], lambda bi, i: (layer,) + (0,) * (p.ndim - 1))
    in_specs += [per_layer(p) for p in (mu, w0, a0, wwa, k_k, k_a, r_k)] + [pl.BlockSpec(seg1.shape, const2)]
    args += [mu, w0, a0, wwa, k_k, k_a, r_k, seg1]
    return pl.pallas_call(
        functools.partial(_wkv_kernel, c=c, n_c=n_c, latent=latent),
        grid=(b, n_c),
        in_specs=in_specs,
        out_specs=[pl.BlockSpec((c, 2 * HALF), lambda bi, i: (bi * n_c + i, 0)),
                   pl.BlockSpec((c, 2 * HALF), lambda bi, i: (bi * n_c + n_c - 1 - i, 0)),
                   pl.BlockSpec((1, 2, N_HEADS // 2, LANES, LANES), lambda bi, i: (bi, 0, 0, 0, 0))],
        out_shape=[jax.ShapeDtypeStruct((t, 2 * HALF), F32), jax.ShapeDtypeStruct((t, 2 * HALF), F32),
                   jax.ShapeDtypeStruct((b, 2, N_HEADS // 2, LANES, LANES), F32)],
        scratch_shapes=[pltpu.VMEM((2, N_HEADS // 2, LANES, LANES), F32)],
        compiler_params=_cparams("parallel", "arbitrary"),
        name="wkv",
    )(*args)


def _rwkv_post_kernel(y0_ref, y1_ref, gd_ref, segm_ref, g2_ref, lg_ref, lb_ref, *rest):
    o_ref = rest[-1]
    ys = y0_ref[:, :HALF] + y1_ref[:, :HALF]
    bonus = y0_ref[:, HALF:] + y1_ref[:, HALF:]
    mu = _dot(ys.astype(BF16), segm_ref[...])
    yc = ys - mu
    var = _dot((yc * yc).astype(BF16), segm_ref[...])
    gn = yc * lax.rsqrt(var + GN_EPS) * lg_ref[0] + lb_ref[0]
    gate = _dot(jax.nn.sigmoid(gd_ref[...].astype(F32)).astype(BF16), g2_ref[0])
    o_ref[...] = ((gn + bonus) * gate).astype(o_ref.dtype)


def _rwkv_post(y0, y1, seg_c, row_off, segm, g2, lnx_g, lnx_b, l, tm, prev=None):
    t = y0.shape[0]
    t_all = seg_c.shape[0]
    off = row_off // tm
    in_specs = [pl.BlockSpec((tm, 2 * HALF), lambda i: (i, 0)),
                pl.BlockSpec((tm, 2 * HALF), lambda i: (i, 0)),
                pl.BlockSpec((tm, GATE_LORA), lambda i: (off + i, (SEG_C - GATE_LORA) // GATE_LORA)),
                pl.BlockSpec((HALF, HALF), lambda i: (0, 0)),
                pl.BlockSpec((1, GATE_LORA, HALF), lambda i: (l, 0, 0)),
                pl.BlockSpec((1, 1, HALF), lambda i: (l, 0, 0)),
                pl.BlockSpec((1, 1, HALF), lambda i: (l, 0, 0))]
    args = [y0, y1, seg_c, segm, g2, lnx_g, lnx_b]
    aliases = {}
    if prev is not None:
        aliases = {len(args): 0}
        in_specs.append(pl.BlockSpec(memory_space=pl.ANY))
        args.append(prev)
    return pl.pallas_call(
        _rwkv_post_kernel,
        grid=(t // tm,),
        in_specs=in_specs,
        out_specs=pl.BlockSpec((tm, HALF), lambda i: (off + i, 0)),
        out_shape=jax.ShapeDtypeStruct((t_all, HALF), BF16),
        input_output_aliases=aliases,
        compiler_params=_cparams("parallel"),
        name="rwkv_post",
    )(*args)


def _merge_kernel(*refs, alpha, n_x, n_first):
    sh_ref, sc_ref, g1_ref, oa_ref, ob_ref, oc_ref, wg_ref, wb_ref, wo_ref, lg_ref, lb_ref, o_ref = refs[n_x:]
    x = _token_tile(refs[:n_x], n_first)
    h = (x * (1.0 + sc_ref[0]) + sh_ref[0]).astype(BF16)
    acc = None
    for j, br in enumerate((oa_ref, ob_ref, oc_ref)):
        gate = jax.nn.sigmoid(_dot(h, wg_ref[0, :, j * D_MODEL:(j + 1) * D_MODEL]))
        term = gate * _dot(br[...], wb_ref[0, j])
        acc = term if acc is None else acc + term
    mixed = _dot(acc.astype(BF16), wo_ref[0])
    o_ref[...] = _layer_norm(alpha * x + g1_ref[0] * mixed, lg_ref[0], lb_ref[0])


def _merge(xs, o_a, o_b, o_c, mod, w_g, w_branch, w_out, ln_g, ln_b, l, tm, t_ctx, l_lat, alpha):
    t = sum(x.shape[0] for x in xs)
    n_first = t_ctx // tm
    row = functools.partial(_mod_row, tm=tm, t_ctx=t_ctx, l_lat=l_lat)
    tok = lambda w: pl.BlockSpec((tm, w), lambda i: (i, 0))
    modspec = lambda blk: pl.BlockSpec((1, 1, D_MODEL), lambda i: (row(i), 0, blk))
    return pl.pallas_call(
        functools.partial(_merge_kernel, alpha=alpha, n_x=len(xs), n_first=n_first),
        grid=(t // tm,),
        in_specs=_token_specs(xs, tm, n_first) + [
                  modspec(0), modspec(1), modspec(2), tok(HALF), tok(HALF), tok(HALF),
                  pl.BlockSpec((1, D_MODEL, SEG_G), lambda i: (l, 0, 0)),
                  pl.BlockSpec((1, 3, HALF, D_MODEL), lambda i: (l, 0, 0, 0)),
                  pl.BlockSpec((1, D_MODEL, D_MODEL), lambda i: (l, 0, 0)),
                  pl.BlockSpec((1, 1, D_MODEL), lambda i: (l, 0, 0)),
                  pl.BlockSpec((1, 1, D_MODEL), lambda i: (l, 0, 0))],
        out_specs=tok(D_MODEL),
        out_shape=jax.ShapeDtypeStruct((t, D_MODEL), F32),
        compiler_params=_cparams("parallel"),
        name="merge",
    )(*xs, mod, mod, mod, o_a, o_b, o_c, w_g, w_branch, w_out, ln_g, ln_b)


def _ffn_kernel(x_ref, sh_ref, sc_ref, g2_ref, wu_ref, wd_ref, lg_ref, lb_ref, *rest, alpha, n_f, n_first):
    *o_refs, h_ref, acc_ref = rest
    i = pl.program_id(0)
    j = pl.program_id(1)

    @pl.when(j == 0)
    def _():
        h_ref[...] = (x_ref[...] * (1.0 + sc_ref[0]) + sh_ref[0]).astype(BF16)
        acc_ref[...] = jnp.zeros(acc_ref.shape, F32)

    h = h_ref[...]
    n_g = wu_ref.shape[2] // FFN_GROUP
    cols = lambda g: slice(g * FFN_GROUP, (g + 1) * FFN_GROUP)
    ups, acts, downs = {}, {}, []
    for t in range(n_g + 2):
        if t < n_g:
            ups[t] = _dot(h, wu_ref[0, :, cols(t)])
        if 0 <= t - 1 < n_g:
            u = jnp.maximum(ups.pop(t - 1), 0.0)
            acts[t - 1] = (u * u).astype(BF16)
        if 0 <= t - 2 < n_g:
            downs.append(_dot(acts.pop(t - 2), wd_ref[0, cols(t - 2), :]))
    acc_ref[...] += functools.reduce(lambda a, b: a + b, downs)

    def finish(o_ref):
        o_ref[...] = _layer_norm(alpha * x_ref[...] + g2_ref[0] * acc_ref[...], lg_ref[0], lb_ref[0])

    last = j == n_f - 1
    if len(o_refs) == 1:
        pl.when(last)(lambda: finish(o_refs[0]))
    else:
        pl.when(jnp.logical_and(last, i < n_first))(lambda: finish(o_refs[0]))
        pl.when(jnp.logical_and(last, i >= n_first))(lambda: finish(o_refs[1]))


def _ffn(x, mod, w_up, w_down, ln_g, ln_b, l, tm, tf, t_ctx, l_lat, alpha, split=False):
    t = x.shape[0]
    n_f = D_FF // tf
    n_first = t_ctx // tm
    if split:
        out_specs = [pl.BlockSpec((tm, D_MODEL), lambda i, j: (jnp.minimum(i, n_first - 1), 0)),
                     pl.BlockSpec((tm, D_MODEL), lambda i, j: (jnp.maximum(i - n_first, 0), 0))]
        out_shape = [jax.ShapeDtypeStruct((t_ctx, D_MODEL), F32), jax.ShapeDtypeStruct((t - t_ctx, D_MODEL), F32)]
    else:
        out_specs = pl.BlockSpec((tm, D_MODEL), lambda i, j: (i, 0))
        out_shape = jax.ShapeDtypeStruct((t, D_MODEL), F32)
    row = functools.partial(_mod_row, tm=tm, t_ctx=t_ctx, l_lat=l_lat)
    modspec = lambda blk: pl.BlockSpec((1, 1, D_MODEL), lambda i, j: (row(i), 0, blk))
    return pl.pallas_call(
        functools.partial(_ffn_kernel, alpha=alpha, n_f=n_f, n_first=n_first),
        grid=(t // tm, n_f),
        in_specs=[pl.BlockSpec((tm, D_MODEL), lambda i, j: (i, 0)),
                  modspec(3), modspec(4), modspec(5),
                  pl.BlockSpec((1, D_MODEL, tf), lambda i, j: (l, 0, j)),
                  pl.BlockSpec((1, tf, D_MODEL), lambda i, j: (l, j, 0)),
                  pl.BlockSpec((1, 1, D_MODEL), lambda i, j: (l, 0, 0)),
                  pl.BlockSpec((1, 1, D_MODEL), lambda i, j: (l, 0, 0))],
        out_specs=out_specs,
        out_shape=out_shape,
        scratch_shapes=[pltpu.VMEM((tm, D_MODEL), BF16), pltpu.VMEM((tm, D_MODEL), F32)],
        compiler_params=_cparams("arbitrary", "arbitrary"),
        name="ffn",
    )(x, mod, mod, mod, w_up, w_down, ln_g, ln_b)


def _rope_tables(l):
    pos = jnp.arange(l, dtype=jnp.int32)
    row = (pos // GRID_W).astype(F32)
    col = (pos % GRID_W).astype(F32)
    half = HEAD // 2
    inv_freq = ROPE_THETA ** (-jnp.arange(0, half, 2, dtype=F32) / half)
    ang_r = row[:, None] * inv_freq[None, :]
    ang_c = col[:, None] * inv_freq[None, :]
    cos = jnp.concatenate([jnp.cos(ang_r), jnp.cos(ang_r), jnp.cos(ang_c), jnp.cos(ang_c)], axis=-1)
    sin = jnp.concatenate([-jnp.sin(ang_r), jnp.sin(ang_r), -jnp.sin(ang_c), jnp.sin(ang_c)], axis=-1)
    return jnp.tile(cos, (1, LANES // HEAD)), jnp.tile(sin, (1, LANES // HEAD))


def _head_block_matrix(width, value):
    idx = jnp.arange(width) // HEAD
    return jnp.where(idx[:, None] == idx[None, :], value, 0.0).astype(BF16)


def _pair_states(s):
    b = s.shape[0]
    s = s.reshape(b, 2, N_HEADS // 2, 2, HEAD, HEAD)
    z = jnp.zeros_like(s[:, :, :, 0])
    top = jnp.concatenate([z, s[:, :, :, 1]], axis=-1)
    bot = jnp.concatenate([s[:, :, :, 0], z], axis=-1)
    return jnp.concatenate([top, bot], axis=-2)


def _unpair_states(sp):
    b = sp.shape[0]
    s = jnp.stack([sp[..., HEAD:, :HEAD], sp[..., :HEAD, HEAD:]], axis=3)
    return s.reshape(b, 2, N_HEADS, HEAD, HEAD)


def _pick_tile(pref, *sizes):
    return min(pref, functools.reduce(math.gcd, sizes))


def kernel(x_prompt, x_sample, cache_k, cache_v, state_wkv, c, c_ctx, w_ada, b_ada, w_in, sgu_ln_g, sgu_ln_b, sgu_w, sgu_b, q_norm, k_norm, rwkv_mu, rwkv_w0, rwkv_w2, rwkv_a0, rwkv_a2, rwkv_k_k, rwkv_k_a, rwkv_r_k, rwkv_g2, rwkv_lnx_g, rwkv_lnx_b, w_branch, w_out, ln1_g, ln1_b, w_up, w_down, ln2_g, ln2_b):
    depth = w_in.shape[0]
    b_ctx, l_ctx, _ = x_prompt.shape
    b_lat, l_lat, _ = x_sample.shape
    past = cache_k.shape[2]
    t_ctx = b_ctx * l_ctx
    t_lat = b_lat * l_lat
    alpha = (2 * depth) ** 0.25

    tm = _pick_tile(512, t_ctx, l_lat)
    tm_ffn = _pick_tile(1024, t_ctx, l_lat)
    tk = _pick_tile(512, l_ctx, l_lat)
    tq_ctx = _pick_tile(256, l_ctx)
    tq_lat = _pick_tile(512, l_lat)
    c_ctx_chunk = _pick_tile(128, l_ctx)
    c_lat_chunk = _pick_tile(128, l_lat)

    n_rows = 1 + b_lat
    pad_rows = -n_rows % 16
    cvec = jnp.concatenate([c_ctx[None, :], c, jnp.zeros((pad_rows, D_MODEL), F32)], axis=0)
    mod_all = _ada(cvec, w_ada, b_ada[:, None, :])

    c_lo = SEG_A + SEG_B
    w_a_b = w_in[:, :, :SEG_A].astype(BF16)
    w_g_b = w_in[:, :, c_lo + SEG_C:].astype(BF16)
    w_bc_b = jnp.concatenate([w_in[:, :, c_lo:c_lo + SEG_C], jnp.zeros((depth, D_MODEL, B_OFF - SEG_C), F32),
                              w_in[:, :, SEG_A:c_lo]], axis=-1).astype(BF16)
    w_branch_b = w_branch.astype(BF16)
    w_out_b = w_out.astype(BF16)
    w_up_b = w_up.astype(BF16)
    w_down_b = w_down.astype(BF16)
    sgu_w_b = sgu_w.astype(BF16)
    g2_b = rwkv_g2.astype(BF16)
    seg_mean2 = _head_block_matrix(KV_W, 1.0 / HEAD)
    seg_mean8 = _head_block_matrix(HALF, 1.0 / HEAD)
    seg_ones8 = _head_block_matrix(HALF, 1.0)
    cos_ctx, sin_ctx = _rope_tables(l_ctx)
    cos_lat, sin_lat = _rope_tables(l_lat)
    zeros_lora = jnp.zeros((depth, 2, LORA, HALF), F32)
    wwa = jnp.concatenate([jnp.concatenate([rwkv_w2, zeros_lora], axis=-1),
                           jnp.concatenate([zeros_lora, rwkv_a2], axis=-1)], axis=-2).astype(BF16)

    xs = (x_prompt.reshape(t_ctx, D_MODEL), x_sample.reshape(t_lat, D_MODEL))
    b_s_full = jnp.repeat(jnp.swapaxes(sgu_b, 1, 2), HALF // SGU_GROUPS, axis=2)
    vec = lambda p: p[:, None, :]
    new_k, new_v, new_s = [], [], []
    for l in range(depth):
        mod = mod_all[l][:, None, :]
        o_a, seg_b = _proj(xs, mod, w_a_b, w_bc_b, vec(sgu_ln_g), vec(sgu_ln_b), sgu_w_b, b_s_full, l, tm, t_ctx, l_lat)
        seg_c = seg_b

        qn2 = jnp.tile(q_norm[l], LANES // HEAD)[None]
        kn2 = jnp.tile(k_norm[l], KV_W // HEAD)[None]
        kn_ctx, kr_ctx, vb_ctx = _kvprep(seg_b, 0, b_ctx, l_ctx, kn2, seg_mean2, cos_ctx, sin_ctx, False, tk)
        kr_lat, vb_lat = _kvprep(seg_b, t_ctx, b_lat, l_lat, kn2, seg_mean2, cos_lat, sin_lat, True, tk)
        new_k.append(kn_ctx.reshape(b_ctx, l_ctx, N_KV, HEAD))
        new_v.append(seg_b[:t_ctx, B_OFF + HALF + KV_W:].astype(F32).reshape(b_ctx, l_ctx, N_KV, HEAD))
        kt_lat = jnp.concatenate([jnp.swapaxes(cache_k[:, l].reshape(b_lat, past, KV_W), 1, 2).astype(BF16),
                                  kr_lat], axis=2)
        v_lat = jnp.concatenate([cache_v[:, l].reshape(b_lat, past, KV_W).astype(BF16),
                                 vb_lat.reshape(b_lat, l_lat, KV_W)], axis=1)
        ob_ctx = _attention(seg_b, 0, b_ctx, l_ctx, kr_ctx, vb_ctx.reshape(b_ctx, l_ctx, KV_W),
                            qn2, seg_mean2, cos_ctx, sin_ctx, False, tq_ctx)
        o_b = _attention(seg_b, t_ctx, b_lat, l_lat, kt_lat, v_lat,
                         qn2, seg_mean2, cos_lat, sin_lat, True, tq_lat, prev=ob_ctx)

        rw = dict(layer=l, mu=rwkv_mu[:, :, None, :], w0=rwkv_w0[:, :, None, :], a0=rwkv_a0[:, :, None, :], wwa=wwa,
                  k_k=vec(rwkv_k_k), k_a=vec(rwkv_k_a), r_k=rwkv_r_k.reshape(depth, 1, HALF), seg1=seg_ones8)
        y0c, y1c, s_ctx = _wkv(seg_c, 0, b_ctx, l_ctx, c_ctx_chunk, None, **rw)
        y0l, y1l, _ = _wkv(seg_c, t_ctx, b_lat, l_lat, c_lat_chunk, _pair_states(state_wkv[:, l]), **rw)
        new_s.append(_unpair_states(s_ctx))
        post = functools.partial(_rwkv_post, segm=seg_mean8, g2=g2_b, lnx_g=vec(rwkv_lnx_g), lnx_b=vec(rwkv_lnx_b),
                                 l=l, tm=tm)
        o_c = post(y0l, y1l, seg_c, t_ctx, prev=post(y0c, y1c, seg_c, 0))

        x = _merge(xs, o_a, o_b, o_c, mod, w_g_b, w_branch_b, w_out_b, vec(ln1_g), vec(ln1_b), l, tm, t_ctx, l_lat, alpha)
        x = _ffn(x, mod, w_up_b, w_down_b, vec(ln2_g), vec(ln2_b), l, tm_ffn, 1024, t_ctx, l_lat, alpha,
                 split=(l == depth - 1))
        xs = (x,)

    y = x[0].reshape(b_ctx, l_ctx, D_MODEL)
    z = x[1].reshape(b_lat, l_lat, D_MODEL)
    return (y, z, jnp.stack(new_k, axis=1), jnp.stack(new_v, axis=1), jnp.stack(new_s, axis=1))
```

```python
import functools
import math

import jax
import jax.numpy as jnp
from jax import lax
from jax.experimental import pallas as pl
from jax.experimental.pallas import tpu as pltpu

F32 = jnp.float32
BF16 = jnp.bfloat16

D_MODEL = 1024
HALF = D_MODEL // 2
HEAD = 64
N_HEADS = HALF // HEAD
N_KV = 2
GQA = N_HEADS // N_KV
KV_W = N_KV * HEAD
GRID_W = 64
SGU_CHUNK = 128
SGU_GROUPS = 4
LORA = 64
GATE_LORA = 128
D_FF = 4 * D_MODEL
ROPE_THETA = 10000.0
GN_EPS = 64e-5
SEG_A = 2 * HALF
SEG_B = HALF + 2 * KV_W
SEG_C = 3 * HALF + 4 * LORA + GATE_LORA
SEG_G = 3 * D_MODEL
RKV_W = 3 * HALF
B_OFF = 2048
SEG_BC = B_OFF + SEG_B
FFN_GROUP = 512
LANES = 128
NB_ROWS = 16
VMEM_LIMIT = 48 * 1024 * 1024


def _cparams(*sem):
    return pltpu.CompilerParams(dimension_semantics=sem, vmem_limit_bytes=VMEM_LIMIT)


def _dot(a, b):
    return jnp.dot(a, b, preferred_element_type=F32)


def _dot_nt(a, b):
    return lax.dot_general(a, b, (((1,), (1,)), ((), ())), preferred_element_type=F32)


def _dot_tn(a, b):
    return lax.dot_general(a, b, (((0,), (0,)), ((), ())), preferred_element_type=F32)


def _layer_norm(x, g, b, eps=1e-5):
    mu = jnp.mean(x, axis=-1, keepdims=True)
    xc = x - mu
    var = jnp.mean(xc * xc, axis=-1, keepdims=True)
    return xc * lax.rsqrt(var + eps) * g + b


def _ada_kernel(c_ref, w_ref, b_ref, o_ref):
    c = c_ref[...]
    s = (c * jax.nn.sigmoid(c)).astype(BF16)
    o_ref[0] = _dot(s, w_ref[0].astype(BF16)) + b_ref[0]


def _ada(cvec, w_ada, b_ada):
    depth, _, n = w_ada.shape
    r = cvec.shape[0]
    tn = 1536
    return pl.pallas_call(
        _ada_kernel,
        grid=(depth, n // tn),
        in_specs=[pl.BlockSpec((r, D_MODEL), lambda l, j: (0, 0)),
                  pl.BlockSpec((1, D_MODEL, tn), lambda l, j: (l, 0, j)),
                  pl.BlockSpec((1, 1, tn), lambda l, j: (l, 0, j))],
        out_specs=pl.BlockSpec((1, r, tn), lambda l, j: (l, 0, j)),
        out_shape=jax.ShapeDtypeStruct((depth, r, n), F32),
        compiler_params=_cparams("parallel", "parallel"),
        name="ada",
    )(cvec, w_ada, b_ada)


def _token_tile(x_refs, n_first):
    if len(x_refs) == 1:
        return x_refs[0][...]
    return jnp.where(pl.program_id(0) < n_first, x_refs[0][...], x_refs[1][...])


def _token_specs(xs, tm, n_first):
    width = xs[0].shape[1]
    if len(xs) == 1:
        return [pl.BlockSpec((tm, width), lambda i: (i, 0))]
    return [pl.BlockSpec((tm, width), lambda i: (jnp.minimum(i, n_first - 1), 0)),
            pl.BlockSpec((tm, width), lambda i: (jnp.maximum(i - n_first, 0), 0))]


def _mod_row(i, tm, t_ctx, l_lat):
    r = i * tm
    return jnp.where(r < t_ctx, 0, 1 + (r - t_ctx) // l_lat)


def _proj_kernel(*refs, tm, n_x, n_first):
    sh_ref, sc_ref, wa_ref, wbc_ref, g_ref, b_ref, ws_ref, bs_ref, oa_ref, obc_ref = refs[n_x:]
    h = (_token_tile(refs[:n_x], n_first) * (1.0 + sc_ref[0]) + sh_ref[0]).astype(BF16)
    n_bc = wbc_ref.shape[2]
    for j in range(2):
        cols = slice(j * (n_bc // 2), (j + 1) * (n_bc // 2))
        obc_ref[:, cols] = _dot(h, wbc_ref[0, :, cols]).astype(obc_ref.dtype)
    uv = _dot(h, wa_ref[0])
    vn = _layer_norm(uv[:, HALF:], g_ref[0], b_ref[0]).astype(BF16)
    gc = HALF // SGU_GROUPS
    for n in range(tm // SGU_CHUNK):
        rows = slice(n * SGU_CHUNK, (n + 1) * SGU_CHUNK)
        for g in range(SGU_GROUPS):
            cols = slice(g * gc, (g + 1) * gc)
            s = _dot(ws_ref[0, g], vn[rows, cols]) + bs_ref[0, :, cols]
            oa_ref[rows, cols] = (uv[rows, cols] * s).astype(oa_ref.dtype)


def _proj(xs, mod, w_a, w_bc, ln_g, ln_b, w_s, b_s_full, l, tm, t_ctx, l_lat):
    t = sum(x.shape[0] for x in xs)
    n_first = t_ctx // tm
    row = functools.partial(_mod_row, tm=tm, t_ctx=t_ctx, l_lat=l_lat)
    return pl.pallas_call(
        functools.partial(_proj_kernel, tm=tm, n_x=len(xs), n_first=n_first),
        grid=(t // tm,),
        in_specs=_token_specs(xs, tm, n_first) + [
                  pl.BlockSpec((1, 1, D_MODEL), lambda i: (row(i), 0, 0)),
                  pl.BlockSpec((1, 1, D_MODEL), lambda i: (row(i), 0, 1)),
                  pl.BlockSpec((1, D_MODEL, SEG_A), lambda i: (l, 0, 0)),
                  pl.BlockSpec((1, D_MODEL, SEG_BC), lambda i: (l, 0, 0)),
                  pl.BlockSpec((1, 1, HALF), lambda i: (l, 0, 0)),
                  pl.BlockSpec((1, 1, HALF), lambda i: (l, 0, 0)),
                  pl.BlockSpec((1, SGU_GROUPS, SGU_CHUNK, SGU_CHUNK), lambda i: (l, 0, 0, 0)),
                  pl.BlockSpec((1, SGU_CHUNK, HALF), lambda i: (l, 0, 0))],
        out_specs=[pl.BlockSpec((tm, HALF), lambda i: (i, 0)),
                   pl.BlockSpec((tm, SEG_BC), lambda i: (i, 0))],
        out_shape=[jax.ShapeDtypeStruct((t, HALF), BF16), jax.ShapeDtypeStruct((t, SEG_BC), BF16)],
        compiler_params=_cparams("parallel"),
        name="proj",
    )(*xs, mod, mod, w_a, w_bc, ln_g, ln_b, w_s, b_s_full)


def _rope_swap(x):
    lane = lax.broadcasted_iota(jnp.int32, x.shape, 1)
    up = pltpu.roll(x, LANES - 16, axis=1)
    dn = pltpu.roll(x, 16, axis=1)
    return jnp.where((lane & 16) == 0, up, dn)


def _head_rms(x, seg_ref, g):
    ms = _dot((x * x).astype(BF16), seg_ref[...])
    return x * lax.rsqrt(ms + 1e-6) * g


def _kvprep_kernel(k_ref, v_ref, g_ref, seg_ref, cos_ref, sin_ref, *out_refs, rope):
    kn = _head_rms(k_ref[...].astype(F32), seg_ref, g_ref[...])
    if rope:
        kt_ref, vb_ref = out_refs
        kn = kn * cos_ref[...] + _rope_swap(kn) * sin_ref[...]
    else:
        kn_ref, kt_ref, vb_ref = out_refs
        kn_ref[...] = kn
    kt_ref[0] = kn.T.astype(BF16)
    vb_ref[...] = v_ref[...].astype(BF16)


def _kvprep(seg_b, row_off, b, l, k_norm2, seg_mat, cos, sin, rope, tk):
    t = b * l
    off = row_off // tk
    lb = l // tk
    out_shape = [jax.ShapeDtypeStruct((b, KV_W, l), BF16), jax.ShapeDtypeStruct((t, KV_W), BF16)]
    out_specs = [pl.BlockSpec((1, KV_W, tk), lambda i: (i // lb, 0, i % lb)),
                 pl.BlockSpec((tk, KV_W), lambda i: (i, 0))]
    if not rope:
        out_shape = [jax.ShapeDtypeStruct((t, KV_W), F32)] + out_shape
        out_specs = [pl.BlockSpec((tk, KV_W), lambda i: (i, 0))] + out_specs
    return pl.pallas_call(
        functools.partial(_kvprep_kernel, rope=rope),
        grid=(t // tk,),
        in_specs=[pl.BlockSpec((tk, KV_W), lambda i: (off + i, (B_OFF + HALF) // KV_W)),
                  pl.BlockSpec((tk, KV_W), lambda i: (off + i, (B_OFF + HALF) // KV_W + 1)),
                  pl.BlockSpec((1, KV_W), lambda i: (0, 0)),
                  pl.BlockSpec((KV_W, KV_W), lambda i: (0, 0)),
                  pl.BlockSpec((tk, KV_W), lambda i: (i % lb, 0)),
                  pl.BlockSpec((tk, KV_W), lambda i: (i % lb, 0))],
        out_specs=out_specs,
        out_shape=out_shape,
        compiler_params=_cparams("parallel"),
        name="kvprep",
    )(seg_b, seg_b, k_norm2, seg_mat, cos, sin)


def _attn_kernel(q_ref, g_ref, seg_ref, cos_ref, sin_ref, kt_ref, v_ref, o_ref, *, rope, tq):
    qs = []
    for s in range(HALF // LANES):
        q = _head_rms(q_ref[:, s * LANES:(s + 1) * LANES].astype(F32), seg_ref, g_ref[...])
        if rope:
            q = q * cos_ref[...] + _rope_swap(q) * sin_ref[...]
        qs.append((q * (HEAD ** -0.5 * math.log2(math.e))).astype(BF16))
    lo = lax.broadcasted_iota(jnp.int32, (tq, LANES), 1) < HEAD
    scores, probs, ratios = {}, {}, {}

    kt = kt_ref[0]
    zeros = jnp.zeros((HEAD, kt.shape[1]), BF16)
    k_slab = {}
    for g in range(N_KV):
        kg = kt[g * HEAD:(g + 1) * HEAD]
        k_slab[g, 0] = jnp.concatenate([kg, zeros], axis=0)
        k_slab[g, 1] = jnp.concatenate([zeros, kg], axis=0)
    lane_row = lax.broadcasted_iota(jnp.int32, (1, LANES), 1)
    own = [(lane_row < HEAD).astype(F32).astype(BF16), (lane_row >= HEAD).astype(F32).astype(BF16)]
    v_slab = [v_ref[0] * own[g] + own[1 - g] for g in range(N_KV)]

    def qk(h):
        scores[h] = _dot(qs[h // 2], k_slab[h // GQA, h % 2])

    def softmax(h):
        s = scores.pop(h)
        probs[h] = jnp.exp2(s - jnp.max(s, axis=-1, keepdims=True)).astype(BF16)

    def pv(h):
        g = h // GQA
        oe = _dot(probs.pop(h), v_slab[g])
        sw = pltpu.roll(oe, HEAD, axis=1)
        ratios[h] = oe / sw if (h % 2 == 0) == (g == 0) else sw / oe
        if h % 2 == 1:
            pair = jnp.where(lo, ratios.pop(h - 1), ratios.pop(h))
            o_ref[:, (h // 2) * LANES:(h // 2 + 1) * LANES] = pair.astype(o_ref.dtype)

    for t in range(N_HEADS + 2):
        if t < N_HEADS:
            qk(t)
        if 0 <= t - 1 < N_HEADS:
            softmax(t - 1)
        if 0 <= t - 2 < N_HEADS:
            pv(t - 2)


def _attention(seg_b, row_off, b, l, kt, v, q_norm2, seg_mat, cos, sin, rope, tq):
    off = row_off // tq
    lb = l // tq
    lk = kt.shape[-1]
    in_specs = [pl.BlockSpec((tq, HALF), lambda bi, i: (off + bi * lb + i, B_OFF // HALF)),
                pl.BlockSpec((1, LANES), lambda bi, i: (0, 0)),
                pl.BlockSpec((KV_W, KV_W), lambda bi, i: (0, 0)),
                pl.BlockSpec((tq, LANES), lambda bi, i: (i, 0)),
                pl.BlockSpec((tq, LANES), lambda bi, i: (i, 0)),
                pl.BlockSpec((1, KV_W, lk), lambda bi, i: (bi, 0, 0)),
                pl.BlockSpec((1, lk, KV_W), lambda bi, i: (bi, 0, 0))]
    return pl.pallas_call(
        functools.partial(_attn_kernel, rope=rope, tq=tq),
        grid=(b, lb),
        in_specs=in_specs,
        out_specs=pl.BlockSpec((tq, HALF), lambda bi, i: (bi * lb + i, 0)),
        out_shape=jax.ShapeDtypeStruct((b * l, HALF), BF16),
        compiler_params=_cparams("parallel", "parallel"),
        name="attention",
    )(seg_b, q_norm2, seg_mat, cos, sin, kt, v)


def _split2(x):
    h = x.astype(BF16)
    return h, (x - h.astype(F32)).astype(BF16)


def _sigmoid(x):
    return 0.5 * jnp.tanh(0.5 * x) + 0.5


def _wkv_prep(x, nb_row, d, c, mu_rkv, mu_lo, w0, a0, wwa, k_k, k_a, r_k, seg1, y_ref, rev):
    rows = lax.broadcasted_iota(jnp.int32, (c, 1), 0)
    edge = (c - 1) if rev else 0

    def shifted(cur, nb):
        rolled = pltpu.roll(cur, (c - 1) if rev else 1, axis=0)
        return jnp.where(rows == edge, nb, rolled)

    rkv = x[:, :RKV_W]
    lo = x[:, RKV_W + 2 * LORA * d:RKV_W + 2 * LORA * (d + 1)]
    f = rkv + mu_rkv * (shifted(rkv, nb_row[:, :RKV_W]) - rkv)
    fl = lo + mu_lo * (shifted(lo, nb_row[:, RKV_W + 2 * LORA * d:RKV_W + 2 * LORA * (d + 1)]) - lo)
    r = f[:, :HALF]
    k = f[:, HALF:2 * HALF]
    v = f[:, 2 * HALF:]
    lane = lax.broadcasted_iota(jnp.int32, fl.shape, 1)
    lin = _dot(jnp.where(lane < LORA, jnp.tanh(fl), fl).astype(BF16), wwa)
    lw = (-math.exp(-0.5)) * _sigmoid(w0 + lin[:, :HALF])
    asig = _sigmoid(a0 + lin[:, HALF:])
    kk = k * k_k
    ss = _dot((kk * kk).astype(BF16), seg1)
    kkn = kk * lax.rsqrt(jnp.maximum(ss, 1e-24))
    kmod = k * (1.0 + (asig - 1.0) * k_a)
    bonus = _dot((r * kmod * r_k).astype(BF16), seg1) * v
    y_ref[:, HALF:] = bonus

    ti = lax.broadcasted_iota(jnp.int32, (c, c), 0)
    si = lax.broadcasted_iota(jnp.int32, (c, c), 1)
    incl = (si >= ti) if rev else (si <= ti)
    strict = (si > ti) if rev else (si < ti)
    tri = incl.astype(BF16)
    hi2, lo2 = _split2(lw)
    cum = _dot(tri, hi2) + _dot(tri, lo2)
    ref = cum[c // 2:c // 2 + 1]
    end = 0 if rev else c - 1
    cum_end = cum[end:end + 1]
    g = cum - ref
    e_pos = jnp.exp(g)
    e_neg = jnp.exp(-g)
    e_ref = jnp.exp(ref)
    e_tot = jnp.exp(cum_end)
    e_end = jnp.exp(cum_end - ref)
    at_c = -kkn * jnp.exp(g - lw)
    rt_c = r * e_pos
    bt = kkn * asig * e_neg
    kt = kmod * e_neg
    return dict(at_c=at_c, rt_c=rt_c, bt=bt, kt=kt, at_true=at_c * e_ref, rt_true=rt_c * e_ref,
                bh=bt * e_end, kh=kt * e_end, v=v, e_tot=e_tot, incl=incl, strict=strict)


def _wkv_chains(preps, c, s_ref, y_refs):
    lane = lax.broadcasted_iota(jnp.int32, (c, LANES), 1)
    lo = lane < HEAD
    hi = jnp.logical_not(lo)
    chains = [(d, p) for d in range(2) for p in range(N_HEADS // 2)]

    def slab(d, p, name):
        return preps[d][name][:, p * LANES:(p + 1) * LANES]

    def keep(mask, x):
        return jnp.where(mask, x, 0.0)

    n_pow, a_ak, m_all, vsw, x_cur = {}, {}, {}, {}, {}
    for ch in chains:
        d, p = ch
        at_c, rt_c = slab(d, p, "at_c"), slab(d, p, "rt_c")
        lhs = jnp.concatenate([keep(lo, at_c), keep(hi, at_c), keep(lo, rt_c), keep(hi, rt_c)], axis=0)
        rhs = jnp.concatenate([slab(d, p, "bt"), slab(d, p, "kt")], axis=0)
        g = _dot_nt(lhs.astype(BF16), rhs.astype(BF16))
        strict, incl = preps[d]["strict"], preps[d]["incl"]
        incl2 = jnp.concatenate([incl, incl], axis=1)
        n_pow[ch] = [keep(strict, g[h * c:(h + 1) * c, :c]).astype(BF16) for h in range(2)]
        a_ak[ch] = [keep(strict, g[h * c:(h + 1) * c, c:]).astype(BF16) for h in range(2)]
        m_all[ch] = [keep(incl2, g[(2 + h) * c:(3 + h) * c, :]).astype(BF16) for h in range(2)]
        v_sw = pltpu.roll(slab(d, p, "v"), HEAD, axis=1)
        vsw[ch] = [keep(hi, v_sw).astype(BF16), keep(lo, v_sw).astype(BF16)]
    for ch in chains:
        d, p = ch
        at_true = slab(d, p, "at_true")
        x_cur[ch] = [keep(lo, at_true) + _dot(a_ak[ch][0], vsw[ch][0]),
                     keep(hi, at_true) + _dot(a_ak[ch][1], vsw[ch][1])]
    steps = int(math.log2(c))
    for j in range(steps):
        for ch in chains:
            for h in range(2):
                xb = x_cur[ch][h].astype(BF16)
                if j + 1 < steps:
                    out = _dot(n_pow[ch][h], jnp.concatenate([xb, n_pow[ch][h]], axis=1))
                    x_cur[ch][h] = x_cur[ch][h] + out[:, :LANES]
                    n_pow[ch][h] = out[:, LANES:].astype(BF16)
                else:
                    x_cur[ch][h] = x_cur[ch][h] + _dot(n_pow[ch][h], xb)
    st, w, kb = {}, {}, {}
    for ch in chains:
        d, p = ch
        x0, x1 = x_cur[ch]
        st[ch] = _dot_nt(jnp.concatenate([x0, x1, slab(d, p, "rt_true")], axis=0).astype(BF16),
                         s_ref[d, p].astype(BF16))
        bh, kh = slab(d, p, "bh"), slab(d, p, "kh")
        kb[ch] = jnp.concatenate([keep(lo, bh), keep(lo, kh), keep(hi, bh), keep(hi, kh)], axis=0).astype(BF16)
    for ch in chains:
        x0, x1 = x_cur[ch]
        w[ch] = [jnp.concatenate([keep(hi, st[ch][:c] + x0).astype(BF16), vsw[ch][0]], axis=0),
                 jnp.concatenate([keep(lo, st[ch][c:2 * c] + x1).astype(BF16), vsw[ch][1]], axis=0)]
    for ch in chains:
        d, p = ch
        e_tot = preps[d]["e_tot"][:, p * LANES:(p + 1) * LANES]
        s_ref[d, p] = s_ref[d, p] * e_tot + _dot_tn(jnp.concatenate(w[ch], axis=0), kb[ch])
    for ch in chains:
        d, p = ch
        y_sw = st[ch][2 * c:] + _dot(m_all[ch][0], w[ch][0]) + _dot(m_all[ch][1], w[ch][1])
        y_refs[d][:, p * LANES:(p + 1) * LANES] = pltpu.roll(y_sw, HEAD, axis=1)


def _wkv_kernel(*refs, c, n_c, latent):
    if latent:
        (x0_ref, x1_ref, p0_ref, n1_ref, s0_ref, mu_ref, w0_ref, a0_ref, wwa_ref, kk_ref, ka_ref, rk_ref,
         seg_ref, y0_ref, y1_ref, sf_ref, s_ref) = refs
    else:
        (x0_ref, x1_ref, p0_ref, n1_ref, mu_ref, w0_ref, a0_ref, wwa_ref, kk_ref, ka_ref, rk_ref,
         seg_ref, y0_ref, y1_ref, sf_ref, s_ref) = refs
    i = pl.program_id(1)

    @pl.when(i == 0)
    def _():
        s_ref[...] = s0_ref[0] if latent else jnp.zeros(s_ref.shape, F32)

    inner = (i > 0).astype(F32)
    preps = []
    for d, (x_ref, nb_ref, y_ref) in enumerate(((x0_ref, p0_ref, y0_ref), (x1_ref, n1_ref, y1_ref))):
        nb = nb_ref[NB_ROWS - 1:NB_ROWS, :] if d == 0 else nb_ref[0:1, :]
        preps.append(_wkv_prep(x_ref[...].astype(F32), nb.astype(F32) * inner, d, c,
                               mu_ref[0, d, :, :RKV_W], mu_ref[0, d, :, RKV_W:], w0_ref[0, d], a0_ref[0, d],
                               wwa_ref[0, d], kk_ref[0], ka_ref[0], rk_ref[0], seg_ref[...], y_ref, rev=(d == 1)))
    _wkv_chains(preps, c, s_ref, (y0_ref, y1_ref))

    @pl.when(i == n_c - 1)
    def _():
        sf_ref[0] = s_ref[...]


def _wkv(seg_c, row_off, b, l, c, s0, layer, mu, w0, a0, wwa, k_k, k_a, r_k, seg1):
    t = b * l
    n_c = l // c
    t_all = seg_c.shape[0]
    cb = row_off // c
    c8 = c // NB_ROWS
    r8 = row_off // NB_ROWS
    last8 = t_all // NB_ROWS - 1
    latent = s0 is not None
    const2 = lambda bi, i: (0, 0)
    in_specs = [pl.BlockSpec((c, SEG_C), lambda bi, i: (cb + bi * n_c + i, 0)),
                pl.BlockSpec((c, SEG_C), lambda bi, i: (cb + bi * n_c + n_c - 1 - i, 0)),
                pl.BlockSpec((NB_ROWS, SEG_C),
                             lambda bi, i: (jnp.maximum(r8 + (bi * n_c + i) * c8 - 1, 0), 0)),
                pl.BlockSpec((NB_ROWS, SEG_C),
                             lambda bi, i: (jnp.minimum(r8 + (bi * n_c + n_c - i) * c8, last8), 0))]
    args = [seg_c, seg_c, seg_c, seg_c]
    if latent:
        in_specs.append(pl.BlockSpec((1, 2, N_HEADS // 2, LANES, LANES), lambda bi, i: (bi, 0, 0, 0, 0)))
        args.append(s0)
    per_layer = lambda p: pl.BlockSpec((1,) + p.shape[1:], lambda bi, i: (layer,) + (0,) * (p.ndim - 1))
    in_specs += [per_layer(p) for p in (mu, w0, a0, wwa, k_k, k_a, r_k)] + [pl.BlockSpec(seg1.shape, const2)]
    args += [mu, w0, a0, wwa, k_k, k_a, r_k, seg1]
    return pl.pallas_call(
        functools.partial(_wkv_kernel, c=c, n_c=n_c, latent=latent),
        grid=(b, n_c),
        in_specs=in_specs,
        out_specs=[pl.BlockSpec((c, 2 * HALF), lambda bi, i: (bi * n_c + i, 0)),
                   pl.BlockSpec((c, 2 * HALF), lambda bi, i: (bi * n_c + n_c - 1 - i, 0)),
                   pl.BlockSpec((1, 2, N_HEADS // 2, LANES, LANES), lambda bi, i: (bi, 0, 0, 0, 0))],
        out_shape=[jax.ShapeDtypeStruct((t, 2 * HALF), F32), jax.ShapeDtypeStruct((t, 2 * HALF), F32),
                   jax.ShapeDtypeStruct((b, 2, N_HEADS // 2, LANES, LANES), F32)],
        scratch_shapes=[pltpu.VMEM((2, N_HEADS // 2, LANES, LANES), F32)],
        compiler_params=_cparams("parallel", "arbitrary"),
        name="wkv",
    )(*args)


def _rwkv_post_kernel(y0_ref, y1_ref, gd_ref, segm_ref, g2_ref, lg_ref, lb_ref, o_ref):
    ys = y0_ref[:, :HALF] + y1_ref[:, :HALF]
    bonus = y0_ref[:, HALF:] + y1_ref[:, HALF:]
    mu = _dot(ys.astype(BF16), segm_ref[...])
    yc = ys - mu
    var = _dot((yc * yc).astype(BF16), segm_ref[...])
    gn = yc * lax.rsqrt(var + GN_EPS) * lg_ref[0] + lb_ref[0]
    gate = _dot(jax.nn.sigmoid(gd_ref[...].astype(F32)).astype(BF16), g2_ref[0])
    o_ref[...] = ((gn + bonus) * gate).astype(o_ref.dtype)


def _rwkv_post(y0, y1, seg_c, row_off, segm, g2, lnx_g, lnx_b, l, tm):
    t = y0.shape[0]
    off = row_off // tm
    in_specs = [pl.BlockSpec((tm, 2 * HALF), lambda i: (i, 0)),
                pl.BlockSpec((tm, 2 * HALF), lambda i: (i, 0)),
                pl.BlockSpec((tm, GATE_LORA), lambda i: (off + i, (SEG_C - GATE_LORA) // GATE_LORA)),
                pl.BlockSpec((HALF, HALF), lambda i: (0, 0)),
                pl.BlockSpec((1, GATE_LORA, HALF), lambda i: (l, 0, 0)),
                pl.BlockSpec((1, 1, HALF), lambda i: (l, 0, 0)),
                pl.BlockSpec((1, 1, HALF), lambda i: (l, 0, 0))]
    return pl.pallas_call(
        _rwkv_post_kernel,
        grid=(t // tm,),
        in_specs=in_specs,
        out_specs=pl.BlockSpec((tm, HALF), lambda i: (i, 0)),
        out_shape=jax.ShapeDtypeStruct((t, HALF), BF16),
        compiler_params=_cparams("parallel"),
        name="rwkv_post",
    )(y0, y1, seg_c, segm, g2, lnx_g, lnx_b)


def _merge_kernel(*refs, alpha, n_x, n_first):
    (sh_ref, sc_ref, g1_ref, oa_ref, ob0_ref, ob1_ref, oc0_ref, oc1_ref, wg_ref, wb_ref, wo_ref, lg_ref, lb_ref,
     o_ref) = refs[n_x:]
    x = _token_tile(refs[:n_x], n_first)
    h = (x * (1.0 + sc_ref[0]) + sh_ref[0]).astype(BF16)
    branches = (oa_ref[...], _token_tile((ob0_ref, ob1_ref), n_first), _token_tile((oc0_ref, oc1_ref), n_first))
    acc = None
    for j, br in enumerate(branches):
        gate = jax.nn.sigmoid(_dot(h, wg_ref[0, :, j * D_MODEL:(j + 1) * D_MODEL]))
        term = gate * _dot(br, wb_ref[0, j])
        acc = term if acc is None else acc + term
    mixed = _dot(acc.astype(BF16), wo_ref[0])
    o_ref[...] = _layer_norm(alpha * x + g1_ref[0] * mixed, lg_ref[0], lb_ref[0])


def _merge(xs, o_a, o_b, o_c, mod, w_g, w_branch, w_out, ln_g, ln_b, l, tm, t_ctx, l_lat, alpha):
    t = sum(x.shape[0] for x in xs)
    n_first = t_ctx // tm
    row = functools.partial(_mod_row, tm=tm, t_ctx=t_ctx, l_lat=l_lat)
    tok = lambda w: pl.BlockSpec((tm, w), lambda i: (i, 0))
    modspec = lambda blk: pl.BlockSpec((1, 1, D_MODEL), lambda i: (row(i), 0, blk))
    return pl.pallas_call(
        functools.partial(_merge_kernel, alpha=alpha, n_x=len(xs), n_first=n_first),
        grid=(t // tm,),
        in_specs=_token_specs(xs, tm, n_first) + [modspec(0), modspec(1), modspec(2), tok(HALF)]
                 + _token_specs(o_b, tm, n_first) + _token_specs(o_c, tm, n_first) + [
                  pl.BlockSpec((1, D_MODEL, SEG_G), lambda i: (l, 0, 0)),
                  pl.BlockSpec((1, 3, HALF, D_MODEL), lambda i: (l, 0, 0, 0)),
                  pl.BlockSpec((1, D_MODEL, D_MODEL), lambda i: (l, 0, 0)),
                  pl.BlockSpec((1, 1, D_MODEL), lambda i: (l, 0, 0)),
                  pl.BlockSpec((1, 1, D_MODEL), lambda i: (l, 0, 0))],
        out_specs=tok(D_MODEL),
        out_shape=jax.ShapeDtypeStruct((t, D_MODEL), F32),
        compiler_params=_cparams("parallel"),
        name="merge",
    )(*xs, mod, mod, mod, o_a, *o_b, *o_c, w_g, w_branch, w_out, ln_g, ln_b)


def _ffn_kernel(x_ref, sh_ref, sc_ref, g2_ref, wu_ref, wd_ref, lg_ref, lb_ref, *rest, alpha, n_f, n_first):
    *o_refs, h_ref, acc_ref = rest
    i = pl.program_id(0)
    j = pl.program_id(1)

    @pl.when(j == 0)
    def _():
        h_ref[...] = (x_ref[...] * (1.0 + sc_ref[0]) + sh_ref[0]).astype(BF16)
        acc_ref[...] = jnp.zeros(acc_ref.shape, F32)

    h = h_ref[...]
    n_g = wu_ref.shape[2] // FFN_GROUP
    cols = lambda g: slice(g * FFN_GROUP, (g + 1) * FFN_GROUP)
    ups, acts, downs = {}, {}, []
    for t in range(n_g + 2):
        if t < n_g:
            ups[t] = _dot(h, wu_ref[0, :, cols(t)])
        if 0 <= t - 1 < n_g:
            u = jnp.maximum(ups.pop(t - 1), 0.0)
            acts[t - 1] = (u * u).astype(BF16)
        if 0 <= t - 2 < n_g:
            downs.append(_dot(acts.pop(t - 2), wd_ref[0, cols(t - 2), :]))
    acc_ref[...] += functools.reduce(lambda a, b: a + b, downs)

    def finish(o_ref):
        o_ref[...] = _layer_norm(alpha * x_ref[...] + g2_ref[0] * acc_ref[...], lg_ref[0], lb_ref[0])

    last = j == n_f - 1
    if len(o_refs) == 1:
        pl.when(last)(lambda: finish(o_refs[0]))
    else:
        pl.when(jnp.logical_and(last, i < n_first))(lambda: finish(o_refs[0]))
        pl.when(jnp.logical_and(last, i >= n_first))(lambda: finish(o_refs[1]))


def _ffn(x, mod, w_up, w_down, ln_g, ln_b, l, tm, tf, t_ctx, l_lat, alpha, split=False):
    t = x.shape[0]
    n_f = D_FF // tf
    n_first = t_ctx // tm
    if split:
        out_specs = [pl.BlockSpec((tm, D_MODEL), lambda i, j: (jnp.minimum(i, n_first - 1), 0)),
                     pl.BlockSpec((tm, D_MODEL), lambda i, j: (jnp.maximum(i - n_first, 0), 0))]
        out_shape = [jax.ShapeDtypeStruct((t_ctx, D_MODEL), F32), jax.ShapeDtypeStruct((t - t_ctx, D_MODEL), F32)]
    else:
        out_specs = pl.BlockSpec((tm, D_MODEL), lambda i, j: (i, 0))
        out_shape = jax.ShapeDtypeStruct((t, D_MODEL), F32)
    row = functools.partial(_mod_row, tm=tm, t_ctx=t_ctx, l_lat=l_lat)
    modspec = lambda blk: pl.BlockSpec((1, 1, D_MODEL), lambda i, j: (row(i), 0, blk))
    return pl.pallas_call(
        functools.partial(_ffn_kernel, alpha=alpha, n_f=n_f, n_first=n_first),
        grid=(t // tm, n_f),
        in_specs=[pl.BlockSpec((tm, D_MODEL), lambda i, j: (i, 0)),
                  modspec(3), modspec(4), modspec(5),
                  pl.BlockSpec((1, D_MODEL, tf), lambda i, j: (l, 0, j)),
                  pl.BlockSpec((1, tf, D_MODEL), lambda i, j: (l, j, 0)),
                  pl.BlockSpec((1, 1, D_MODEL), lambda i, j: (l, 0, 0)),
                  pl.BlockSpec((1, 1, D_MODEL), lambda i, j: (l, 0, 0))],
        out_specs=out_specs,
        out_shape=out_shape,
        scratch_shapes=[pltpu.VMEM((tm, D_MODEL), BF16), pltpu.VMEM((tm, D_MODEL), F32)],
        compiler_params=_cparams("arbitrary", "arbitrary"),
        name="ffn",
    )(x, mod, mod, mod, w_up, w_down, ln_g, ln_b)


def _rope_tables(l):
    pos = jnp.arange(l, dtype=jnp.int32)
    row = (pos // GRID_W).astype(F32)
    col = (pos % GRID_W).astype(F32)
    half = HEAD // 2
    inv_freq = ROPE_THETA ** (-jnp.arange(0, half, 2, dtype=F32) / half)
    ang_r = row[:, None] * inv_freq[None, :]
    ang_c = col[:, None] * inv_freq[None, :]
    cos = jnp.concatenate([jnp.cos(ang_r), jnp.cos(ang_r), jnp.cos(ang_c), jnp.cos(ang_c)], axis=-1)
    sin = jnp.concatenate([-jnp.sin(ang_r), jnp.sin(ang_r), -jnp.sin(ang_c), jnp.sin(ang_c)], axis=-1)
    return jnp.tile(cos, (1, LANES // HEAD)), jnp.tile(sin, (1, LANES // HEAD))


def _head_block_matrix(width, value):
    idx = jnp.arange(width) // HEAD
    return jnp.where(idx[:, None] == idx[None, :], value, 0.0).astype(BF16)


def _pair_states(s):
    b = s.shape[0]
    s = s.reshape(b, 2, N_HEADS // 2, 2, HEAD, HEAD)
    z = jnp.zeros_like(s[:, :, :, 0])
    top = jnp.concatenate([z, s[:, :, :, 1]], axis=-1)
    bot = jnp.concatenate([s[:, :, :, 0], z], axis=-1)
    return jnp.concatenate([top, bot], axis=-2)


def _unpair_states(sp):
    b = sp.shape[0]
    s = jnp.stack([sp[..., HEAD:, :HEAD], sp[..., :HEAD, HEAD:]], axis=3)
    return s.reshape(b, 2, N_HEADS, HEAD, HEAD)


def _pick_tile(pref, *sizes):
    return min(pref, functools.reduce(math.gcd, sizes))


def kernel(x_prompt, x_sample, cache_k, cache_v, state_wkv, c, c_ctx, w_ada, b_ada, w_in, sgu_ln_g, sgu_ln_b, sgu_w, sgu_b, q_norm, k_norm, rwkv_mu, rwkv_w0, rwkv_w2, rwkv_a0, rwkv_a2, rwkv_k_k, rwkv_k_a, rwkv_r_k, rwkv_g2, rwkv_lnx_g, rwkv_lnx_b, w_branch, w_out, ln1_g, ln1_b, w_up, w_down, ln2_g, ln2_b):
    depth = w_in.shape[0]
    b_ctx, l_ctx, _ = x_prompt.shape
    b_lat, l_lat, _ = x_sample.shape
    past = cache_k.shape[2]
    t_ctx = b_ctx * l_ctx
    t_lat = b_lat * l_lat
    alpha = (2 * depth) ** 0.25

    tm = _pick_tile(512, t_ctx, l_lat)
    tm_ffn = _pick_tile(1024, t_ctx, l_lat)
    tk = _pick_tile(512, l_ctx, l_lat)
    tq_ctx = _pick_tile(256, l_ctx)
    tq_lat = _pick_tile(512, l_lat)
    c_ctx_chunk = _pick_tile(128, l_ctx)
    c_lat_chunk = _pick_tile(128, l_lat)

    n_rows = 1 + b_lat
    pad_rows = -n_rows % 16
    cvec = jnp.concatenate([c_ctx[None, :], c, jnp.zeros((pad_rows, D_MODEL), F32)], axis=0)
    mod_all = _ada(cvec, w_ada, b_ada[:, None, :])

    c_lo = SEG_A + SEG_B
    w_a_b = w_in[:, :, :SEG_A].astype(BF16)
    w_g_b = w_in[:, :, c_lo + SEG_C:].astype(BF16)
    w_bc_b = jnp.concatenate([w_in[:, :, c_lo:c_lo + SEG_C], jnp.zeros((depth, D_MODEL, B_OFF - SEG_C), F32),
                              w_in[:, :, SEG_A:c_lo]], axis=-1).astype(BF16)
    w_branch_b = w_branch.astype(BF16)
    w_out_b = w_out.astype(BF16)
    w_up_b = w_up.astype(BF16)
    w_down_b = w_down.astype(BF16)
    sgu_w_b = sgu_w.astype(BF16)
    g2_b = rwkv_g2.astype(BF16)
    seg_mean2 = _head_block_matrix(KV_W, 1.0 / HEAD)
    seg_mean8 = _head_block_matrix(HALF, 1.0 / HEAD)
    seg_ones8 = _head_block_matrix(HALF, 1.0)
    cos_ctx, sin_ctx = _rope_tables(l_ctx)
    cos_lat, sin_lat = _rope_tables(l_lat)
    zeros_lora = jnp.zeros((depth, 2, LORA, HALF), F32)
    wwa = jnp.concatenate([jnp.concatenate([rwkv_w2, zeros_lora], axis=-1),
                           jnp.concatenate([zeros_lora, rwkv_a2], axis=-1)], axis=-2).astype(BF16)

    xs = (x_prompt.reshape(t_ctx, D_MODEL), x_sample.reshape(t_lat, D_MODEL))
    b_s_full = jnp.repeat(jnp.swapaxes(sgu_b, 1, 2), HALF // SGU_GROUPS, axis=2)
    vec = lambda p: p[:, None, :]
    new_k, new_v, new_s = [], [], []
    for l in range(depth):
        mod = mod_all[l][:, None, :]
        o_a, seg_b = _proj(xs, mod, w_a_b, w_bc_b, vec(sgu_ln_g), vec(sgu_ln_b), sgu_w_b, b_s_full, l, tm, t_ctx, l_lat)
        seg_c = seg_b

        qn2 = jnp.tile(q_norm[l], LANES // HEAD)[None]
        kn2 = jnp.tile(k_norm[l], KV_W // HEAD)[None]
        kn_ctx, kr_ctx, vb_ctx = _kvprep(seg_b, 0, b_ctx, l_ctx, kn2, seg_mean2, cos_ctx, sin_ctx, False, tk)
        kr_lat, vb_lat = _kvprep(seg_b, t_ctx, b_lat, l_lat, kn2, seg_mean2, cos_lat, sin_lat, True, tk)
        new_k.append(kn_ctx.reshape(b_ctx, l_ctx, N_KV, HEAD))
        new_v.append(seg_b[:t_ctx, B_OFF + HALF + KV_W:].astype(F32).reshape(b_ctx, l_ctx, N_KV, HEAD))
        kt_lat = jnp.concatenate([jnp.swapaxes(cache_k[:, l].reshape(b_lat, past, KV_W), 1, 2).astype(BF16),
                                  kr_lat], axis=2)
        v_lat = jnp.concatenate([cache_v[:, l].reshape(b_lat, past, KV_W).astype(BF16),
                                 vb_lat.reshape(b_lat, l_lat, KV_W)], axis=1)
        o_b = (_attention(seg_b, 0, b_ctx, l_ctx, kr_ctx, vb_ctx.reshape(b_ctx, l_ctx, KV_W),
                          qn2, seg_mean2, cos_ctx, sin_ctx, False, tq_ctx),
               _attention(seg_b, t_ctx, b_lat, l_lat, kt_lat, v_lat, qn2, seg_mean2, cos_lat, sin_lat, True, tq_lat))

        rw = dict(layer=l, mu=rwkv_mu[:, :, None, :], w0=rwkv_w0[:, :, None, :], a0=rwkv_a0[:, :, None, :], wwa=wwa,
                  k_k=vec(rwkv_k_k), k_a=vec(rwkv_k_a), r_k=rwkv_r_k.reshape(depth, 1, HALF), seg1=seg_ones8)
        y0c, y1c, s_ctx = _wkv(seg_c, 0, b_ctx, l_ctx, c_ctx_chunk, None, **rw)
        y0l, y1l, _ = _wkv(seg_c, t_ctx, b_lat, l_lat, c_lat_chunk, _pair_states(state_wkv[:, l]), **rw)
        new_s.append(_unpair_states(s_ctx))
        post = functools.partial(_rwkv_post, segm=seg_mean8, g2=g2_b, lnx_g=vec(rwkv_lnx_g), lnx_b=vec(rwkv_lnx_b),
                                 l=l, tm=tm)
        o_c = (post(y0c, y1c, seg_c, 0), post(y0l, y1l, seg_c, t_ctx))

        x = _merge(xs, o_a, o_b, o_c, mod, w_g_b, w_branch_b, w_out_b, vec(ln1_g), vec(ln1_b), l, tm, t_ctx, l_lat, alpha)
        x = _ffn(x, mod, w_up_b, w_down_b, vec(ln2_g), vec(ln2_b), l, tm_ffn, 1024, t_ctx, l_lat, alpha,
                 split=(l == depth - 1))
        xs = (x,)

    y = x[0].reshape(b_ctx, l_ctx, D_MODEL)
    z = x[1].reshape(b_lat, l_lat, D_MODEL)
    return (y, z, jnp.stack(new_k, axis=1), jnp.stack(new_v, axis=1), jnp.stack(new_s, axis=1))
```

```python
import functools
import math

import jax
import jax.numpy as jnp
from jax import lax
from jax.experimental import pallas as pl
from jax.experimental.pallas import tpu as pltpu

F32 = jnp.float32
BF16 = jnp.bfloat16

D_MODEL = 1024
HALF = D_MODEL // 2
HEAD = 64
N_HEADS = HALF // HEAD
N_KV = 2
GQA = N_HEADS // N_KV
KV_W = N_KV * HEAD
GRID_W = 64
SGU_CHUNK = 128
SGU_GROUPS = 4
LORA = 64
GATE_LORA = 128
D_FF = 4 * D_MODEL
ROPE_THETA = 10000.0
GN_EPS = 64e-5
SEG_A = 2 * HALF
SEG_B = HALF + 2 * KV_W
SEG_C = 3 * HALF + 4 * LORA + GATE_LORA
SEG_G = 3 * D_MODEL
RKV_W = 3 * HALF
B_OFF = 2048
SEG_BC = B_OFF + SEG_B
FFN_GROUP = 512
LANES = 128
NB_ROWS = 16
VMEM_LIMIT = 48 * 1024 * 1024


def _cparams(*sem):
    return pltpu.CompilerParams(dimension_semantics=sem, vmem_limit_bytes=VMEM_LIMIT)


def _dot(a, b):
    return jnp.dot(a, b, preferred_element_type=F32)


def _dot_nt(a, b):
    return lax.dot_general(a, b, (((1,), (1,)), ((), ())), preferred_element_type=F32)


def _dot_tn(a, b):
    return lax.dot_general(a, b, (((0,), (0,)), ((), ())), preferred_element_type=F32)


def _layer_norm(x, g, b, eps=1e-5):
    mu = jnp.mean(x, axis=-1, keepdims=True)
    xc = x - mu
    var = jnp.mean(xc * xc, axis=-1, keepdims=True)
    return xc * lax.rsqrt(var + eps) * g + b


def _sigmoid(x):
    return 0.5 * jnp.tanh(0.5 * x) + 0.5


def _ada_kernel(c_ref, w_ref, b_ref, o_ref):
    c = c_ref[...]
    s = (c * jax.nn.sigmoid(c)).astype(BF16)
    o_ref[0] = _dot(s, w_ref[0].astype(BF16)) + b_ref[0]


def _ada(cvec, w_ada, b_ada):
    depth, _, n = w_ada.shape
    r = cvec.shape[0]
    tn = 1536
    return pl.pallas_call(
        _ada_kernel,
        grid=(depth, n // tn),
        in_specs=[pl.BlockSpec((r, D_MODEL), lambda l, j: (0, 0)),
                  pl.BlockSpec((1, D_MODEL, tn), lambda l, j: (l, 0, j)),
                  pl.BlockSpec((1, 1, tn), lambda l, j: (l, 0, j))],
        out_specs=pl.BlockSpec((1, r, tn), lambda l, j: (l, 0, j)),
        out_shape=jax.ShapeDtypeStruct((depth, r, n), F32),
        compiler_params=_cparams("parallel", "parallel"),
        name="ada",
    )(cvec, w_ada, b_ada)


def _token_tile(x_refs, n_first):
    if len(x_refs) == 1:
        return x_refs[0][...]
    return jnp.where(pl.program_id(0) < n_first, x_refs[0][...], x_refs[1][...])


def _token_specs(xs, tm, n_first):
    width = xs[0].shape[1]
    if len(xs) == 1:
        return [pl.BlockSpec((tm, width), lambda i: (i, 0))]
    return [pl.BlockSpec((tm, width), lambda i: (jnp.minimum(i, n_first - 1), 0)),
            pl.BlockSpec((tm, width), lambda i: (jnp.maximum(i - n_first, 0), 0))]


def _mod_row(i, tm, t_ctx, l_lat):
    r = i * tm
    return jnp.where(r < t_ctx, 0, 1 + (r - t_ctx) // l_lat)


def _proj_kernel(*refs, tm, n_x, n_first):
    sh_ref, sc_ref, wa_ref, wbc_ref, g_ref, b_ref, ws_ref, bs_ref, oa_ref, obc_ref = refs[n_x:]
    h = (_token_tile(refs[:n_x], n_first) * (1.0 + sc_ref[0]) + sh_ref[0]).astype(BF16)
    n_bc = wbc_ref.shape[2]
    for j in range(2):
        cols = slice(j * (n_bc // 2), (j + 1) * (n_bc // 2))
        obc_ref[:, cols] = _dot(h, wbc_ref[0, :, cols]).astype(obc_ref.dtype)
    uv = _dot(h, wa_ref[0])
    vn = _layer_norm(uv[:, HALF:], g_ref[0], b_ref[0]).astype(BF16)
    gc = HALF // SGU_GROUPS
    for n in range(tm // SGU_CHUNK):
        rows = slice(n * SGU_CHUNK, (n + 1) * SGU_CHUNK)
        for g in range(SGU_GROUPS):
            cols = slice(g * gc, (g + 1) * gc)
            s = _dot(ws_ref[0, g], vn[rows, cols]) + bs_ref[0, :, cols]
            oa_ref[rows, cols] = (uv[rows, cols] * s).astype(oa_ref.dtype)


def _proj(xs, mod, w_a, w_bc, ln_g, ln_b, w_s, b_s_full, l, tm, t_ctx, l_lat):
    t = sum(x.shape[0] for x in xs)
    n_first = t_ctx // tm
    row = functools.partial(_mod_row, tm=tm, t_ctx=t_ctx, l_lat=l_lat)
    return pl.pallas_call(
        functools.partial(_proj_kernel, tm=tm, n_x=len(xs), n_first=n_first),
        grid=(t // tm,),
        in_specs=_token_specs(xs, tm, n_first) + [
                  pl.BlockSpec((1, 1, D_MODEL), lambda i: (row(i), 0, 0)),
                  pl.BlockSpec((1, 1, D_MODEL), lambda i: (row(i), 0, 1)),
                  pl.BlockSpec((1, D_MODEL, SEG_A), lambda i: (l, 0, 0)),
                  pl.BlockSpec((1, D_MODEL, SEG_BC), lambda i: (l, 0, 0)),
                  pl.BlockSpec((1, 1, HALF), lambda i: (l, 0, 0)),
                  pl.BlockSpec((1, 1, HALF), lambda i: (l, 0, 0)),
                  pl.BlockSpec((1, SGU_GROUPS, SGU_CHUNK, SGU_CHUNK), lambda i: (l, 0, 0, 0)),
                  pl.BlockSpec((1, SGU_CHUNK, HALF), lambda i: (l, 0, 0))],
        out_specs=[pl.BlockSpec((tm, HALF), lambda i: (i, 0)),
                   pl.BlockSpec((tm, SEG_BC), lambda i: (i, 0))],
        out_shape=[jax.ShapeDtypeStruct((t, HALF), BF16), jax.ShapeDtypeStruct((t, SEG_BC), BF16)],
        compiler_params=_cparams("parallel"),
        name="proj",
    )(*xs, mod, mod, w_a, w_bc, ln_g, ln_b, w_s, b_s_full)


def _rope_swap(x):
    lane = lax.broadcasted_iota(jnp.int32, x.shape, 1)
    up = pltpu.roll(x, LANES - 16, axis=1)
    dn = pltpu.roll(x, 16, axis=1)
    return jnp.where((lane & 16) == 0, up, dn)


def _head_rms(x, seg_ref, g):
    ms = _dot((x * x).astype(BF16), seg_ref[...])
    return x * lax.rsqrt(ms + 1e-6) * g


def _kvprep_kernel(k_ref, v_ref, g_ref, seg_ref, cos_ref, sin_ref, *out_refs, rope):
    kn = _head_rms(k_ref[...].astype(F32), seg_ref, g_ref[...])
    if rope:
        kt_ref, vb_ref = out_refs
        kn = kn * cos_ref[...] + _rope_swap(kn) * sin_ref[...]
    else:
        kn_ref, kt_ref, vb_ref = out_refs
        kn_ref[...] = kn
    kt_ref[0] = kn.T.astype(BF16)
    vb_ref[...] = v_ref[...].astype(BF16)


def _kvprep(seg_b, row_off, b, l, k_norm2, seg_mat, cos, sin, rope, tk):
    t = b * l
    off = row_off // tk
    lb = l // tk
    out_shape = [jax.ShapeDtypeStruct((b, KV_W, l), BF16), jax.ShapeDtypeStruct((t, KV_W), BF16)]
    out_specs = [pl.BlockSpec((1, KV_W, tk), lambda i: (i // lb, 0, i % lb)),
                 pl.BlockSpec((tk, KV_W), lambda i: (i, 0))]
    if not rope:
        out_shape = [jax.ShapeDtypeStruct((t, KV_W), F32)] + out_shape
        out_specs = [pl.BlockSpec((tk, KV_W), lambda i: (i, 0))] + out_specs
    return pl.pallas_call(
        functools.partial(_kvprep_kernel, rope=rope),
        grid=(t // tk,),
        in_specs=[pl.BlockSpec((tk, KV_W), lambda i: (off + i, (B_OFF + HALF) // KV_W)),
                  pl.BlockSpec((tk, KV_W), lambda i: (off + i, (B_OFF + HALF) // KV_W + 1)),
                  pl.BlockSpec((1, KV_W), lambda i: (0, 0)),
                  pl.BlockSpec((KV_W, KV_W), lambda i: (0, 0)),
                  pl.BlockSpec((tk, KV_W), lambda i: (i % lb, 0)),
                  pl.BlockSpec((tk, KV_W), lambda i: (i % lb, 0))],
        out_specs=out_specs,
        out_shape=out_shape,
        compiler_params=_cparams("parallel"),
        name="kvprep",
    )(seg_b, seg_b, k_norm2, seg_mat, cos, sin)


def _attn_kernel(q_ref, g_ref, seg_ref, cos_ref, sin_ref, kt_ref, v_ref, o_ref, *, rope, tq):
    qs = []
    for s in range(HALF // LANES):
        q = _head_rms(q_ref[:, s * LANES:(s + 1) * LANES].astype(F32), seg_ref, g_ref[...])
        if rope:
            q = q * cos_ref[...] + _rope_swap(q) * sin_ref[...]
        qs.append((q * (HEAD ** -0.5 * math.log2(math.e))).astype(BF16))
    lo = lax.broadcasted_iota(jnp.int32, (tq, LANES), 1) < HEAD
    scores, probs, ratios = {}, {}, {}

    kt = kt_ref[0]
    zeros = jnp.zeros((HEAD, kt.shape[1]), BF16)
    k_slab = {}
    for g in range(N_KV):
        kg = kt[g * HEAD:(g + 1) * HEAD]
        k_slab[g, 0] = jnp.concatenate([kg, zeros], axis=0)
        k_slab[g, 1] = jnp.concatenate([zeros, kg], axis=0)
    lane_row = lax.broadcasted_iota(jnp.int32, (1, LANES), 1)
    own = [(lane_row < HEAD).astype(F32).astype(BF16), (lane_row >= HEAD).astype(F32).astype(BF16)]
    v_slab = [v_ref[0] * own[g] + own[1 - g] for g in range(N_KV)]

    def qk(h):
        scores[h] = _dot(qs[h // 2], k_slab[h // GQA, h % 2])

    def softmax(h):
        s = scores.pop(h)
        probs[h] = jnp.exp2(s - jnp.max(s, axis=-1, keepdims=True)).astype(BF16)

    def pv(h):
        g = h // GQA
        oe = _dot(probs.pop(h), v_slab[g])
        sw = pltpu.roll(oe, HEAD, axis=1)
        ratios[h] = oe / sw if (h % 2 == 0) == (g == 0) else sw / oe
        if h % 2 == 1:
            pair = jnp.where(lo, ratios.pop(h - 1), ratios.pop(h))
            o_ref[:, (h // 2) * LANES:(h // 2 + 1) * LANES] = pair.astype(o_ref.dtype)

    for t in range(N_HEADS + 2):
        if t < N_HEADS:
            qk(t)
        if 0 <= t - 1 < N_HEADS:
            softmax(t - 1)
        if 0 <= t - 2 < N_HEADS:
            pv(t - 2)


def _attention(seg_b, row_off, b, l, kt, v, q_norm2, seg_mat, cos, sin, rope, tq):
    off = row_off // tq
    lb = l // tq
    lk = kt.shape[-1]
    in_specs = [pl.BlockSpec((tq, HALF), lambda bi, i: (off + bi * lb + i, B_OFF // HALF)),
                pl.BlockSpec((1, LANES), lambda bi, i: (0, 0)),
                pl.BlockSpec((KV_W, KV_W), lambda bi, i: (0, 0)),
                pl.BlockSpec((tq, LANES), lambda bi, i: (i, 0)),
                pl.BlockSpec((tq, LANES), lambda bi, i: (i, 0)),
                pl.BlockSpec((1, KV_W, lk), lambda bi, i: (bi, 0, 0)),
                pl.BlockSpec((1, lk, KV_W), lambda bi, i: (bi, 0, 0))]
    return pl.pallas_call(
        functools.partial(_attn_kernel, rope=rope, tq=tq),
        grid=(b, lb),
        in_specs=in_specs,
        out_specs=pl.BlockSpec((tq, HALF), lambda bi, i: (bi * lb + i, 0)),
        out_shape=jax.ShapeDtypeStruct((b * l, HALF), BF16),
        compiler_params=_cparams("parallel", "parallel"),
        name="attention",
    )(seg_b, q_norm2, seg_mat, cos, sin, kt, v)


def _split2(x):
    h = x.astype(BF16)
    return h, (x - h.astype(F32)).astype(BF16)


def _wkv_prep(x, nb_row, d, c, mu_rkv, mu_lo, w0, a0, wwa, k_k, k_a, r_k, seg1, y_ref, rev):
    rows = lax.broadcasted_iota(jnp.int32, (c, 1), 0)
    edge = (c - 1) if rev else 0

    def shifted(cur, nb):
        rolled = pltpu.roll(cur, (c - 1) if rev else 1, axis=0)
        return jnp.where(rows == edge, nb, rolled)

    rkv = x[:, :RKV_W]
    lo = x[:, RKV_W + 2 * LORA * d:RKV_W + 2 * LORA * (d + 1)]
    f = rkv + mu_rkv * (shifted(rkv, nb_row[:, :RKV_W]) - rkv)
    fl = lo + mu_lo * (shifted(lo, nb_row[:, RKV_W + 2 * LORA * d:RKV_W + 2 * LORA * (d + 1)]) - lo)
    r = f[:, :HALF]
    k = f[:, HALF:2 * HALF]
    v = f[:, 2 * HALF:]
    lane = lax.broadcasted_iota(jnp.int32, fl.shape, 1)
    lin = _dot(jnp.where(lane < LORA, jnp.tanh(fl), fl).astype(BF16), wwa)
    lw = (-math.exp(-0.5)) * _sigmoid(w0 + lin[:, :HALF])
    asig = _sigmoid(a0 + lin[:, HALF:])
    kk = k * k_k
    ss = _dot((kk * kk).astype(BF16), seg1)
    kkn = kk * lax.rsqrt(jnp.maximum(ss, 1e-24))
    kmod = k * (1.0 + (asig - 1.0) * k_a)
    bonus = _dot((r * kmod * r_k).astype(BF16), seg1) * v
    y_ref[:, HALF:] = bonus.astype(y_ref.dtype)

    ti = lax.broadcasted_iota(jnp.int32, (c, c), 0)
    si = lax.broadcasted_iota(jnp.int32, (c, c), 1)
    incl = (si >= ti) if rev else (si <= ti)
    strict = (si > ti) if rev else (si < ti)
    tri = incl.astype(BF16)
    hi2, lo2 = _split2(lw)
    cum = _dot(tri, hi2) + _dot(tri, lo2)
    ref = cum[c // 2:c // 2 + 1]
    end = 0 if rev else c - 1
    cum_end = cum[end:end + 1]
    g = cum - ref
    e_pos = jnp.exp(g)
    e_neg = jnp.exp(-g)
    e_ref = jnp.exp(ref)
    e_tot = jnp.exp(cum_end)
    e_end = jnp.exp(cum_end - ref)
    at_c = -kkn * jnp.exp(g - lw)
    rt_c = r * e_pos
    bt = kkn * asig * e_neg
    kt = kmod * e_neg
    return dict(at_c=at_c, rt_c=rt_c, bt=bt, kt=kt, at_true=at_c * e_ref, rt_true=rt_c * e_ref,
                bh=bt * e_end, kh=kt * e_end, v=v, e_tot=e_tot, incl=incl, strict=strict)


def _wkv_chains(preps, c, s_ref, y_refs):
    lane = lax.broadcasted_iota(jnp.int32, (c, LANES), 1)
    lo = lane < HEAD
    hi = jnp.logical_not(lo)
    chains = [(d, p) for d in range(2) for p in range(N_HEADS // 2)]

    def slab(d, p, name):
        return preps[d][name][:, p * LANES:(p + 1) * LANES]

    def keep(mask, x):
        return jnp.where(mask, x, 0.0)

    n_pow, a_ak, m_all, vsw, x_cur = {}, {}, {}, {}, {}
    for ch in chains:
        d, p = ch
        at_c, rt_c = slab(d, p, "at_c"), slab(d, p, "rt_c")
        lhs = jnp.concatenate([keep(lo, at_c), keep(hi, at_c), keep(lo, rt_c), keep(hi, rt_c)], axis=0)
        rhs = jnp.concatenate([slab(d, p, "bt"), slab(d, p, "kt")], axis=0)
        g = _dot_nt(lhs.astype(BF16), rhs.astype(BF16))
        strict, incl = preps[d]["strict"], preps[d]["incl"]
        incl2 = jnp.concatenate([incl, incl], axis=1)
        n_pow[ch] = [keep(strict, g[h * c:(h + 1) * c, :c]).astype(BF16) for h in range(2)]
        a_ak[ch] = [keep(strict, g[h * c:(h + 1) * c, c:]).astype(BF16) for h in range(2)]
        m_all[ch] = [keep(incl2, g[(2 + h) * c:(3 + h) * c, :]).astype(BF16) for h in range(2)]
        v_sw = pltpu.roll(slab(d, p, "v"), HEAD, axis=1)
        vsw[ch] = [keep(hi, v_sw).astype(BF16), keep(lo, v_sw).astype(BF16)]
    for ch in chains:
        d, p = ch
        at_true = slab(d, p, "at_true")
        x_cur[ch] = [keep(lo, at_true) + _dot(a_ak[ch][0], vsw[ch][0]),
                     keep(hi, at_true) + _dot(a_ak[ch][1], vsw[ch][1])]
    steps = int(math.log2(c))
    for j in range(steps):
        for ch in chains:
            for h in range(2):
                xb = x_cur[ch][h].astype(BF16)
                if j + 1 < steps:
                    out = _dot(n_pow[ch][h], jnp.concatenate([xb, n_pow[ch][h]], axis=1))
                    x_cur[ch][h] = x_cur[ch][h] + out[:, :LANES]
                    n_pow[ch][h] = out[:, LANES:].astype(BF16)
                else:
                    x_cur[ch][h] = x_cur[ch][h] + _dot(n_pow[ch][h], xb)
    st, w, kb = {}, {}, {}
    for ch in chains:
        d, p = ch
        x0, x1 = x_cur[ch]
        st[ch] = _dot_nt(jnp.concatenate([x0, x1, slab(d, p, "rt_true")], axis=0).astype(BF16),
                         s_ref[d, p].astype(BF16))
        bh, kh = slab(d, p, "bh"), slab(d, p, "kh")
        kb[ch] = jnp.concatenate([keep(lo, bh), keep(lo, kh), keep(hi, bh), keep(hi, kh)], axis=0).astype(BF16)
    for ch in chains:
        x0, x1 = x_cur[ch]
        w[ch] = [jnp.concatenate([keep(hi, st[ch][:c] + x0).astype(BF16), vsw[ch][0]], axis=0),
                 jnp.concatenate([keep(lo, st[ch][c:2 * c] + x1).astype(BF16), vsw[ch][1]], axis=0)]
    for ch in chains:
        d, p = ch
        e_tot = preps[d]["e_tot"][:, p * LANES:(p + 1) * LANES]
        s_ref[d, p] = s_ref[d, p] * e_tot + _dot_tn(jnp.concatenate(w[ch], axis=0), kb[ch])
    for ch in chains:
        d, p = ch
        y_sw = st[ch][2 * c:] + _dot(m_all[ch][0], w[ch][0]) + _dot(m_all[ch][1], w[ch][1])
        y_refs[d][:, p * LANES:(p + 1) * LANES] = pltpu.roll(y_sw, HEAD, axis=1).astype(y_refs[d].dtype)


def _wkv_kernel(*refs, c, n_c, latent):
    if latent:
        (x0_ref, x1_ref, p0_ref, n1_ref, s0_ref, mu_ref, w0_ref, a0_ref, wwa_ref, kk_ref, ka_ref, rk_ref,
         seg_ref, y0_ref, y1_ref, sf_ref, s_ref) = refs
    else:
        (x0_ref, x1_ref, p0_ref, n1_ref, mu_ref, w0_ref, a0_ref, wwa_ref, kk_ref, ka_ref, rk_ref,
         seg_ref, y0_ref, y1_ref, sf_ref, s_ref) = refs
    i = pl.program_id(1)

    @pl.when(i == 0)
    def _():
        s_ref[...] = s0_ref[0] if latent else jnp.zeros(s_ref.shape, F32)

    inner = (i > 0).astype(F32)
    preps = []
    for d, (x_ref, nb_ref, y_ref) in enumerate(((x0_ref, p0_ref, y0_ref), (x1_ref, n1_ref, y1_ref))):
        nb = nb_ref[NB_ROWS - 1:NB_ROWS, :] if d == 0 else nb_ref[0:1, :]
        preps.append(_wkv_prep(x_ref[...].astype(F32), nb.astype(F32) * inner, d, c,
                               mu_ref[0, d, :, :RKV_W], mu_ref[0, d, :, RKV_W:], w0_ref[0, d], a0_ref[0, d],
                               wwa_ref[0, d], kk_ref[0], ka_ref[0], rk_ref[0], seg_ref[...], y_ref, rev=(d == 1)))
    _wkv_chains(preps, c, s_ref, (y0_ref, y1_ref))

    @pl.when(i == n_c - 1)
    def _():
        sf_ref[0] = s_ref[...]


def _wkv(seg_c, row_off, b, l, c, s0, layer, mu, w0, a0, wwa, k_k, k_a, r_k, seg1):
    t = b * l
    n_c = l // c
    t_all = seg_c.shape[0]
    cb = row_off // c
    c8 = c // NB_ROWS
    r8 = row_off // NB_ROWS
    last8 = t_all // NB_ROWS - 1
    latent = s0 is not None
    const2 = lambda bi, i: (0, 0)
    in_specs = [pl.BlockSpec((c, SEG_C), lambda bi, i: (cb + bi * n_c + i, 0)),
                pl.BlockSpec((c, SEG_C), lambda bi, i: (cb + bi * n_c + n_c - 1 - i, 0)),
                pl.BlockSpec((NB_ROWS, SEG_C),
                             lambda bi, i: (jnp.maximum(r8 + (bi * n_c + i) * c8 - 1, 0), 0)),
                pl.BlockSpec((NB_ROWS, SEG_C),
                             lambda bi, i: (jnp.minimum(r8 + (bi * n_c + n_c - i) * c8, last8), 0))]
    args = [seg_c, seg_c, seg_c, seg_c]
    if latent:
        in_specs.append(pl.BlockSpec((1, 2, N_HEADS // 2, LANES, LANES), lambda bi, i: (bi, 0, 0, 0, 0)))
        args.append(s0)
    per_layer = lambda p: pl.BlockSpec((1,) + p.shape[1:], lambda bi, i: (layer,) + (0,) * (p.ndim - 1))
    in_specs += [per_layer(p) for p in (mu, w0, a0, wwa, k_k, k_a, r_k)] + [pl.BlockSpec(seg1.shape, const2)]
    args += [mu, w0, a0, wwa, k_k, k_a, r_k, seg1]
    return pl.pallas_call(
        functools.partial(_wkv_kernel, c=c, n_c=n_c, latent=latent),
        grid=(b, n_c),
        in_specs=in_specs,
        out_specs=[pl.BlockSpec((c, 2 * HALF), lambda bi, i: (bi * n_c + i, 0)),
                   pl.BlockSpec((c, 2 * HALF), lambda bi, i: (bi * n_c + n_c - 1 - i, 0)),
                   pl.BlockSpec((1, 2, N_HEADS // 2, LANES, LANES), lambda bi, i: (bi, 0, 0, 0, 0))],
        out_shape=[jax.ShapeDtypeStruct((t, 2 * HALF), BF16), jax.ShapeDtypeStruct((t, 2 * HALF), BF16),
                   jax.ShapeDtypeStruct((b, 2, N_HEADS // 2, LANES, LANES), F32)],
        scratch_shapes=[pltpu.VMEM((2, N_HEADS // 2, LANES, LANES), F32)],
        compiler_params=_cparams("parallel", "arbitrary"),
        name="wkv",
    )(*args)


def _rwkv_post_kernel(y0_ref, y1_ref, gd_ref, segm_ref, g2_ref, lg_ref, lb_ref, o_ref):
    ys = y0_ref[:, :HALF].astype(F32) + y1_ref[:, :HALF].astype(F32)
    bonus = y0_ref[:, HALF:].astype(F32) + y1_ref[:, HALF:].astype(F32)
    mu = _dot(ys.astype(BF16), segm_ref[...])
    yc = ys - mu
    var = _dot((yc * yc).astype(BF16), segm_ref[...])
    gn = yc * lax.rsqrt(var + GN_EPS) * lg_ref[0] + lb_ref[0]
    gate = _dot(_sigmoid(gd_ref[...].astype(F32)).astype(BF16), g2_ref[0])
    o_ref[...] = ((gn + bonus) * gate).astype(o_ref.dtype)


def _rwkv_post(y0, y1, seg_c, row_off, segm, g2, lnx_g, lnx_b, l, tm):
    t = y0.shape[0]
    off = row_off // tm
    in_specs = [pl.BlockSpec((tm, 2 * HALF), lambda i: (i, 0)),
                pl.BlockSpec((tm, 2 * HALF), lambda i: (i, 0)),
                pl.BlockSpec((tm, GATE_LORA), lambda i: (off + i, (SEG_C - GATE_LORA) // GATE_LORA)),
                pl.BlockSpec((HALF, HALF), lambda i: (0, 0)),
                pl.BlockSpec((1, GATE_LORA, HALF), lambda i: (l, 0, 0)),
                pl.BlockSpec((1, 1, HALF), lambda i: (l, 0, 0)),
                pl.BlockSpec((1, 1, HALF), lambda i: (l, 0, 0))]
    return pl.pallas_call(
        _rwkv_post_kernel,
        grid=(t // tm,),
        in_specs=in_specs,
        out_specs=pl.BlockSpec((tm, HALF), lambda i: (i, 0)),
        out_shape=jax.ShapeDtypeStruct((t, HALF), BF16),
        compiler_params=_cparams("parallel"),
        name="rwkv_post",
    )(y0, y1, seg_c, segm, g2, lnx_g, lnx_b)


def _merge_kernel(*refs, alpha, n_x, n_first):
    (sh_ref, sc_ref, g1_ref, oa_ref, ob0_ref, ob1_ref, oc0_ref, oc1_ref, wg_ref, wb_ref, wo_ref, lg_ref, lb_ref,
     o_ref) = refs[n_x:]
    x = _token_tile(refs[:n_x], n_first)
    h = (x * (1.0 + sc_ref[0]) + sh_ref[0]).astype(BF16)
    branches = (oa_ref[...], _token_tile((ob0_ref, ob1_ref), n_first), _token_tile((oc0_ref, oc1_ref), n_first))
    acc = None
    for j, br in enumerate(branches):
        gate = _sigmoid(_dot(h, wg_ref[0, :, j * D_MODEL:(j + 1) * D_MODEL]))
        term = gate * _dot(br, wb_ref[0, j])
        acc = term if acc is None else acc + term
    mixed = _dot(acc.astype(BF16), wo_ref[0])
    o_ref[...] = _layer_norm(alpha * x + g1_ref[0] * mixed, lg_ref[0], lb_ref[0])


def _merge(xs, o_a, o_b, o_c, mod, w_g, w_branch, w_out, ln_g, ln_b, l, tm, t_ctx, l_lat, alpha):
    t = sum(x.shape[0] for x in xs)
    n_first = t_ctx // tm
    row = functools.partial(_mod_row, tm=tm, t_ctx=t_ctx, l_lat=l_lat)
    tok = lambda w: pl.BlockSpec((tm, w), lambda i: (i, 0))
    modspec = lambda blk: pl.BlockSpec((1, 1, D_MODEL), lambda i: (row(i), 0, blk))
    return pl.pallas_call(
        functools.partial(_merge_kernel, alpha=alpha, n_x=len(xs), n_first=n_first),
        grid=(t // tm,),
        in_specs=_token_specs(xs, tm, n_first) + [modspec(0), modspec(1), modspec(2), tok(HALF)]
                 + _token_specs(o_b, tm, n_first) + _token_specs(o_c, tm, n_first) + [
                  pl.BlockSpec((1, D_MODEL, SEG_G), lambda i: (l, 0, 0)),
                  pl.BlockSpec((1, 3, HALF, D_MODEL), lambda i: (l, 0, 0, 0)),
                  pl.BlockSpec((1, D_MODEL, D_MODEL), lambda i: (l, 0, 0)),
                  pl.BlockSpec((1, 1, D_MODEL), lambda i: (l, 0, 0)),
                  pl.BlockSpec((1, 1, D_MODEL), lambda i: (l, 0, 0))],
        out_specs=tok(D_MODEL),
        out_shape=jax.ShapeDtypeStruct((t, D_MODEL), F32),
        compiler_params=_cparams("parallel"),
        name="merge",
    )(*xs, mod, mod, mod, o_a, *o_b, *o_c, w_g, w_branch, w_out, ln_g, ln_b)


def _ffn_kernel(x_ref, sh_ref, sc_ref, g2_ref, wu_ref, wd_ref, lg_ref, lb_ref, *rest, alpha, n_f, n_first):
    *o_refs, h_ref, acc_ref = rest
    i = pl.program_id(0)
    j = pl.program_id(1)

    @pl.when(j == 0)
    def _():
        h_ref[...] = (x_ref[...] * (1.0 + sc_ref[0]) + sh_ref[0]).astype(BF16)
        acc_ref[...] = jnp.zeros(acc_ref.shape, F32)

    h = h_ref[...]
    n_g = wu_ref.shape[2] // FFN_GROUP
    cols = lambda g: slice(g * FFN_GROUP, (g + 1) * FFN_GROUP)
    ups, acts, downs = {}, {}, []
    for t in range(n_g + 2):
        if t < n_g:
            ups[t] = _dot(h, wu_ref[0, :, cols(t)])
        if 0 <= t - 1 < n_g:
            u = jnp.maximum(ups.pop(t - 1), 0.0)
            acts[t - 1] = (u * u).astype(BF16)
        if 0 <= t - 2 < n_g:
            downs.append(_dot(acts.pop(t - 2), wd_ref[0, cols(t - 2), :]))
    acc_ref[...] += functools.reduce(lambda a, b: a + b, downs)

    def finish(o_ref):
        o_ref[...] = _layer_norm(alpha * x_ref[...] + g2_ref[0] * acc_ref[...], lg_ref[0], lb_ref[0])

    last = j == n_f - 1
    if len(o_refs) == 1:
        pl.when(last)(lambda: finish(o_refs[0]))
    else:
        pl.when(jnp.logical_and(last, i < n_first))(lambda: finish(o_refs[0]))
        pl.when(jnp.logical_and(last, i >= n_first))(lambda: finish(o_refs[1]))


def _ffn(x, mod, w_up, w_down, ln_g, ln_b, l, tm, tf, t_ctx, l_lat, alpha, split=False):
    t = x.shape[0]
    n_f = D_FF // tf
    n_first = t_ctx // tm
    if split:
        out_specs = [pl.BlockSpec((tm, D_MODEL), lambda i, j: (jnp.minimum(i, n_first - 1), 0)),
                     pl.BlockSpec((tm, D_MODEL), lambda i, j: (jnp.maximum(i - n_first, 0), 0))]
        out_shape = [jax.ShapeDtypeStruct((t_ctx, D_MODEL), F32), jax.ShapeDtypeStruct((t - t_ctx, D_MODEL), F32)]
    else:
        out_specs = pl.BlockSpec((tm, D_MODEL), lambda i, j: (i, 0))
        out_shape = jax.ShapeDtypeStruct((t, D_MODEL), F32)
    row = functools.partial(_mod_row, tm=tm, t_ctx=t_ctx, l_lat=l_lat)
    modspec = lambda blk: pl.BlockSpec((1, 1, D_MODEL), lambda i, j: (row(i), 0, blk))
    return pl.pallas_call(
        functools.partial(_ffn_kernel, alpha=alpha, n_f=n_f, n_first=n_first),
        grid=(t // tm, n_f),
        in_specs=[pl.BlockSpec((tm, D_MODEL), lambda i, j: (i, 0)),
                  modspec(3), modspec(4), modspec(5),
                  pl.BlockSpec((1, D_MODEL, tf), lambda i, j: (l, 0, j)),
                  pl.BlockSpec((1, tf, D_MODEL), lambda i, j: (l, j, 0)),
                  pl.BlockSpec((1, 1, D_MODEL), lambda i, j: (l, 0, 0)),
                  pl.BlockSpec((1, 1, D_MODEL), lambda i, j: (l, 0, 0))],
        out_specs=out_specs,
        out_shape=out_shape,
        scratch_shapes=[pltpu.VMEM((tm, D_MODEL), BF16), pltpu.VMEM((tm, D_MODEL), F32)],
        compiler_params=_cparams("arbitrary", "arbitrary"),
        name="ffn",
    )(x, mod, mod, mod, w_up, w_down, ln_g, ln_b)


def _rope_tables(l):
    pos = jnp.arange(l, dtype=jnp.int32)
    row = (pos // GRID_W).astype(F32)
    col = (pos % GRID_W).astype(F32)
    half = HEAD // 2
    inv_freq = ROPE_THETA ** (-jnp.arange(0, half, 2, dtype=F32) / half)
    ang_r = row[:, None] * inv_freq[None, :]
    ang_c = col[:, None] * inv_freq[None, :]
    cos = jnp.concatenate([jnp.cos(ang_r), jnp.cos(ang_r), jnp.cos(ang_c), jnp.cos(ang_c)], axis=-1)
    sin = jnp.concatenate([-jnp.sin(ang_r), jnp.sin(ang_r), -jnp.sin(ang_c), jnp.sin(ang_c)], axis=-1)
    return jnp.tile(cos, (1, LANES // HEAD)), jnp.tile(sin, (1, LANES // HEAD))


def _head_block_matrix(width, value):
    idx = jnp.arange(width) // HEAD
    return jnp.where(idx[:, None] == idx[None, :], value, 0.0).astype(BF16)


def _pair_states(s):
    b = s.shape[0]
    s = s.reshape(b, 2, N_HEADS // 2, 2, HEAD, HEAD)
    z = jnp.zeros_like(s[:, :, :, 0])
    top = jnp.concatenate([z, s[:, :, :, 1]], axis=-1)
    bot = jnp.concatenate([s[:, :, :, 0], z], axis=-1)
    return jnp.concatenate([top, bot], axis=-2)


def _unpair_states(sp):
    b = sp.shape[0]
    s = jnp.stack([sp[..., HEAD:, :HEAD], sp[..., :HEAD, HEAD:]], axis=3)
    return s.reshape(b, 2, N_HEADS, HEAD, HEAD)


def _pick_tile(pref, *sizes):
    return min(pref, functools.reduce(math.gcd, sizes))


def kernel(x_prompt, x_sample, cache_k, cache_v, state_wkv, c, c_ctx, w_ada, b_ada, w_in, sgu_ln_g, sgu_ln_b, sgu_w, sgu_b, q_norm, k_norm, rwkv_mu, rwkv_w0, rwkv_w2, rwkv_a0, rwkv_a2, rwkv_k_k, rwkv_k_a, rwkv_r_k, rwkv_g2, rwkv_lnx_g, rwkv_lnx_b, w_branch, w_out, ln1_g, ln1_b, w_up, w_down, ln2_g, ln2_b):
    depth = w_in.shape[0]
    b_ctx, l_ctx, _ = x_prompt.shape
    b_lat, l_lat, _ = x_sample.shape
    past = cache_k.shape[2]
    t_ctx = b_ctx * l_ctx
    t_lat = b_lat * l_lat
    alpha = (2 * depth) ** 0.25

    tm = _pick_tile(512, t_ctx, l_lat)
    tm_ffn = _pick_tile(1024, t_ctx, l_lat)
    tk = _pick_tile(512, l_ctx, l_lat)
    tq_ctx = _pick_tile(256, l_ctx)
    tq_lat = _pick_tile(512, l_lat)
    c_ctx_chunk = _pick_tile(128, l_ctx)
    c_lat_chunk = _pick_tile(128, l_lat)

    n_rows = 1 + b_lat
    pad_rows = -n_rows % 16
    cvec = jnp.concatenate([c_ctx[None, :], c, jnp.zeros((pad_rows, D_MODEL), F32)], axis=0)
    mod_all = _ada(cvec, w_ada, b_ada[:, None, :])

    c_lo = SEG_A + SEG_B
    w_a_b = w_in[:, :, :SEG_A].astype(BF16)
    w_g_b = w_in[:, :, c_lo + SEG_C:].astype(BF16)
    w_bc_b = jnp.concatenate([w_in[:, :, c_lo:c_lo + SEG_C], jnp.zeros((depth, D_MODEL, B_OFF - SEG_C), F32),
                              w_in[:, :, SEG_A:c_lo]], axis=-1).astype(BF16)
    w_branch_b = w_branch.astype(BF16)
    w_out_b = w_out.astype(BF16)
    w_up_b = w_up.astype(BF16)
    w_down_b = w_down.astype(BF16)
    sgu_w_b = sgu_w.astype(BF16)
    g2_b = rwkv_g2.astype(BF16)
    seg_mean2 = _head_block_matrix(KV_W, 1.0 / HEAD)
    seg_mean8 = _head_block_matrix(HALF, 1.0 / HEAD)
    seg_ones8 = _head_block_matrix(HALF, 1.0)
    cos_ctx, sin_ctx = _rope_tables(l_ctx)
    cos_lat, sin_lat = _rope_tables(l_lat)
    zeros_lora = jnp.zeros((depth, 2, LORA, HALF), F32)
    wwa = jnp.concatenate([jnp.concatenate([rwkv_w2, zeros_lora], axis=-1),
                           jnp.concatenate([zeros_lora, rwkv_a2], axis=-1)], axis=-2).astype(BF16)

    xs = (x_prompt.reshape(t_ctx, D_MODEL), x_sample.reshape(t_lat, D_MODEL))
    b_s_full = jnp.repeat(jnp.swapaxes(sgu_b, 1, 2), HALF // SGU_GROUPS, axis=2)
    vec = lambda p: p[:, None, :]
    new_k, new_v, new_s = [], [], []
    for l in range(depth):
        mod = mod_all[l][:, None, :]
        o_a, seg_b = _proj(xs, mod, w_a_b, w_bc_b, vec(sgu_ln_g), vec(sgu_ln_b), sgu_w_b, b_s_full, l, tm, t_ctx, l_lat)
        seg_c = seg_b

        qn2 = jnp.tile(q_norm[l], LANES // HEAD)[None]
        kn2 = jnp.tile(k_norm[l], KV_W // HEAD)[None]
        kn_ctx, kr_ctx, vb_ctx = _kvprep(seg_b, 0, b_ctx, l_ctx, kn2, seg_mean2, cos_ctx, sin_ctx, False, tk)
        kr_lat, vb_lat = _kvprep(seg_b, t_ctx, b_lat, l_lat, kn2, seg_mean2, cos_lat, sin_lat, True, tk)
        new_k.append(kn_ctx.reshape(b_ctx, l_ctx, N_KV, HEAD))
        new_v.append(seg_b[:t_ctx, B_OFF + HALF + KV_W:].astype(F32).reshape(b_ctx, l_ctx, N_KV, HEAD))
        kt_lat = jnp.concatenate([jnp.swapaxes(cache_k[:, l].reshape(b_lat, past, KV_W), 1, 2).astype(BF16),
                                  kr_lat], axis=2)
        v_lat = jnp.concatenate([cache_v[:, l].reshape(b_lat, past, KV_W).astype(BF16),
                                 vb_lat.reshape(b_lat, l_lat, KV_W)], axis=1)
        o_b = (_attention(seg_b, 0, b_ctx, l_ctx, kr_ctx, vb_ctx.reshape(b_ctx, l_ctx, KV_W),
                          qn2, seg_mean2, cos_ctx, sin_ctx, False, tq_ctx),
               _attention(seg_b, t_ctx, b_lat, l_lat, kt_lat, v_lat, qn2, seg_mean2, cos_lat, sin_lat, True, tq_lat))

        rw = dict(layer=l, mu=rwkv_mu[:, :, None, :], w0=rwkv_w0[:, :, None, :], a0=rwkv_a0[:, :, None, :], wwa=wwa,
                  k_k=vec(rwkv_k_k), k_a=vec(rwkv_k_a), r_k=rwkv_r_k.reshape(depth, 1, HALF), seg1=seg_ones8)
        y0c, y1c, s_ctx = _wkv(seg_c, 0, b_ctx, l_ctx, c_ctx_chunk, None, **rw)
        y0l, y1l, _ = _wkv(seg_c, t_ctx, b_lat, l_lat, c_lat_chunk, _pair_states(state_wkv[:, l]), **rw)
        new_s.append(_unpair_states(s_ctx))
        post = functools.partial(_rwkv_post, segm=seg_mean8, g2=g2_b, lnx_g=vec(rwkv_lnx_g), lnx_b=vec(rwkv_lnx_b),
                                 l=l, tm=tm)
        o_c = (post(y0c, y1c, seg_c, 0), post(y0l, y1l, seg_c, t_ctx))

        x = _merge(xs, o_a, o_b, o_c, mod, w_g_b, w_branch_b, w_out_b, vec(ln1_g), vec(ln1_b), l, tm, t_ctx, l_lat, alpha)
        x = _ffn(x, mod, w_up_b, w_down_b, vec(ln2_g), vec(ln2_b), l, tm_ffn, 1024, t_ctx, l_lat, alpha,
                 split=(l == depth - 1))
        xs = (x,)

    y = x[0].reshape(b_ctx, l_ctx, D_MODEL)
    z = x[1].reshape(b_lat, l_lat, D_MODEL)
    return (y, z, jnp.stack(new_k, axis=1), jnp.stack(new_v, axis=1), jnp.stack(new_s, axis=1))
```

```python
import functools
import math

import jax
import jax.numpy as jnp
from jax import lax
from jax.experimental import pallas as pl
from jax.experimental.pallas import tpu as pltpu

F32 = jnp.float32
BF16 = jnp.bfloat16

D_MODEL = 1024
HALF = D_MODEL // 2
HEAD = 64
N_HEADS = HALF // HEAD
N_KV = 2
GQA = N_HEADS // N_KV
KV_W = N_KV * HEAD
GRID_W = 64
SGU_CHUNK = 128
SGU_GROUPS = 4
LORA = 64
GATE_LORA = 128
D_FF = 4 * D_MODEL
ROPE_THETA = 10000.0
GN_EPS = 64e-5
SEG_A = 2 * HALF
SEG_B = HALF + 2 * KV_W
SEG_C = 3 * HALF + 4 * LORA + GATE_LORA
SEG_G = 3 * D_MODEL
RKV_W = 3 * HALF
B_OFF = 2048
SEG_BC = B_OFF + SEG_B
FFN_GROUP = 512
LANES = 128
NB_ROWS = 16
VMEM_LIMIT = 48 * 1024 * 1024


def _cparams(*sem):
    return pltpu.CompilerParams(dimension_semantics=sem, vmem_limit_bytes=VMEM_LIMIT)


def _dot(a, b):
    return jnp.dot(a, b, preferred_element_type=F32)


def _dot_nt(a, b):
    return lax.dot_general(a, b, (((1,), (1,)), ((), ())), preferred_element_type=F32)


def _dot_tn(a, b):
    return lax.dot_general(a, b, (((0,), (0,)), ((), ())), preferred_element_type=F32)


def _layer_norm(x, g, b, eps=1e-5):
    mu = jnp.mean(x, axis=-1, keepdims=True)
    xc = x - mu
    var = jnp.mean(xc * xc, axis=-1, keepdims=True)
    return xc * lax.rsqrt(var + eps) * g + b


def _sigmoid(x):
    return 0.5 * jnp.tanh(0.5 * x) + 0.5


def _ada_kernel(c_ref, w_ref, b_ref, o_ref):
    c = c_ref[...]
    s = (c * jax.nn.sigmoid(c)).astype(BF16)
    o_ref[0] = _dot(s, w_ref[0].astype(BF16)) + b_ref[0]


def _ada(cvec, w_ada, b_ada):
    depth, _, n = w_ada.shape
    r = cvec.shape[0]
    tn = 1536
    return pl.pallas_call(
        _ada_kernel,
        grid=(depth, n // tn),
        in_specs=[pl.BlockSpec((r, D_MODEL), lambda l, j: (0, 0)),
                  pl.BlockSpec((1, D_MODEL, tn), lambda l, j: (l, 0, j)),
                  pl.BlockSpec((1, 1, tn), lambda l, j: (l, 0, j))],
        out_specs=pl.BlockSpec((1, r, tn), lambda l, j: (l, 0, j)),
        out_shape=jax.ShapeDtypeStruct((depth, r, n), F32),
        compiler_params=_cparams("parallel", "parallel"),
        name="ada",
    )(cvec, w_ada, b_ada)


def _token_tile(x_refs, n_first):
    if len(x_refs) == 1:
        return x_refs[0][...]
    return jnp.where(pl.program_id(0) < n_first, x_refs[0][...], x_refs[1][...])


def _token_specs(xs, tm, n_first):
    width = xs[0].shape[1]
    if len(xs) == 1:
        return [pl.BlockSpec((tm, width), lambda i: (i, 0))]
    return [pl.BlockSpec((tm, width), lambda i: (jnp.minimum(i, n_first - 1), 0)),
            pl.BlockSpec((tm, width), lambda i: (jnp.maximum(i - n_first, 0), 0))]


def _mod_row(i, tm, t_ctx, l_lat):
    r = i * tm
    return jnp.where(r < t_ctx, 0, 1 + (r - t_ctx) // l_lat)


def _proj_kernel(*refs, tm, n_x, n_first):
    sh_ref, sc_ref, wa_ref, wbc_ref, g_ref, b_ref, ws_ref, bs_ref, oa_ref, obc_ref = refs[n_x:]
    h = (_token_tile(refs[:n_x], n_first) * (1.0 + sc_ref[0]) + sh_ref[0]).astype(BF16)
    uv = _dot(h, wa_ref[0])
    vn = _layer_norm(uv[:, HALF:], g_ref[0], b_ref[0]).astype(BF16)
    n_bc = wbc_ref.shape[2]
    for j in range(2):
        cols = slice(j * (n_bc // 2), (j + 1) * (n_bc // 2))
        obc_ref[:, cols] = _dot(h, wbc_ref[0, :, cols]).astype(obc_ref.dtype)
    gc = HALF // SGU_GROUPS
    for n in range(tm // SGU_CHUNK):
        rows = slice(n * SGU_CHUNK, (n + 1) * SGU_CHUNK)
        for g in range(SGU_GROUPS):
            cols = slice(g * gc, (g + 1) * gc)
            s = _dot(ws_ref[0, g], vn[rows, cols]) + bs_ref[0, :, cols]
            oa_ref[rows, cols] = (uv[rows, cols] * s).astype(oa_ref.dtype)


def _proj(xs, mod, w_a, w_bc, ln_g, ln_b, w_s, b_s_full, l, tm, t_ctx, l_lat):
    t = sum(x.shape[0] for x in xs)
    n_first = t_ctx // tm
    row = functools.partial(_mod_row, tm=tm, t_ctx=t_ctx, l_lat=l_lat)
    return pl.pallas_call(
        functools.partial(_proj_kernel, tm=tm, n_x=len(xs), n_first=n_first),
        grid=(t // tm,),
        in_specs=_token_specs(xs, tm, n_first) + [
                  pl.BlockSpec((1, 1, D_MODEL), lambda i: (row(i), 0, 0)),
                  pl.BlockSpec((1, 1, D_MODEL), lambda i: (row(i), 0, 1)),
                  pl.BlockSpec((1, D_MODEL, SEG_A), lambda i: (l, 0, 0)),
                  pl.BlockSpec((1, D_MODEL, SEG_BC), lambda i: (l, 0, 0)),
                  pl.BlockSpec((1, 1, HALF), lambda i: (l, 0, 0)),
                  pl.BlockSpec((1, 1, HALF), lambda i: (l, 0, 0)),
                  pl.BlockSpec((1, SGU_GROUPS, SGU_CHUNK, SGU_CHUNK), lambda i: (l, 0, 0, 0)),
                  pl.BlockSpec((1, SGU_CHUNK, HALF), lambda i: (l, 0, 0))],
        out_specs=[pl.BlockSpec((tm, HALF), lambda i: (i, 0)),
                   pl.BlockSpec((tm, SEG_BC), lambda i: (i, 0))],
        out_shape=[jax.ShapeDtypeStruct((t, HALF), BF16), jax.ShapeDtypeStruct((t, SEG_BC), BF16)],
        compiler_params=_cparams("parallel"),
        name="proj",
    )(*xs, mod, mod, w_a, w_bc, ln_g, ln_b, w_s, b_s_full)


def _rope_swap(x):
    lane = lax.broadcasted_iota(jnp.int32, x.shape, 1)
    up = pltpu.roll(x, LANES - 16, axis=1)
    dn = pltpu.roll(x, 16, axis=1)
    return jnp.where((lane & 16) == 0, up, dn)


def _head_rms(x, seg_ref, g):
    ms = _dot((x * x).astype(BF16), seg_ref[...])
    return x * lax.rsqrt(ms + 1e-6) * g


def _kvprep_kernel(k_ref, v_ref, g_ref, seg_ref, cos_ref, sin_ref, *out_refs, rope):
    kn = _head_rms(k_ref[...].astype(F32), seg_ref, g_ref[...])
    if rope:
        kt_ref, vb_ref = out_refs
        kn = kn * cos_ref[...] + _rope_swap(kn) * sin_ref[...]
    else:
        kn_ref, kt_ref, vb_ref = out_refs
        kn_ref[...] = kn
    kt_ref[0] = kn.T.astype(BF16)
    vb_ref[...] = v_ref[...].astype(BF16)


def _kvprep(seg_b, row_off, b, l, k_norm2, seg_mat, cos, sin, rope, tk):
    t = b * l
    off = row_off // tk
    lb = l // tk
    out_shape = [jax.ShapeDtypeStruct((b, KV_W, l), BF16), jax.ShapeDtypeStruct((t, KV_W), BF16)]
    out_specs = [pl.BlockSpec((1, KV_W, tk), lambda i: (i // lb, 0, i % lb)),
                 pl.BlockSpec((tk, KV_W), lambda i: (i, 0))]
    if not rope:
        out_shape = [jax.ShapeDtypeStruct((t, KV_W), F32)] + out_shape
        out_specs = [pl.BlockSpec((tk, KV_W), lambda i: (i, 0))] + out_specs
    return pl.pallas_call(
        functools.partial(_kvprep_kernel, rope=rope),
        grid=(t // tk,),
        in_specs=[pl.BlockSpec((tk, KV_W), lambda i: (off + i, (B_OFF + HALF) // KV_W)),
                  pl.BlockSpec((tk, KV_W), lambda i: (off + i, (B_OFF + HALF) // KV_W + 1)),
                  pl.BlockSpec((1, KV_W), lambda i: (0, 0)),
                  pl.BlockSpec((KV_W, KV_W), lambda i: (0, 0)),
                  pl.BlockSpec((tk, KV_W), lambda i: (i % lb, 0)),
                  pl.BlockSpec((tk, KV_W), lambda i: (i % lb, 0))],
        out_specs=out_specs,
        out_shape=out_shape,
        compiler_params=_cparams("parallel"),
        name="kvprep",
    )(seg_b, seg_b, k_norm2, seg_mat, cos, sin)


def _attn_kernel(q_ref, g_ref, seg_ref, cos_ref, sin_ref, kt_ref, v_ref, o_ref, *, rope, tq):
    qs = []
    for s in range(HALF // LANES):
        q = _head_rms(q_ref[:, s * LANES:(s + 1) * LANES].astype(F32), seg_ref, g_ref[...])
        if rope:
            q = q * cos_ref[...] + _rope_swap(q) * sin_ref[...]
        qs.append((q * (HEAD ** -0.5 * math.log2(math.e))).astype(BF16))
    lo = lax.broadcasted_iota(jnp.int32, (tq, LANES), 1) < HEAD
    scores, probs, ratios = {}, {}, {}

    kt = kt_ref[0]
    zeros = jnp.zeros((HEAD, kt.shape[1]), BF16)
    k_slab = {}
    for g in range(N_KV):
        kg = kt[g * HEAD:(g + 1) * HEAD]
        k_slab[g, 0] = jnp.concatenate([kg, zeros], axis=0)
        k_slab[g, 1] = jnp.concatenate([zeros, kg], axis=0)
    lane_row = lax.broadcasted_iota(jnp.int32, (1, LANES), 1)
    own = [(lane_row < HEAD).astype(F32).astype(BF16), (lane_row >= HEAD).astype(F32).astype(BF16)]
    v_slab = [v_ref[0] * own[g] + own[1 - g] for g in range(N_KV)]

    def qk(h):
        scores[h] = _dot(qs[h // 2], k_slab[h // GQA, h % 2])

    def softmax(h):
        s = scores.pop(h)
        probs[h] = jnp.exp2(s - jnp.max(s, axis=-1, keepdims=True)).astype(BF16)

    def pv(h):
        g = h // GQA
        oe = _dot(probs.pop(h), v_slab[g])
        sw = pltpu.roll(oe, HEAD, axis=1)
        ratios[h] = oe / sw if (h % 2 == 0) == (g == 0) else sw / oe
        if h % 2 == 1:
            pair = jnp.where(lo, ratios.pop(h - 1), ratios.pop(h))
            o_ref[:, (h // 2) * LANES:(h // 2 + 1) * LANES] = pair.astype(o_ref.dtype)

    for t in range(N_HEADS + 2):
        if t < N_HEADS:
            qk(t)
        if 0 <= t - 1 < N_HEADS:
            softmax(t - 1)
        if 0 <= t - 2 < N_HEADS:
            pv(t - 2)


def _attention(seg_b, row_off, b, l, kt, v, q_norm2, seg_mat, cos, sin, rope, tq):
    off = row_off // tq
    lb = l // tq
    lk = kt.shape[-1]
    in_specs = [pl.BlockSpec((tq, HALF), lambda bi, i: (off + bi * lb + i, B_OFF // HALF)),
                pl.BlockSpec((1, LANES), lambda bi, i: (0, 0)),
                pl.BlockSpec((KV_W, KV_W), lambda bi, i: (0, 0)),
                pl.BlockSpec((tq, LANES), lambda bi, i: (i, 0)),
                pl.BlockSpec((tq, LANES), lambda bi, i: (i, 0)),
                pl.BlockSpec((1, KV_W, lk), lambda bi, i: (bi, 0, 0)),
                pl.BlockSpec((1, lk, KV_W), lambda bi, i: (bi, 0, 0))]
    return pl.pallas_call(
        functools.partial(_attn_kernel, rope=rope, tq=tq),
        grid=(b, lb),
        in_specs=in_specs,
        out_specs=pl.BlockSpec((tq, HALF), lambda bi, i: (bi * lb + i, 0)),
        out_shape=jax.ShapeDtypeStruct((b * l, HALF), BF16),
        compiler_params=_cparams("parallel", "parallel"),
        name="attention",
    )(seg_b, q_norm2, seg_mat, cos, sin, kt, v)


def _split2(x):
    h = x.astype(BF16)
    return h, (x - h.astype(F32)).astype(BF16)


def _wkv_prep(x, nb_row, d, c, mu_rkv, mu_lo, w0, a0, wwa, k_k, k_a, r_k, seg1, y_ref, rev):
    rows = lax.broadcasted_iota(jnp.int32, (c, 1), 0)
    edge = (c - 1) if rev else 0

    def shifted(cur, nb):
        rolled = pltpu.roll(cur, (c - 1) if rev else 1, axis=0)
        return jnp.where(rows == edge, nb, rolled)

    rkv = x[:, :RKV_W]
    lo = x[:, RKV_W + 2 * LORA * d:RKV_W + 2 * LORA * (d + 1)]
    f = rkv + mu_rkv * (shifted(rkv, nb_row[:, :RKV_W]) - rkv)
    fl = lo + mu_lo * (shifted(lo, nb_row[:, RKV_W + 2 * LORA * d:RKV_W + 2 * LORA * (d + 1)]) - lo)
    r = f[:, :HALF]
    k = f[:, HALF:2 * HALF]
    v = f[:, 2 * HALF:]
    lane = lax.broadcasted_iota(jnp.int32, fl.shape, 1)
    lin = _dot(jnp.where(lane < LORA, jnp.tanh(fl), fl).astype(BF16), wwa)
    lw = (-math.exp(-0.5)) * _sigmoid(w0 + lin[:, :HALF])
    asig = _sigmoid(a0 + lin[:, HALF:])
    kk = k * k_k
    ss = _dot((kk * kk).astype(BF16), seg1)
    kkn = kk * lax.rsqrt(jnp.maximum(ss, 1e-24))
    kmod = k * (1.0 + (asig - 1.0) * k_a)
    bonus = _dot((r * kmod * r_k).astype(BF16), seg1) * v
    y_ref[:, HALF:] = bonus.astype(y_ref.dtype)

    ti = lax.broadcasted_iota(jnp.int32, (c, c), 0)
    si = lax.broadcasted_iota(jnp.int32, (c, c), 1)
    incl = (si >= ti) if rev else (si <= ti)
    strict = (si > ti) if rev else (si < ti)
    tri = incl.astype(BF16)
    hi2, lo2 = _split2(lw)
    cum = _dot(tri, hi2) + _dot(tri, lo2)
    ref = cum[c // 2:c // 2 + 1]
    end = 0 if rev else c - 1
    cum_end = cum[end:end + 1]
    g = cum - ref
    e_pos = jnp.exp(g)
    e_neg = jnp.exp(-g)
    e_ref = jnp.exp(ref)
    e_tot = jnp.exp(cum_end)
    e_end = jnp.exp(cum_end - ref)
    at_c = -kkn * jnp.exp(g - lw)
    rt_c = r * e_pos
    bt = kkn * asig * e_neg
    kt = kmod * e_neg
    return dict(at_c=at_c, rt_c=rt_c, bt=bt, kt=kt, at_true=at_c * e_ref, rt_true=rt_c * e_ref,
                bh=bt * e_end, kh=kt * e_end, v=v, e_tot=e_tot, incl=incl, strict=strict)


def _wkv_chains(preps, c, s_ref, y_refs):
    lane = lax.broadcasted_iota(jnp.int32, (c, LANES), 1)
    lo = lane < HEAD
    hi = jnp.logical_not(lo)
    chains = [(d, p) for d in range(2) for p in range(N_HEADS // 2)]

    def slab(d, p, name):
        return preps[d][name][:, p * LANES:(p + 1) * LANES]

    def keep(mask, x):
        return jnp.where(mask, x, 0.0)

    n_pow, a_ak, m_all, vsw, x_cur = {}, {}, {}, {}, {}
    for ch in chains:
        d, p = ch
        at_c, rt_c = slab(d, p, "at_c"), slab(d, p, "rt_c")
        lhs = jnp.concatenate([keep(lo, at_c), keep(hi, at_c), keep(lo, rt_c), keep(hi, rt_c)], axis=0)
        rhs = jnp.concatenate([slab(d, p, "bt"), slab(d, p, "kt")], axis=0)
        g = _dot_nt(lhs.astype(BF16), rhs.astype(BF16))
        strict, incl = preps[d]["strict"], preps[d]["incl"]
        incl2 = jnp.concatenate([incl, incl], axis=1)
        n_pow[ch] = [keep(strict, g[h * c:(h + 1) * c, :c]).astype(BF16) for h in range(2)]
        a_ak[ch] = [keep(strict, g[h * c:(h + 1) * c, c:]).astype(BF16) for h in range(2)]
        m_all[ch] = [keep(incl2, g[(2 + h) * c:(3 + h) * c, :]).astype(BF16) for h in range(2)]
        v_sw = pltpu.roll(slab(d, p, "v"), HEAD, axis=1)
        vsw[ch] = [keep(hi, v_sw).astype(BF16), keep(lo, v_sw).astype(BF16)]
    for ch in chains:
        d, p = ch
        at_true = slab(d, p, "at_true")
        x_cur[ch] = [keep(lo, at_true) + _dot(a_ak[ch][0], vsw[ch][0]),
                     keep(hi, at_true) + _dot(a_ak[ch][1], vsw[ch][1])]
    steps = int(math.log2(c))
    for j in range(steps):
        for ch in chains:
            for h in range(2):
                xb = x_cur[ch][h].astype(BF16)
                if j + 1 < steps:
                    out = _dot(n_pow[ch][h], jnp.concatenate([xb, n_pow[ch][h]], axis=1))
                    x_cur[ch][h] = x_cur[ch][h] + out[:, :LANES]
                    n_pow[ch][h] = out[:, LANES:].astype(BF16)
                else:
                    x_cur[ch][h] = x_cur[ch][h] + _dot(n_pow[ch][h], xb)
    st, w, kb = {}, {}, {}
    for ch in chains:
        d, p = ch
        x0, x1 = x_cur[ch]
        st[ch] = _dot_nt(jnp.concatenate([x0, x1, slab(d, p, "rt_true")], axis=0).astype(BF16),
                         s_ref[d, p].astype(BF16))
        bh, kh = slab(d, p, "bh"), slab(d, p, "kh")
        kb[ch] = jnp.concatenate([keep(lo, bh), keep(lo, kh), keep(hi, bh), keep(hi, kh)], axis=0).astype(BF16)
    for ch in chains:
        x0, x1 = x_cur[ch]
        w[ch] = [jnp.concatenate([keep(hi, st[ch][:c] + x0).astype(BF16), vsw[ch][0]], axis=0),
                 jnp.concatenate([keep(lo, st[ch][c:2 * c] + x1).astype(BF16), vsw[ch][1]], axis=0)]
    for ch in chains:
        d, p = ch
        e_tot = preps[d]["e_tot"][:, p * LANES:(p + 1) * LANES]
        s_ref[d, p] = s_ref[d, p] * e_tot + _dot_tn(jnp.concatenate(w[ch], axis=0), kb[ch])
    for ch in chains:
        d, p = ch
        y_sw = st[ch][2 * c:] + _dot(m_all[ch][0], w[ch][0]) + _dot(m_all[ch][1], w[ch][1])
        y_refs[d][:, p * LANES:(p + 1) * LANES] = pltpu.roll(y_sw, HEAD, axis=1).astype(y_refs[d].dtype)


def _wkv_kernel(*refs, c, n_c, latent):
    if latent:
        (x0_ref, x1_ref, p0_ref, n1_ref, s0_ref, mu_ref, w0_ref, a0_ref, wwa_ref, kk_ref, ka_ref, rk_ref,
         seg_ref, y0_ref, y1_ref, sf_ref, s_ref) = refs
    else:
        (x0_ref, x1_ref, p0_ref, n1_ref, mu_ref, w0_ref, a0_ref, wwa_ref, kk_ref, ka_ref, rk_ref,
         seg_ref, y0_ref, y1_ref, sf_ref, s_ref) = refs
    i = pl.program_id(1)

    @pl.when(i == 0)
    def _():
        s_ref[...] = s0_ref[0] if latent else jnp.zeros(s_ref.shape, F32)

    inner = (i > 0).astype(F32)
    preps = []
    for d, (x_ref, nb_ref, y_ref) in enumerate(((x0_ref, p0_ref, y0_ref), (x1_ref, n1_ref, y1_ref))):
        nb = nb_ref[NB_ROWS - 1:NB_ROWS, :] if d == 0 else nb_ref[0:1, :]
        preps.append(_wkv_prep(x_ref[...].astype(F32), nb.astype(F32) * inner, d, c,
                               mu_ref[0, d, :, :RKV_W], mu_ref[0, d, :, RKV_W:], w0_ref[0, d], a0_ref[0, d],
                               wwa_ref[0, d], kk_ref[0], ka_ref[0], rk_ref[0], seg_ref[...], y_ref, rev=(d == 1)))
    _wkv_chains(preps, c, s_ref, (y0_ref, y1_ref))

    @pl.when(i == n_c - 1)
    def _():
        sf_ref[0] = s_ref[...]


def _wkv(seg_c, row_off, b, l, c, s0, layer, mu, w0, a0, wwa, k_k, k_a, r_k, seg1):
    t = b * l
    n_c = l // c
    t_all = seg_c.shape[0]
    cb = row_off // c
    c8 = c // NB_ROWS
    r8 = row_off // NB_ROWS
    last8 = t_all // NB_ROWS - 1
    latent = s0 is not None
    const2 = lambda bi, i: (0, 0)
    in_specs = [pl.BlockSpec((c, SEG_C), lambda bi, i: (cb + bi * n_c + i, 0)),
                pl.BlockSpec((c, SEG_C), lambda bi, i: (cb + bi * n_c + n_c - 1 - i, 0)),
                pl.BlockSpec((NB_ROWS, SEG_C),
                             lambda bi, i: (jnp.maximum(r8 + (bi * n_c + i) * c8 - 1, 0), 0)),
                pl.BlockSpec((NB_ROWS, SEG_C),
                             lambda bi, i: (jnp.minimum(r8 + (bi * n_c + n_c - i) * c8, last8), 0))]
    args = [seg_c, seg_c, seg_c, seg_c]
    if latent:
        in_specs.append(pl.BlockSpec((1, 2, N_HEADS // 2, LANES, LANES), lambda bi, i: (bi, 0, 0, 0, 0)))
        args.append(s0)
    per_layer = lambda p: pl.BlockSpec((1,) + p.shape[1:], lambda bi, i: (layer,) + (0,) * (p.ndim - 1))
    in_specs += [per_layer(p) for p in (mu, w0, a0, wwa, k_k, k_a, r_k)] + [pl.BlockSpec(seg1.shape, const2)]
    args += [mu, w0, a0, wwa, k_k, k_a, r_k, seg1]
    return pl.pallas_call(
        functools.partial(_wkv_kernel, c=c, n_c=n_c, latent=latent),
        grid=(b, n_c),
        in_specs=in_specs,
        out_specs=[pl.BlockSpec((c, 2 * HALF), lambda bi, i: (bi * n_c + i, 0)),
                   pl.BlockSpec((c, 2 * HALF), lambda bi, i: (bi * n_c + n_c - 1 - i, 0)),
                   pl.BlockSpec((1, 2, N_HEADS // 2, LANES, LANES), lambda bi, i: (bi, 0, 0, 0, 0))],
        out_shape=[jax.ShapeDtypeStruct((t, 2 * HALF), BF16), jax.ShapeDtypeStruct((t, 2 * HALF), BF16),
                   jax.ShapeDtypeStruct((b, 2, N_HEADS // 2, LANES, LANES), F32)],
        scratch_shapes=[pltpu.VMEM((2, N_HEADS // 2, LANES, LANES), F32)],
        compiler_params=_cparams("parallel", "arbitrary"),
        name="wkv",
    )(*args)


def _rwkv_post_kernel(y0_ref, y1_ref, gd_ref, segm_ref, g2_ref, lg_ref, lb_ref, o_ref):
    ys = y0_ref[:, :HALF].astype(F32) + y1_ref[:, :HALF].astype(F32)
    bonus = y0_ref[:, HALF:].astype(F32) + y1_ref[:, HALF:].astype(F32)
    mu = _dot(ys.astype(BF16), segm_ref[...])
    yc = ys - mu
    var = _dot((yc * yc).astype(BF16), segm_ref[...])
    gn = yc * lax.rsqrt(var + GN_EPS) * lg_ref[0] + lb_ref[0]
    gate = _dot(_sigmoid(gd_ref[...].astype(F32)).astype(BF16), g2_ref[0])
    o_ref[...] = ((gn + bonus) * gate).astype(o_ref.dtype)


def _rwkv_post(y0, y1, seg_c, row_off, segm, g2, lnx_g, lnx_b, l, tm):
    t = y0.shape[0]
    off = row_off // tm
    in_specs = [pl.BlockSpec((tm, 2 * HALF), lambda i: (i, 0)),
                pl.BlockSpec((tm, 2 * HALF), lambda i: (i, 0)),
                pl.BlockSpec((tm, GATE_LORA), lambda i: (off + i, (SEG_C - GATE_LORA) // GATE_LORA)),
                pl.BlockSpec((HALF, HALF), lambda i: (0, 0)),
                pl.BlockSpec((1, GATE_LORA, HALF), lambda i: (l, 0, 0)),
                pl.BlockSpec((1, 1, HALF), lambda i: (l, 0, 0)),
                pl.BlockSpec((1, 1, HALF), lambda i: (l, 0, 0))]
    return pl.pallas_call(
        _rwkv_post_kernel,
        grid=(t // tm,),
        in_specs=in_specs,
        out_specs=pl.BlockSpec((tm, HALF), lambda i: (i, 0)),
        out_shape=jax.ShapeDtypeStruct((t, HALF), BF16),
        compiler_params=_cparams("parallel"),
        name="rwkv_post",
    )(y0, y1, seg_c, segm, g2, lnx_g, lnx_b)


def _merge_kernel(*refs, alpha, n_x, n_first):
    (sh_ref, sc_ref, g1_ref, oa_ref, ob0_ref, ob1_ref, oc0_ref, oc1_ref, wg_ref, wb_ref, wo_ref, lg_ref, lb_ref,
     o_ref) = refs[n_x:]
    x = _token_tile(refs[:n_x], n_first)
    h = (x * (1.0 + sc_ref[0]) + sh_ref[0]).astype(BF16)
    branches = (oa_ref[...], _token_tile((ob0_ref, ob1_ref), n_first), _token_tile((oc0_ref, oc1_ref), n_first))
    acc = None
    for j, br in enumerate(branches):
        gate = _sigmoid(_dot(h, wg_ref[0, :, j * D_MODEL:(j + 1) * D_MODEL]))
        term = gate * _dot(br, wb_ref[0, j])
        acc = term if acc is None else acc + term
    mixed = _dot(acc.astype(BF16), wo_ref[0])
    o_ref[...] = _layer_norm(alpha * x + g1_ref[0] * mixed, lg_ref[0], lb_ref[0])


def _merge(xs, o_a, o_b, o_c, mod, w_g, w_branch, w_out, ln_g, ln_b, l, tm, t_ctx, l_lat, alpha):
    t = sum(x.shape[0] for x in xs)
    n_first = t_ctx // tm
    row = functools.partial(_mod_row, tm=tm, t_ctx=t_ctx, l_lat=l_lat)
    tok = lambda w: pl.BlockSpec((tm, w), lambda i: (i, 0))
    modspec = lambda blk: pl.BlockSpec((1, 1, D_MODEL), lambda i: (row(i), 0, blk))
    return pl.pallas_call(
        functools.partial(_merge_kernel, alpha=alpha, n_x=len(xs), n_first=n_first),
        grid=(t // tm,),
        in_specs=_token_specs(xs, tm, n_first) + [modspec(0), modspec(1), modspec(2), tok(HALF)]
                 + _token_specs(o_b, tm, n_first) + _token_specs(o_c, tm, n_first) + [
                  pl.BlockSpec((1, D_MODEL, SEG_G), lambda i: (l, 0, 0)),
                  pl.BlockSpec((1, 3, HALF, D_MODEL), lambda i: (l, 0, 0, 0)),
                  pl.BlockSpec((1, D_MODEL, D_MODEL), lambda i: (l, 0, 0)),
                  pl.BlockSpec((1, 1, D_MODEL), lambda i: (l, 0, 0)),
                  pl.BlockSpec((1, 1, D_MODEL), lambda i: (l, 0, 0))],
        out_specs=tok(D_MODEL),
        out_shape=jax.ShapeDtypeStruct((t, D_MODEL), F32),
        compiler_params=_cparams("parallel"),
        name="merge",
    )(*xs, mod, mod, mod, o_a, *o_b, *o_c, w_g, w_branch, w_out, ln_g, ln_b)


def _ffn_kernel(x_ref, sh_ref, sc_ref, g2_ref, wu_ref, wd_ref, lg_ref, lb_ref, *rest, alpha, n_f, n_first):
    *o_refs, h_ref, acc_ref = rest
    i = pl.program_id(0)
    j = pl.program_id(1)

    @pl.when(j == 0)
    def _():
        h_ref[...] = (x_ref[...] * (1.0 + sc_ref[0]) + sh_ref[0]).astype(BF16)
        acc_ref[...] = jnp.zeros(acc_ref.shape, F32)

    h = h_ref[...]
    n_g = wu_ref.shape[2] // FFN_GROUP
    cols = lambda g: slice(g * FFN_GROUP, (g + 1) * FFN_GROUP)
    ups, acts, downs = {}, {}, []
    for t in range(n_g + 2):
        if t < n_g:
            ups[t] = _dot(h, wu_ref[0, :, cols(t)])
        if 0 <= t - 1 < n_g:
            u = jnp.maximum(ups.pop(t - 1), 0.0)
            acts[t - 1] = (u * u).astype(BF16)
        if 0 <= t - 2 < n_g:
            downs.append(_dot(acts.pop(t - 2), wd_ref[0, cols(t - 2), :]))
    acc_ref[...] += functools.reduce(lambda a, b: a + b, downs)

    def finish(o_ref):
        o_ref[...] = _layer_norm(alpha * x_ref[...] + g2_ref[0] * acc_ref[...], lg_ref[0], lb_ref[0])

    last = j == n_f - 1
    if len(o_refs) == 1:
        pl.when(last)(lambda: finish(o_refs[0]))
    else:
        pl.when(jnp.logical_and(last, i < n_first))(lambda: finish(o_refs[0]))
        pl.when(jnp.logical_and(last, i >= n_first))(lambda: finish(o_refs[1]))


def _ffn(x, mod, w_up, w_down, ln_g, ln_b, l, tm, tf, t_ctx, l_lat, alpha, split=False):
    t = x.shape[0]
    n_f = D_FF // tf
    n_first = t_ctx // tm
    if split:
        out_specs = [pl.BlockSpec((tm, D_MODEL), lambda i, j: (jnp.minimum(i, n_first - 1), 0)),
                     pl.BlockSpec((tm, D_MODEL), lambda i, j: (jnp.maximum(i - n_first, 0), 0))]
        out_shape = [jax.ShapeDtypeStruct((t_ctx, D_MODEL), F32), jax.ShapeDtypeStruct((t - t_ctx, D_MODEL), F32)]
    else:
        out_specs = pl.BlockSpec((tm, D_MODEL), lambda i, j: (i, 0))
        out_shape = jax.ShapeDtypeStruct((t, D_MODEL), F32)
    row = functools.partial(_mod_row, tm=tm, t_ctx=t_ctx, l_lat=l_lat)
    modspec = lambda blk: pl.BlockSpec((1, 1, D_MODEL), lambda i, j: (row(i), 0, blk))
    return pl.pallas_call(
        functools.partial(_ffn_kernel, alpha=alpha, n_f=n_f, n_first=n_first),
        grid=(t // tm, n_f),
        in_specs=[pl.BlockSpec((tm, D_MODEL), lambda i, j: (i, 0)),
                  modspec(3), modspec(4), modspec(5),
                  pl.BlockSpec((1, D_MODEL, tf), lambda i, j: (l, 0, j)),
                  pl.BlockSpec((1, tf, D_MODEL), lambda i, j: (l, j, 0)),
                  pl.BlockSpec((1, 1, D_MODEL), lambda i, j: (l, 0, 0)),
                  pl.BlockSpec((1, 1, D_MODEL), lambda i, j: (l, 0, 0))],
        out_specs=out_specs,
        out_shape=out_shape,
        scratch_shapes=[pltpu.VMEM((tm, D_MODEL), BF16), pltpu.VMEM((tm, D_MODEL), F32)],
        compiler_params=_cparams("arbitrary", "arbitrary"),
        name="ffn",
    )(x, mod, mod, mod, w_up, w_down, ln_g, ln_b)


def _rope_tables(l):
    pos = jnp.arange(l, dtype=jnp.int32)
    row = (pos // GRID_W).astype(F32)
    col = (pos % GRID_W).astype(F32)
    half = HEAD // 2
    inv_freq = ROPE_THETA ** (-jnp.arange(0, half, 2, dtype=F32) / half)
    ang_r = row[:, None] * inv_freq[None, :]
    ang_c = col[:, None] * inv_freq[None, :]
    cos = jnp.concatenate([jnp.cos(ang_r), jnp.cos(ang_r), jnp.cos(ang_c), jnp.cos(ang_c)], axis=-1)
    sin = jnp.concatenate([-jnp.sin(ang_r), jnp.sin(ang_r), -jnp.sin(ang_c), jnp.sin(ang_c)], axis=-1)
    return jnp.tile(cos, (1, LANES // HEAD)), jnp.tile(sin, (1, LANES // HEAD))


def _head_block_matrix(width, value):
    idx = jnp.arange(width) // HEAD
    return jnp.where(idx[:, None] == idx[None, :], value, 0.0).astype(BF16)


def _pair_states(s):
    b = s.shape[0]
    s = s.reshape(b, 2, N_HEADS // 2, 2, HEAD, HEAD)
    z = jnp.zeros_like(s[:, :, :, 0])
    top = jnp.concatenate([z, s[:, :, :, 1]], axis=-1)
    bot = jnp.concatenate([s[:, :, :, 0], z], axis=-1)
    return jnp.concatenate([top, bot], axis=-2)


def _unpair_states(sp):
    b = sp.shape[0]
    s = jnp.stack([sp[..., HEAD:, :HEAD], sp[..., :HEAD, HEAD:]], axis=3)
    return s.reshape(b, 2, N_HEADS, HEAD, HEAD)


def _pick_tile(pref, *sizes):
    return min(pref, functools.reduce(math.gcd, sizes))


def kernel(x_prompt, x_sample, cache_k, cache_v, state_wkv, c, c_ctx, w_ada, b_ada, w_in, sgu_ln_g, sgu_ln_b, sgu_w, sgu_b, q_norm, k_norm, rwkv_mu, rwkv_w0, rwkv_w2, rwkv_a0, rwkv_a2, rwkv_k_k, rwkv_k_a, rwkv_r_k, rwkv_g2, rwkv_lnx_g, rwkv_lnx_b, w_branch, w_out, ln1_g, ln1_b, w_up, w_down, ln2_g, ln2_b):
    depth = w_in.shape[0]
    b_ctx, l_ctx, _ = x_prompt.shape
    b_lat, l_lat, _ = x_sample.shape
    past = cache_k.shape[2]
    t_ctx = b_ctx * l_ctx
    t_lat = b_lat * l_lat
    alpha = (2 * depth) ** 0.25

    tm = _pick_tile(512, t_ctx, l_lat)
    tm_ffn = _pick_tile(1024, t_ctx, l_lat)
    tk = _pick_tile(512, l_ctx, l_lat)
    tq_ctx = _pick_tile(256, l_ctx)
    tq_lat = _pick_tile(512, l_lat)
    c_ctx_chunk = _pick_tile(128, l_ctx)
    c_lat_chunk = _pick_tile(128, l_lat)

    n_rows = 1 + b_lat
    pad_rows = -n_rows % 16
    cvec = jnp.concatenate([c_ctx[None, :], c, jnp.zeros((pad_rows, D_MODEL), F32)], axis=0)
    mod_all = _ada(cvec, w_ada, b_ada[:, None, :])

    c_lo = SEG_A + SEG_B
    w_a_b = w_in[:, :, :SEG_A].astype(BF16)
    w_g_b = w_in[:, :, c_lo + SEG_C:].astype(BF16)
    w_bc_b = jnp.concatenate([w_in[:, :, c_lo:c_lo + SEG_C], jnp.zeros((depth, D_MODEL, B_OFF - SEG_C), F32),
                              w_in[:, :, SEG_A:c_lo]], axis=-1).astype(BF16)
    w_branch_b = w_branch.astype(BF16)
    w_out_b = w_out.astype(BF16)
    w_up_b = w_up.astype(BF16)
    w_down_b = w_down.astype(BF16)
    sgu_w_b = sgu_w.astype(BF16)
    g2_b = rwkv_g2.astype(BF16)
    seg_mean2 = _head_block_matrix(KV_W, 1.0 / HEAD)
    seg_mean8 = _head_block_matrix(HALF, 1.0 / HEAD)
    seg_ones8 = _head_block_matrix(HALF, 1.0)
    cos_ctx, sin_ctx = _rope_tables(l_ctx)
    cos_lat, sin_lat = _rope_tables(l_lat)
    zeros_lora = jnp.zeros((depth, 2, LORA, HALF), F32)
    wwa = jnp.concatenate([jnp.concatenate([rwkv_w2, zeros_lora], axis=-1),
                           jnp.concatenate([zeros_lora, rwkv_a2], axis=-1)], axis=-2).astype(BF16)

    xs = (x_prompt.reshape(t_ctx, D_MODEL), x_sample.reshape(t_lat, D_MODEL))
    b_s_full = jnp.repeat(jnp.swapaxes(sgu_b, 1, 2), HALF // SGU_GROUPS, axis=2)
    vec = lambda p: p[:, None, :]
    new_k, new_v, new_s = [], [], []
    for l in range(depth):
        mod = mod_all[l][:, None, :]
        o_a, seg_b = _proj(xs, mod, w_a_b, w_bc_b, vec(sgu_ln_g), vec(sgu_ln_b), sgu_w_b, b_s_full, l, tm, t_ctx, l_lat)
        seg_c = seg_b

        qn2 = jnp.tile(q_norm[l], LANES // HEAD)[None]
        kn2 = jnp.tile(k_norm[l], KV_W // HEAD)[None]
        kn_ctx, kr_ctx, vb_ctx = _kvprep(seg_b, 0, b_ctx, l_ctx, kn2, seg_mean2, cos_ctx, sin_ctx, False, tk)
        kr_lat, vb_lat = _kvprep(seg_b, t_ctx, b_lat, l_lat, kn2, seg_mean2, cos_lat, sin_lat, True, tk)
        new_k.append(kn_ctx.reshape(b_ctx, l_ctx, N_KV, HEAD))
        new_v.append(seg_b[:t_ctx, B_OFF + HALF + KV_W:].astype(F32).reshape(b_ctx, l_ctx, N_KV, HEAD))
        kt_lat = jnp.concatenate([jnp.swapaxes(cache_k[:, l].reshape(b_lat, past, KV_W), 1, 2).astype(BF16),
                                  kr_lat], axis=2)
        v_lat = jnp.concatenate([cache_v[:, l].reshape(b_lat, past, KV_W).astype(BF16),
                                 vb_lat.reshape(b_lat, l_lat, KV_W)], axis=1)
        o_b = (_attention(seg_b, 0, b_ctx, l_ctx, kr_ctx, vb_ctx.reshape(b_ctx, l_ctx, KV_W),
                          qn2, seg_mean2, cos_ctx, sin_ctx, False, tq_ctx),
               _attention(seg_b, t_ctx, b_lat, l_lat, kt_lat, v_lat, qn2, seg_mean2, cos_lat, sin_lat, True, tq_lat))

        rw = dict(layer=l, mu=rwkv_mu[:, :, None, :], w0=rwkv_w0[:, :, None, :], a0=rwkv_a0[:, :, None, :], wwa=wwa,
                  k_k=vec(rwkv_k_k), k_a=vec(rwkv_k_a), r_k=rwkv_r_k.reshape(depth, 1, HALF), seg1=seg_ones8)
        y0c, y1c, s_ctx = _wkv(seg_c, 0, b_ctx, l_ctx, c_ctx_chunk, None, **rw)
        y0l, y1l, _ = _wkv(seg_c, t_ctx, b_lat, l_lat, c_lat_chunk, _pair_states(state_wkv[:, l]), **rw)
        new_s.append(_unpair_states(s_ctx))
        post = functools.partial(_rwkv_post, segm=seg_mean8, g2=g2_b, lnx_g=vec(rwkv_lnx_g), lnx_b=vec(rwkv_lnx_b),
                                 l=l, tm=tm)
        o_c = (post(y0c, y1c, seg_c, 0), post(y0l, y1l, seg_c, t_ctx))

        x = _merge(xs, o_a, o_b, o_c, mod, w_g_b, w_branch_b, w_out_b, vec(ln1_g), vec(ln1_b), l, tm, t_ctx, l_lat, alpha)
        x = _ffn(x, mod, w_up_b, w_down_b, vec(ln2_g), vec(ln2_b), l, tm_ffn, 1024, t_ctx, l_lat, alpha,
                 split=(l == depth - 1))
        xs = (x,)

    y = x[0].reshape(b_ctx, l_ctx, D_MODEL)
    z = x[1].reshape(b_lat, l_lat, D_MODEL)
    return (y, z, jnp.stack(new_k, axis=1), jnp.stack(new_v, axis=1), jnp.stack(new_s, axis=1))
```
